```python
import math
import jax
import jax.numpy as jnp
from jax import lax
import numpy as np

D_MODEL = 1024
BATCH = 8
SEQ = 2048
DEPTH = 4
DEC_BATCH = 128
DEC_SEQ = 8
PAST_LEN = 16384
PAGE_SIZE = 128

BRANCH_W = D_MODEL
N_BRANCH = 3
D_FF = ((8 * D_MODEL // 3 + 127) // 128) * 128
CHUNK = 128
EPS = 1e-6
RET_HEADS = 8
RET_DK = BRANCH_W // RET_HEADS
RET_DV = BRANCH_W // RET_HEADS
RET_GN_EPS = 1e-5
ROPE_BASE = 10000.0
SSM_HEADDIM = 64
SSM_HEADS = BRANCH_W // SSM_HEADDIM
SSM_GROUPS = 2
SSM_STATE = 128
CONV_W = 4
CONV_DIM = BRANCH_W + 2 * SSM_GROUPS * SSM_STATE
RWKV_HEADDIM = 64
RWKV_HEADS = BRANCH_W // RWKV_HEADDIM
LORA_W = 64
LORA_A = 64
LORA_G = 128
RWKV_PROJ = 3 * BRANCH_W + LORA_W + LORA_A + LORA_G
RWKV_GN_EPS = 64e-5
N_MEM = 256
X_HEADS = 4
X_HEAD_DIM = D_MODEL // X_HEADS
RET_PROJ = 4 * BRANCH_W
SSM_PROJ = BRANCH_W + CONV_DIM + SSM_HEADS
GATE_PROJ = N_BRANCH * D_MODEL
PROJ = RET_PROJ + SSM_PROJ + RWKV_PROJ + GATE_PROJ
PROJ_SPLITS = [RET_PROJ, RET_PROJ + SSM_PROJ, RET_PROJ + SSM_PROJ + RWKV_PROJ]
SSM_SPLITS = [BRANCH_W, BRANCH_W + CONV_DIM]
RWKV_SPLITS = [BRANCH_W, 2 * BRANCH_W, 3 * BRANCH_W, 3 * BRANCH_W + LORA_W, 3 * BRANCH_W + LORA_W + LORA_A]

kernel_name = 'hybrid_retention_ssd_rwkv7_decoder_step'


def rmsnorm(x, g):
    xf = x.astype(jnp.float32)
    y = xf * lax.rsqrt(jnp.mean(xf * xf, axis=-1, keepdims=True) + EPS)
    return (y * g).astype(x.dtype)


def head_norm(x, eps):
    mu = jnp.mean(x, axis=-1, keepdims=True)
    xc = x - mu
    return xc * lax.rsqrt(jnp.mean(xc * xc, axis=-1, keepdims=True) + eps)


def swiglu(h, w1, w3, w2):
    return (jax.nn.silu(h @ w1) * (h @ w3)) @ w2


def rotary(x, pos):
    half = x.shape[-1] // 2
    freqs = ROPE_BASE ** (-jnp.arange(half, dtype=jnp.float32) / half)
    ang = pos[:, None] * freqs[None, :]
    cos = jnp.cos(ang)[None, :, None, :]
    sin = jnp.sin(ang)[None, :, None, :]
    x1, x2 = x[..., :half], x[..., half:]
    return jnp.concatenate([x1 * cos - x2 * sin, x1 * sin + x2 * cos], axis=-1)


def to_chunks(a, c):
    b, t = a.shape[0], a.shape[1]
    return jnp.moveaxis(a.reshape(b, t // c, c, *a.shape[2:]), 1, 0)


def from_chunks(a):
    a = jnp.moveaxis(a, 0, 1)
    return a.reshape(a.shape[0], a.shape[1] * a.shape[2], *a.shape[3:])


def retention_chunked(q, k, v, s0):
    t, h = q.shape[1], q.shape[2]
    c = min(CHUNK, t)
    log_g = jnp.log1p(-jnp.exp2(-5.0 - jnp.arange(h, dtype=jnp.float32)))
    idx = jnp.arange(c, dtype=jnp.float32)
    diff = idx[:, None] - idx[None, :]
    dmat = jnp.where(diff >= 0, jnp.exp(jnp.maximum(diff, 0.0)[None] * log_g[:, None, None]), 0.0)
    q_dec = jnp.exp((idx + 1.0)[:, None] * log_g[None, :])
    k_dec = jnp.exp((c - 1.0 - idx)[:, None] * log_g[None, :])
    s_dec = jnp.exp(c * log_g)

    def step(s, inp):
        qc, kc, vc = inp
        scores = jnp.einsum('bihd,bjhd->bhij', qc, kc) * dmat
        o = jnp.einsum('bhij,bjhv->bihv', scores, vc)
        o = o + jnp.einsum('bihd,bhdv->bihv', qc, s) * q_dec[None, :, :, None]
        s = s * s_dec[None, :, None, None] + jnp.einsum('bjhd,jh,bjhv->bhdv', kc, k_dec, vc)
        return s, o

    s, o = lax.scan(step, s0, (to_chunks(q, c), to_chunks(k, c), to_chunks(v, c)))
    return from_chunks(o), s


def ssd_chunked(x, dt, a, bm, cm, h0):
    b, t, h, p = x.shape
    g, n = bm.shape[2], bm.shape[3]
    hg = h // g
    c = min(CHUNK, t)
    x = x.reshape(b, t, g, hg, p)
    dt = dt.reshape(b, t, g, hg)
    a = a.reshape(g, hg)
    state0 = h0.reshape(b, g, hg, p, n)
    mask = jnp.tril(jnp.ones((c, c), dtype=bool))

    def step(state, inp):
        xc, dtc, bc, cc = inp
        cum = jnp.cumsum(dtc * a, axis=1)
        cum_t = jnp.moveaxis(cum, 1, -1)
        seg = cum_t[..., :, None] - cum_t[..., None, :]
        lmat = jnp.where(mask, jnp.exp(jnp.where(mask, seg, 0.0)), 0.0)
        cb = jnp.einsum('bign,bjgn->bgij', cc, bc)
        scores = cb[:, :, None] * lmat * jnp.moveaxis(dtc, 1, -1)[..., None, :]
        y = jnp.einsum('bghij,bjghp->bighp', scores, xc)
        y = y + jnp.einsum('bign,bghpn->bighp', cc, state) * jnp.exp(cum)[..., None]
        last = cum[:, -1]
        wts = jnp.exp(last[:, None] - cum) * dtc
        state = state * jnp.exp(last)[..., None, None] + jnp.einsum('bjgn,bjgh,bjghp->bghpn', bc, wts, xc)
        return state, y

    state, y = lax.scan(step, state0, (to_chunks(x, c), to_chunks(dt, c), to_chunks(bm, c), to_chunks(cm, c)))
    return from_chunks(y).reshape(b, t, h, p), state.reshape(b, h, p, n)


def rwkv7_scan(r, w, k, v, kk, bb, s0):
    def step(s, inp):
        rt, wt, kt, vt, kkt, bt = inp
        sa = jnp.einsum('bhvk,bhk->bhv', s, kkt)
        s = s * wt[:, :, None, :] - sa[..., None] * bt[:, :, None, :] + vt[..., None] * kt[:, :, None, :]
        return s, jnp.einsum('bhvk,bhk->bhv', s, rt)

    xs = tuple(jnp.moveaxis(z, 1, 0) for z in (r, w, k, v, kk, bb))
    s, y = lax.scan(step, s0, xs)
    return jnp.moveaxis(y, 0, 1), s


def mixer(h, l, p, st, offset):
    ret_s, ssm_s, conv_s, rwkv_s, shift_s = st
    bsz, t, _ = h.shape
    odt = h.dtype
    f32 = jnp.float32
    proj = h @ p['w_in'][l]
    ret_cols, ssm_cols, rwkv_cols, gate_cols = jnp.split(proj, PROJ_SPLITS, axis=-1)

    q, k, v, g_ret = jnp.split(ret_cols.astype(f32), 4, axis=-1)
    pos = jnp.float32(offset) + jnp.arange(t, dtype=f32)
    q = rotary(q.reshape(bsz, t, RET_HEADS, RET_DK), pos) * (RET_DK ** -0.5)
    k = rotary(k.reshape(bsz, t, RET_HEADS, RET_DK), pos)
    v = v.reshape(bsz, t, RET_HEADS, RET_DV)
    o, ret_new = retention_chunked(q, k, v, ret_s.astype(f32))
    out_a = jax.nn.silu(g_ret) * head_norm(o, RET_GN_EPS).reshape(bsz, t, BRANCH_W)

    z, xbc, dt_raw = jnp.split(ssm_cols, SSM_SPLITS, axis=-1)
    full = jnp.concatenate([conv_s.astype(odt), xbc], axis=1)
    conv_new = full[:, t:]
    cw = p['ssm_conv_w'][l]
    xbc = sum(full[:, j:j + t] * cw[j] for j in range(CONV_W)) + p['ssm_conv_b'][l]
    xbc = jax.nn.silu(xbc.astype(f32))
    xs, bm, cm = jnp.split(xbc, [BRANCH_W, BRANCH_W + SSM_GROUPS * SSM_STATE], axis=-1)
    xs = xs.reshape(bsz, t, SSM_HEADS, SSM_HEADDIM)
    bm = bm.reshape(bsz, t, SSM_GROUPS, SSM_STATE)
    cm = cm.reshape(bsz, t, SSM_GROUPS, SSM_STATE)
    dt = jax.nn.softplus(dt_raw.astype(f32) + p['ssm_dt_bias'][l])
    a = -jnp.exp(p['ssm_a_log'][l].astype(f32))
    y, ssm_new = ssd_chunked(xs, dt, a, bm, cm, ssm_s.astype(f32))
    y = (y + p['ssm_d'][l][:, None] * xs).reshape(bsz, t, BRANCH_W)
    out_b = rmsnorm(y * jax.nn.silu(z.astype(f32)), p['ssm_norm'][l])

    prev = jnp.concatenate([shift_s.astype(odt), rwkv_cols], axis=1)[:, :t]
    shift_new = rwkv_cols[:, t - 1:]
    mixed = (rwkv_cols + (prev - rwkv_cols) * p['rwkv_mu'][l]).astype(f32)
    r, kc, vc, wd, ad, gd = jnp.split(mixed, RWKV_SPLITS, axis=-1)
    w_log = -jax.nn.softplus(-(p['rwkv_w0'][l] + jnp.tanh(wd) @ p['rwkv_w_w2'][l])) - 0.5
    decay = jnp.exp(-jnp.exp(w_log))
    a_lr = jax.nn.sigmoid(p['rwkv_a0'][l] + ad @ p['rwkv_w_a2'][l])
    g_rw = jax.nn.sigmoid(gd) @ p['rwkv_w_g2'][l]
    kk = kc * p['rwkv_k_k'][l]
    kc = kc * (1.0 + (a_lr - 1.0) * p['rwkv_k_a'][l])
    shp = (bsz, t, RWKV_HEADS, RWKV_HEADDIM)
    kk = kk.reshape(shp)
    kk = kk / jnp.maximum(jnp.sqrt(jnp.sum(kk * kk, axis=-1, keepdims=True)), 1e-12)
    r, kc, vc, decay, a_lr = (z_.reshape(shp) for z_ in (r, kc, vc, decay, a_lr))
    yr, rwkv_new = rwkv7_scan(r, decay, kc, vc, kk, kk * a_lr, rwkv_s.astype(f32))
    bonus = jnp.sum(r * kc * p['rwkv_r_k'][l], axis=-1, keepdims=True) * vc
    yr = head_norm(yr, RWKV_GN_EPS).reshape(bsz, t, BRANCH_W) * p['rwkv_lnx_g'][l] + p['rwkv_lnx_b'][l]
    out_c = (yr + bonus.reshape(bsz, t, BRANCH_W)) * g_rw

    branches = jnp.stack([out_a, out_b, out_c], axis=2).astype(odt)
    gates = jax.nn.sigmoid(gate_cols.reshape(bsz, t, N_BRANCH, D_MODEL))
    merged = jnp.sum(gates * jnp.einsum('btnc,ncd->btnd', branches, p['w_branch'][l]), axis=2)
    out = merged @ p['w_out'][l]
    new_st = (ret_new.astype(odt), ssm_new.astype(odt), conv_new, rwkv_new.astype(odt), shift_new)
    return out, new_st


def mem_kv(mem, l, p):
    m = rmsnorm(mem, p['mem_norm'][l])
    b = mem.shape[0]
    mk = (m @ p['xattn_wk'][l]).reshape(b, N_MEM, X_HEADS, X_HEAD_DIM)
    mv = (m @ p['xattn_wv'][l]).reshape(b, N_MEM, X_HEADS, X_HEAD_DIM)
    return mk, mv


def cross_attn(h, mk, mv, wq, wo):
    bsz, t, _ = h.shape
    q = (h @ wq).reshape(bsz, t, X_HEADS, X_HEAD_DIM)
    s = jnp.einsum('bthd,bmhd->bhtm', q, mk.astype(h.dtype)).astype(jnp.float32) * (X_HEAD_DIM ** -0.5)
    pr = jax.nn.softmax(s, axis=-1).astype(h.dtype)
    o = jnp.einsum('bhtm,bmhd->bthd', pr, mv.astype(h.dtype)).reshape(bsz, t, D_MODEL)
    return o @ wo


def layer(x, l, p, st, mk, mv, offset):
    x = x + 0.5 * swiglu(rmsnorm(x, p['ffa_norm'][l]), p['ffa_w1'][l], p['ffa_w3'][l], p['ffa_w2'][l])
    out, new_st = mixer(rmsnorm(x, p['mix_norm'][l]), l, p, st, offset)
    x = x + out
    x = x + cross_attn(rmsnorm(x, p['xattn_norm'][l]), mk, mv, p['xattn_wq'][l], p['xattn_wo'][l])
    x = x + 0.5 * swiglu(rmsnorm(x, p['ffb_norm'][l]), p['ffb_w1'][l], p['ffb_w3'][l], p['ffb_w2'][l])
    return x, new_st


def trunk(x, p, states, mems, offset):
    new = []
    for l in range(DEPTH):
        st = tuple(s[l] for s in states)
        x, ns = layer(x, l, p, st, mems[l][0], mems[l][1], offset)
        new.append(ns)
    stacked = tuple(jnp.stack([ns[i] for ns in new]) for i in range(5))
    return rmsnorm(x, p['final_norm']), stacked


def setup_inputs(seed: int = 0) -> dict:
    key = jax.random.key(seed)
    ks = iter(jax.random.split(key, 64))
    L = DEPTH

    def nrm(shape, scale):
        return jax.random.normal(next(ks), shape, jnp.float32) * scale

    def gain(shape):
        return 1.0 + nrm(shape, 0.02)

    dt0 = jnp.exp(jax.random.uniform(next(ks), (L, SSM_HEADS), jnp.float32, math.log(1e-3), math.log(1e-1)))
    return {
        'x_prompt': nrm((BATCH, SEQ, D_MODEL), 1.0),
        'x_sample': nrm((DEC_BATCH, DEC_SEQ, D_MODEL), 1.0),
        'mem_prompt': nrm((BATCH, N_MEM, D_MODEL), 1.0),
        'state_ret': nrm((L, DEC_BATCH, RET_HEADS, RET_DK, RET_DV), 1.0),
        'state_ssm': nrm((L, DEC_BATCH, SSM_HEADS, SSM_HEADDIM, SSM_STATE), 0.5),
        'state_conv': nrm((L, DEC_BATCH, CONV_W - 1, CONV_DIM), 1.0),
        'state_rwkv': nrm((L, DEC_BATCH, RWKV_HEADS, RWKV_HEADDIM, RWKV_HEADDIM), 0.5),
        'state_shift': nrm((L, DEC_BATCH, 1, RWKV_PROJ), 1.0),
        'cache_mem_k': nrm((L, DEC_BATCH, N_MEM, X_HEADS, X_HEAD_DIM), 1.0),
        'cache_mem_v': nrm((L, DEC_BATCH, N_MEM, X_HEADS, X_HEAD_DIM), 1.0),
        'ffa_norm': gain((L, D_MODEL)),
        'ffa_w1': nrm((L, D_MODEL, D_FF), D_MODEL ** -0.5),
        'ffa_w3': nrm((L, D_MODEL, D_FF), D_MODEL ** -0.5),
        'ffa_w2': nrm((L, D_FF, D_MODEL), D_FF ** -0.5),
        'mix_norm': gain((L, D_MODEL)),
        'w_in': nrm((L, D_MODEL, PROJ), D_MODEL ** -0.5),
        'ssm_conv_w': nrm((L, CONV_W, CONV_DIM), CONV_W ** -0.5),
        'ssm_conv_b': nrm((L, CONV_DIM), 0.01),
        'ssm_dt_bias': dt0 + jnp.log(-jnp.expm1(-dt0)),
        'ssm_a_log': jnp.log(jax.random.uniform(next(ks), (L, SSM_HEADS), jnp.float32, 1.0, 16.0)),
        'ssm_d': gain((L, SSM_HEADS)),
        'ssm_norm': gain((L, BRANCH_W)),
        'rwkv_mu': jax.random.uniform(next(ks), (L, RWKV_PROJ), jnp.float32),
        'rwkv_w0': jax.random.uniform(next(ks), (L, BRANCH_W), jnp.float32, -6.0, -1.0),
        'rwkv_w_w2': nrm((L, LORA_W, BRANCH_W), 0.1 * LORA_W ** -0.5),
        'rwkv_a0': nrm((L, BRANCH_W), 0.1),
        'rwkv_w_a2': nrm((L, LORA_A, BRANCH_W), 0.1 * LORA_A ** -0.5),
        'rwkv_w_g2': nrm((L, LORA_G, BRANCH_W), LORA_G ** -0.5),
        'rwkv_k_k': 0.85 + nrm((L, BRANCH_W), 0.02),
        'rwkv_k_a': gain((L, BRANCH_W)),
        'rwkv_r_k': nrm((L, RWKV_HEADS, RWKV_HEADDIM), 0.1),
        'rwkv_lnx_g': gain((L, BRANCH_W)),
        'rwkv_lnx_b': nrm((L, BRANCH_W), 0.01),
        'w_branch': nrm((L, N_BRANCH, BRANCH_W, D_MODEL), BRANCH_W ** -0.5),
        'w_out': nrm((L, D_MODEL, D_MODEL), D_MODEL ** -0.5),
        'xattn_norm': gain((L, D_MODEL)),
        'mem_norm': gain((L, D_MODEL)),
        'xattn_wq': nrm((L, D_MODEL, X_HEADS * X_HEAD_DIM), D_MODEL ** -0.5),
        'xattn_wk': nrm((L, D_MODEL, X_HEADS * X_HEAD_DIM), D_MODEL ** -0.5),
        'xattn_wv': nrm((L, D_MODEL, X_HEADS * X_HEAD_DIM), D_MODEL ** -0.5),
        'xattn_wo': nrm((L, X_HEADS * X_HEAD_DIM, D_MODEL), D_MODEL ** -0.5),
        'ffb_norm': gain((L, D_MODEL)),
        'ffb_w1': nrm((L, D_MODEL, D_FF), D_MODEL ** -0.5),
        'ffb_w3': nrm((L, D_MODEL, D_FF), D_MODEL ** -0.5),
        'ffb_w2': nrm((L, D_FF, D_MODEL), D_FF ** -0.5),
        'final_norm': gain((D_MODEL,)),
    }


def reference(x_prompt, x_sample, mem_prompt, state_ret, state_ssm, state_conv, state_rwkv, state_shift,
              cache_mem_k, cache_mem_v, ffa_norm, ffa_w1, ffa_w3, ffa_w2, mix_norm, w_in,
              ssm_conv_w, ssm_conv_b, ssm_dt_bias, ssm_a_log, ssm_d, ssm_norm,
              rwkv_mu, rwkv_w0, rwkv_w_w2, rwkv_a0, rwkv_w_a2, rwkv_w_g2, rwkv_k_k, rwkv_k_a, rwkv_r_k,
              rwkv_lnx_g, rwkv_lnx_b, w_branch, w_out, xattn_norm, mem_norm, xattn_wq, xattn_wk, xattn_wv,
              xattn_wo, ffb_norm, ffb_w1, ffb_w3, ffb_w2, final_norm):
    p = {
        'ffa_norm': ffa_norm, 'ffa_w1': ffa_w1, 'ffa_w3': ffa_w3, 'ffa_w2': ffa_w2,
        'mix_norm': mix_norm, 'w_in': w_in,
        'ssm_conv_w': ssm_conv_w, 'ssm_conv_b': ssm_conv_b, 'ssm_dt_bias': ssm_dt_bias,
        'ssm_a_log': ssm_a_log, 'ssm_d': ssm_d, 'ssm_norm': ssm_norm,
        'rwkv_mu': rwkv_mu, 'rwkv_w0': rwkv_w0, 'rwkv_w_w2': rwkv_w_w2, 'rwkv_a0': rwkv_a0,
        'rwkv_w_a2': rwkv_w_a2, 'rwkv_w_g2': rwkv_w_g2, 'rwkv_k_k': rwkv_k_k, 'rwkv_k_a': rwkv_k_a,
        'rwkv_r_k': rwkv_r_k, 'rwkv_lnx_g': rwkv_lnx_g, 'rwkv_lnx_b': rwkv_lnx_b,
        'w_branch': w_branch, 'w_out': w_out,
        'xattn_norm': xattn_norm, 'mem_norm': mem_norm, 'xattn_wq': xattn_wq, 'xattn_wk': xattn_wk,
        'xattn_wv': xattn_wv, 'xattn_wo': xattn_wo,
        'ffb_norm': ffb_norm, 'ffb_w1': ffb_w1, 'ffb_w3': ffb_w3, 'ffb_w2': ffb_w2,
        'final_norm': final_norm,
    }
    b = x_prompt.shape[0]
    dt_p = x_prompt.dtype
    zero_states = (
        jnp.zeros((DEPTH, b, RET_HEADS, RET_DK, RET_DV), dt_p),
        jnp.zeros((DEPTH, b, SSM_HEADS, SSM_HEADDIM, SSM_STATE), dt_p),
        jnp.zeros((DEPTH, b, CONV_W - 1, CONV_DIM), dt_p),
        jnp.zeros((DEPTH, b, RWKV_HEADS, RWKV_HEADDIM, RWKV_HEADDIM), dt_p),
        jnp.zeros((DEPTH, b, 1, RWKV_PROJ), dt_p),
    )
    mems_p = [mem_kv(mem_prompt, l, p) for l in range(DEPTH)]
    p_mem_k = jnp.stack([m[0] for m in mems_p])
    p_mem_v = jnp.stack([m[1] for m in mems_p])
    y_prompt, (p_ret, p_ssm, p_conv, p_rwkv, p_shift) = trunk(x_prompt, p, zero_states, mems_p, 0)

    sample_states = (state_ret, state_ssm, state_conv, state_rwkv, state_shift)
    mems_s = [(cache_mem_k[l], cache_mem_v[l]) for l in range(DEPTH)]
    y_sample, (s_ret, s_ssm, s_conv, s_rwkv, s_shift) = trunk(x_sample, p, sample_states, mems_s, PAST_LEN)

    return (y_prompt, y_sample, p_ret, p_ssm, p_conv, p_rwkv, p_shift, p_mem_k, p_mem_v,
            s_ret, s_ssm, s_conv, s_rwkv, s_shift)
```

```python
import functools
import math

import numpy as np
import jax
import jax.numpy as jnp
from jax import lax
from jax.experimental import pallas as pl
from jax.experimental.pallas import tpu as pltpu

F32 = jnp.float32
BF16 = jnp.bfloat16

D_MODEL = 1024
DEPTH = 4
PAST_LEN = 16384
D_FF = 2816
EPS = 1e-6
RET_HEADS = 8
RET_DK = 128
RET_GN_EPS = 1e-5
ROPE_BASE = 10000.0
SSM_HEADS = 16
SSM_HEADDIM = 64
SSM_GROUPS = 2
SSM_STATE = 128
CONV_W = 4
CONV_DIM = 1536
RWKV_HEADS = 16
RWKV_HD = 64
RWKV_PROJ = 3328
RWKV_GN_EPS = 64e-5
N_MEM = 256
X_HEADS = 4
X_HEAD_DIM = 256
RET_PROJ = 4096
SSM_PROJ = 2576
SSM_PROJ_PAD = 2688
GATE_PROJ = 3072
LANES = 128
VMEM_LIMIT = 56 * 1024 * 1024

RET_CHUNK = 128
SSD_CHUNK = 128
RWKV_CHUNK = 64
ATTN_ROWS = 512
SEQ_BLOCK_ROWS = 64


def _params(sem):
    return pltpu.CompilerParams(dimension_semantics=sem, vmem_limit_bytes=VMEM_LIMIT)


def _rms(x, g):
    return x * lax.rsqrt(jnp.mean(x * x, axis=-1, keepdims=True) + EPS) * g


def _head_norm(x, eps):
    mu = jnp.mean(x, axis=-1, keepdims=True)
    xc = x - mu
    return xc * lax.rsqrt(jnp.mean(xc * xc, axis=-1, keepdims=True) + eps)


def _silu(x):
    return x * jax.nn.sigmoid(x)


def _dot(a, b):
    return jnp.dot(a.astype(BF16), b.astype(BF16), preferred_element_type=F32)


def _dot_nt(a, b):
    return lax.dot_general(a.astype(BF16), b.astype(BF16), (((1,), (1,)), ((), ())),
                           preferred_element_type=F32)


def _dot_tn(a, b):
    return lax.dot_general(a.astype(BF16), b.astype(BF16), (((0,), (0,)), ((), ())),
                           preferred_element_type=F32)


def _dot_f32(a, b):
    return jnp.dot(a, b, precision=lax.Precision.HIGHEST, preferred_element_type=F32)


def _seq_blocking(bsz, t, chunk_max, block_rows):
    if t > chunk_max:
        assert t % chunk_max == 0
        return 1, chunk_max
    nb = max(1, min(bsz, block_rows // t))
    assert bsz % nb == 0
    return nb, t


def _seq_masks(nb, c):
    r = nb * c
    i = np.arange(r)
    same = (i[:, None] // c) == (i[None, :] // c)
    incl = same & (i[:, None] >= i[None, :])
    strict = same & (i[:, None] > i[None, :])
    return same, incl, strict


def _nm_kernel(x_ref, g_ref, w_ref, o_ref, h_ref):
    @pl.when(pl.program_id(1) == 0)
    def _():
        h_ref[...] = _rms(x_ref[...], g_ref[...]).astype(BF16)

    o_ref[...] = jnp.dot(h_ref[...], w_ref[...], preferred_element_type=F32)


def _norm_matmul(x, g, l, w, col0, ncols):
    m, d = x.shape
    if ncols % 1024 == 0:
        tn = 1024
        tm = min(m, 1024)
    else:
        tn = ncols
        tm = min(m, 512)
    assert col0 % tn == 0 and m % tm == 0
    cb = col0 // tn
    return pl.pallas_call(
        _nm_kernel,
        grid=(m // tm, ncols // tn),
        in_specs=[
            pl.BlockSpec((tm, d), lambda i, j: (i, 0)),
            pl.BlockSpec((None, 1, d), lambda i, j: (l, 0, 0)),
            pl.BlockSpec((None, d, tn), lambda i, j: (l, 0, cb + j)),
        ],
        out_specs=pl.BlockSpec((tm, tn), lambda i, j: (i, j)),
        out_shape=jax.ShapeDtypeStruct((m, ncols), F32),
        scratch_shapes=[pltpu.VMEM((tm, d), BF16)],
        compiler_params=_params(("parallel", "arbitrary")),
        name="norm_matmul",
    )(x, g, w)


def _mr_kernel(a_ref, w_ref, x_ref, o_ref):
    o_ref[...] = x_ref[...] + jnp.dot(a_ref[...].astype(BF16), w_ref[...],
                                       preferred_element_type=F32)


def _matmul_residual(a, l, w, x):
    m, k = a.shape
    n = w.shape[-1]
    tm = min(m, 512)
    return pl.pallas_call(
        _mr_kernel,
        grid=(m // tm,),
        in_specs=[
            pl.BlockSpec((tm, k), lambda i: (i, 0)),
            pl.BlockSpec((None, k, n), lambda i: (l, 0, 0)),
            pl.BlockSpec((tm, n), lambda i: (i, 0)),
        ],
        out_specs=pl.BlockSpec((tm, n), lambda i: (i, 0)),
        out_shape=jax.ShapeDtypeStruct((m, n), F32),
        compiler_params=_params(("parallel",)),
        name="matmul_residual",
    )(a, w, x)


def _ffn_kernel(x_ref, g_ref, w1_ref, w3_ref, w2_ref, o_ref, h_ref, acc_ref):
    j = pl.program_id(1)

    @pl.when(j == 0)
    def _():
        h_ref[...] = _rms(x_ref[...], g_ref[...]).astype(BF16)
        acc_ref[...] = jnp.zeros_like(acc_ref)

    h = h_ref[...]
    a = jnp.dot(h, w1_ref[...], preferred_element_type=F32)
    b = jnp.dot(h, w3_ref[...], preferred_element_type=F32)
    u = (_silu(a) * b).astype(BF16)
    acc_ref[...] += jnp.dot(u, w2_ref[...], preferred_element_type=F32)

    @pl.when(j == pl.num_programs(1) - 1)
    def _():
        o_ref[...] = x_ref[...] + 0.5 * acc_ref[...]


def _ffn(x, g, l, w1, w3, w2):
    m, d = x.shape
    f = w1.shape[-1]
    tm = min(m, 1024)
    tf = 256
    return pl.pallas_call(
        _ffn_kernel,
        grid=(m // tm, f // tf),
        in_specs=[
            pl.BlockSpec((tm, d), lambda i, j: (i, 0)),
            pl.BlockSpec((None, 1, d), lambda i, j: (l, 0, 0)),
            pl.BlockSpec((None, d, tf), lambda i, j: (l, 0, j)),
            pl.BlockSpec((None, d, tf), lambda i, j: (l, 0, j)),
            pl.BlockSpec((None, tf, d), lambda i, j: (l, j, 0)),
        ],
        out_specs=pl.BlockSpec((tm, d), lambda i, j: (i, 0)),
        out_shape=jax.ShapeDtypeStruct((m, d), F32),
        scratch_shapes=[pltpu.VMEM((tm, d), BF16), pltpu.VMEM((tm, d), F32)],
        compiler_params=_params(("parallel", "arbitrary")),
        name="ffn",
    )(x, g, w1, w3, w2)


def _norm_kernel(x_ref, g_ref, o_ref):
    o_ref[...] = _rms(x_ref[...], g_ref[...])


def _final_norm(x, g):
    m, d = x.shape
    tm = min(m, 1024)
    return pl.pallas_call(
        _norm_kernel,
        grid=(m // tm,),
        in_specs=[pl.BlockSpec((tm, d), lambda i: (i, 0)), pl.BlockSpec((1, d), lambda i: (0, 0))],
        out_specs=pl.BlockSpec((tm, d), lambda i: (i, 0)),
        out_shape=jax.ShapeDtypeStruct((m, d), F32),
        compiler_params=_params(("parallel",)),
        name="final_norm",
    )(x, g)


def _ret_kernel(p_ref, cos_ref, sin_ref, dmat_ref, qdec_ref, kdec_ref, st_in_ref,
                o_ref, st_ref, *, nb, c, sdec):
    @pl.when(pl.program_id(1) == 0)
    def _():
        st_ref[...] = st_in_ref[...]

    cosf = cos_ref[...]
    sinf = sin_ref[...]
    hd = RET_DK
    for h in range(RET_HEADS):
        q = p_ref[:, h * hd:(h + 1) * hd]
        k = p_ref[:, 1024 + h * hd:1024 + (h + 1) * hd]
        v = p_ref[:, 2048 + h * hd:2048 + (h + 1) * hd]
        g = p_ref[:, 3072 + h * hd:3072 + (h + 1) * hd]
        qr = (q * cosf + pltpu.roll(q, hd // 2, 1) * sinf) * (RET_DK ** -0.5)
        kr = k * cosf + pltpu.roll(k, hd // 2, 1) * sinf
        scores = _dot_nt(qr, kr) * dmat_ref[h]
        o = _dot(scores, v)
        kd = kr * kdec_ref[h]
        qdec = qdec_ref[h]
        for s in range(nb):
            rows = slice(s * c, (s + 1) * c)
            st = st_ref[s, h]
            o_s = o[rows] + _dot(qr[rows], st) * qdec[rows]
            st_ref[s, h] = st * sdec[h] + _dot_tn(kd[rows], v[rows])
            o_ref[rows, h * hd:(h + 1) * hd] = _silu(g[rows]) * _head_norm(o_s, RET_GN_EPS)


def _retention(p_ret, st_all, l, bsz, t, offset):
    nb, c = _seq_blocking(bsz, t, RET_CHUNK, SEQ_BLOCK_ROWS)
    r = nb * c
    nchunk = t // c
    heads = np.arange(RET_HEADS, dtype=np.float64)
    log_g = np.log1p(-np.exp2(-5.0 - heads))
    idx = np.arange(r) % c
    _, incl, _ = _seq_masks(nb, c)
    diff = (idx[:, None] - idx[None, :]).astype(np.float64)
    dmat = np.where(incl[None], np.exp(np.maximum(diff, 0.0)[None] * log_g[:, None, None]), 0.0)
    qdec = np.exp((idx + 1.0)[None, :] * log_g[:, None])
    kdec = np.exp((c - 1.0 - idx)[None, :] * log_g[:, None])
    qdec = np.broadcast_to(qdec[:, :, None], (RET_HEADS, r, RET_DK))
    kdec = np.broadcast_to(kdec[:, :, None], (RET_HEADS, r, RET_DK))
    sdec = tuple(float(x) for x in np.exp(c * log_g))

    half = RET_DK // 2
    freqs = ROPE_BASE ** (-jnp.arange(half, dtype=F32) / half)
    pos = jnp.float32(offset) + jnp.arange(t, dtype=F32)
    ang = pos[:, None] * freqs[None, :]
    cos = jnp.cos(ang)
    sin = jnp.sin(ang)
    cosf = jnp.tile(jnp.concatenate([cos, cos], axis=-1), (nb, 1))
    sinf = jnp.tile(jnp.concatenate([-sin, sin], axis=-1), (nb, 1))

    st_shape = (nb, RET_HEADS, RET_DK, RET_DK)
    if st_all is None:
        st_in = jnp.zeros((bsz,) + st_shape[1:], F32)
        st_spec = pl.BlockSpec(st_shape, lambda i, j: (i, 0, 0, 0))
    else:
        st_in = st_all
        st_spec = pl.BlockSpec((None,) + st_shape, lambda i, j: (l, i, 0, 0, 0))
    m = bsz * t
    return pl.pallas_call(
        functools.partial(_ret_kernel, nb=nb, c=c, sdec=sdec),
        grid=(bsz // nb, nchunk),
        in_specs=[
            pl.BlockSpec((r, RET_PROJ), lambda i, j: (i * nchunk + j, 0)),
            pl.BlockSpec((r, RET_DK), lambda i, j: (j, 0)),
            pl.BlockSpec((r, RET_DK), lambda i, j: (j, 0)),
            pl.BlockSpec((RET_HEADS, r, r), lambda i, j: (0, 0, 0)),
            pl.BlockSpec((RET_HEADS, r, RET_DK), lambda i, j: (0, 0, 0)),
            pl.BlockSpec((RET_HEADS, r, RET_DK), lambda i, j: (0, 0, 0)),
            st_spec,
        ],
        out_specs=[
            pl.BlockSpec((r, 1024), lambda i, j: (i * nchunk + j, 0)),
            pl.BlockSpec(st_shape, lambda i, j: (i, 0, 0, 0)),
        ],
        out_shape=[
            jax.ShapeDtypeStruct((m, 1024), F32),
            jax.ShapeDtypeStruct((bsz,) + st_shape[1:], F32),
        ],
        compiler_params=_params(("parallel", "arbitrary")),
        name="retention",
    )(p_ret, cosf, sinf, jnp.asarray(dmat, F32), jnp.asarray(qdec, F32), jnp.asarray(kdec, F32), st_in)


def _ssd_kernel(p_ref, cw_ref, cb_ref, dtb_ref, alog_ref, dfull_ref, norm_ref, tri_ref,
                conv_in_ref, st_in_ref, o_ref, conv_out_ref, st_ref, ext_ref, y_ref, *, nb, c):
    r = nb * c

    @pl.when(pl.program_id(1) == 0)
    def _():
        st_ref[...] = st_in_ref[...]
        for s in range(nb):
            ext_ref[s, 5:8, :] = conv_in_ref[s]

    for s in range(nb):
        ext_ref[s, 8:8 + c, :] = p_ref[s * c:(s + 1) * c, 1024:1024 + CONV_DIM]
    pieces = []
    for s in range(nb):
        acc = cb_ref[...] + ext_ref[s, 5:5 + c, :] * cw_ref[0:1, :]
        for j in range(1, CONV_W):
            acc = acc + ext_ref[s, 5 + j:5 + j + c, :] * cw_ref[j:j + 1, :]
        pieces.append(acc)
    for s in range(nb):
        conv_out_ref[s] = ext_ref[s, c + 5:c + 8, :]
        ext_ref[s, 0:8, :] = ext_ref[s, c:c + 8, :]
    xbc = _silu(pieces[0] if nb == 1 else jnp.concatenate(pieces, axis=0))
    xs = xbc[:, :1024]
    bm = xbc[:, 1024:1024 + SSM_GROUPS * SSM_STATE]
    cm = xbc[:, 1024 + SSM_GROUPS * SSM_STATE:]

    dt = jax.nn.softplus(p_ref[:, 1024 + CONV_DIM:] + dtb_ref[...])
    a = -jnp.exp(alog_ref[...])
    tri = tri_ref[...]
    cum = _dot_f32(tri, dt * a)
    cum_t = cum.T
    dt_t = dt.T
    mask = tri > 0.5

    hp = SSM_HEADDIM
    hg = SSM_HEADS // SSM_GROUPS
    for g in range(SSM_GROUPS):
        bm_g = bm[:, g * SSM_STATE:(g + 1) * SSM_STATE]
        cm_g = cm[:, g * SSM_STATE:(g + 1) * SSM_STATE]
        cb = _dot_nt(cm_g, bm_g)
        for hh in range(hg):
            h = g * hg + hh
            ci = cum[:, h:h + 1]
            seg = ci - cum_t[h:h + 1, :]
            lmat = jnp.where(mask, jnp.exp(jnp.where(mask, seg, 0.0)), 0.0)
            scores = cb * lmat * dt_t[h:h + 1, :]
            xs_h = xs[:, h * hp:(h + 1) * hp]
            y = _dot(scores, xs_h)
            eci = jnp.exp(ci)
            dt_h = dt[:, h:h + 1]
            for s in range(nb):
                rows = slice(s * c, (s + 1) * c)
                st = st_ref[s, h]
                last = cum[s * c + c - 1:s * c + c, h:h + 1]
                y_s = y[rows] + _dot_nt(cm_g[rows], st) * eci[rows]
                wts = jnp.exp(last - ci[rows]) * dt_h[rows]
                st_ref[s, h] = st * jnp.exp(last) + _dot_tn(xs_h[rows] * wts, bm_g[rows])
                y_ref[rows, h * hp:(h + 1) * hp] = y_s
    y = y_ref[...] + dfull_ref[...] * xs
    z = p_ref[:, :1024]
    o_ref[...] = _rms(y * _silu(z), norm_ref[...])


def _ssd(p_ssm, conv_all, st_all, l, bsz, t, prm):
    nb, c = _seq_blocking(bsz, t, SSD_CHUNK, SEQ_BLOCK_ROWS)
    r = nb * c
    nchunk = t // c
    _, incl, _ = _seq_masks(nb, c)
    tri = jnp.asarray(incl.astype(np.float32))
    st_shape = (nb, SSM_HEADS, SSM_HEADDIM, SSM_STATE)
    cv_shape = (nb, CONV_W - 1, CONV_DIM)
    if st_all is None:
        st_in = jnp.zeros((bsz,) + st_shape[1:], F32)
        conv_in = jnp.zeros((bsz,) + cv_shape[1:], F32)
        st_spec = pl.BlockSpec(st_shape, lambda i, j: (i, 0, 0, 0))
        cv_spec = pl.BlockSpec(cv_shape, lambda i, j: (i, 0, 0))
    else:
        st_in, conv_in = st_all, conv_all
        st_spec = pl.BlockSpec((None,) + st_shape, lambda i, j: (l, i, 0, 0, 0))
        cv_spec = pl.BlockSpec((None,) + cv_shape, lambda i, j: (l, i, 0, 0))
    m = bsz * t

    def vec(n):
        return pl.BlockSpec((None, 1, n), lambda i, j: (l, 0, 0))

    return pl.pallas_call(
        functools.partial(_ssd_kernel, nb=nb, c=c),
        grid=(bsz // nb, nchunk),
        in_specs=[
            pl.BlockSpec((r, SSM_PROJ_PAD), lambda i, j: (i * nchunk + j, 0)),
            pl.BlockSpec((None, CONV_W, CONV_DIM), lambda i, j: (l, 0, 0)),
            vec(CONV_DIM), vec(LANES), vec(LANES), vec(1024), vec(1024),
            pl.BlockSpec((r, r), lambda i, j: (0, 0)),
            cv_spec, st_spec,
        ],
        out_specs=[
            pl.BlockSpec((r, 1024), lambda i, j: (i * nchunk + j, 0)),
            pl.BlockSpec(cv_shape, lambda i, j: (i, 0, 0)),
            pl.BlockSpec(st_shape, lambda i, j: (i, 0, 0, 0)),
        ],
        out_shape=[
            jax.ShapeDtypeStruct((m, 1024), F32),
            jax.ShapeDtypeStruct((bsz,) + cv_shape[1:], F32),
            jax.ShapeDtypeStruct((bsz,) + st_shape[1:], F32),
        ],
        scratch_shapes=[pltpu.VMEM((nb, 8 + c, CONV_DIM), F32), pltpu.VMEM((r, 1024), F32)],
        compiler_params=_params(("parallel", "arbitrary")),
        name="ssd",
    )(p_ssm, prm["ssm_conv_w"], prm["ssm_conv_b"], prm["ssm_dt_bias"], prm["ssm_a_log"],
      prm["ssm_d"], prm["ssm_norm"], tri, conv_in, st_in)


def _rwkv_kernel(p_ref, mu_ref, w0_ref, ww2_ref, a0_ref, wa2_ref, wg2_ref, kk_ref, ka_ref,
                 rk_ref, lng_ref, lnb_ref, masks_ref, shift_in_ref, st_in_ref,
                 o_ref, shift_out_ref, st_ref, ext_ref, *, nb, c):
    r = nb * c

    @pl.when(pl.program_id(1) == 0)
    def _():
        st_ref[...] = st_in_ref[...]
        for s in range(nb):
            ext_ref[s, 7:8, :] = shift_in_ref[s]

    x = p_ref[...]
    for s in range(nb):
        ext_ref[s, 8:8 + c, :] = x[s * c:(s + 1) * c]
    prev = [ext_ref[s, 7:7 + c, :] for s in range(nb)]
    prev = prev[0] if nb == 1 else jnp.concatenate(prev, axis=0)
    for s in range(nb):
        ext_ref[s, 0:8, :] = ext_ref[s, c:c + 8, :]
        shift_out_ref[s] = ext_ref[s, 7:8, :]

    mixed = x + (prev - x) * mu_ref[...]
    rr = mixed[:, 0:1024]
    kc = mixed[:, 1024:2048]
    vc = mixed[:, 2048:3072]
    wd = mixed[:, 3072:3136]
    ad = mixed[:, 3136:3200]
    gd = mixed[:, 3200:3328]
    w_log = -jax.nn.softplus(-(w0_ref[...] + _dot(jnp.tanh(wd), ww2_ref[...]))) - 0.5
    lw = -jnp.exp(w_log)
    a_lr = jax.nn.sigmoid(a0_ref[...] + _dot(ad, wa2_ref[...]))
    g_rw = _dot(jax.nn.sigmoid(gd), wg2_ref[...])
    kk = kc * kk_ref[...]
    kmod = kc * (1.0 + (a_lr - 1.0) * ka_ref[...])
    rkk = rr * kmod * rk_ref[...]

    m_incl = masks_ref[0]
    m_strict = masks_ref[1]
    m_blk = masks_ref[2]
    eye = masks_ref[3]
    cl = _dot_f32(m_incl, lw)

    hd = RWKV_HD
    for h in range(RWKV_HEADS):
        cols = slice(h * hd, (h + 1) * hd)
        r_h, km_h, v_h, a_h = rr[:, cols], kmod[:, cols], vc[:, cols], a_lr[:, cols]
        kk_h = kk[:, cols]
        kk_h = kk_h / jnp.maximum(jnp.sqrt(jnp.sum(kk_h * kk_h, axis=-1, keepdims=True)), 1e-12)
        cl_h, lw_h = cl[:, cols], lw[:, cols]
        e_incl = jnp.exp(cl_h)
        e_inv = jnp.exp(-cl_h)
        rt = r_h * e_incl
        kt = kk_h * jnp.exp(cl_h - lw_h)
        kh = km_h * e_inv
        bh = kk_h * a_h * e_inv
        lhs = jnp.concatenate([kt, rt], axis=0)
        rhs = jnp.concatenate([kh, bh], axis=0)
        gram = _dot_nt(lhs, rhs)
        a_kk = gram[:r, :r] * m_strict
        a_kb = gram[:r, r:] * m_strict
        a_rk = gram[r:, :r] * m_incl
        a_rb = gram[r:, r:] * m_incl

        n1 = a_kb * m_blk
        lo = a_kb - n1
        n2 = _dot(n1, n1)
        n4 = _dot(n2, n2)
        n8 = _dot(n4, n4)
        dinv = _dot(_dot(_dot(eye - n1, eye + n2), eye + n4), eye + n8)
        if r > 16:
            xm = _dot(dinv, lo)
            x2 = _dot(xm, xm)
            tinv = _dot(_dot(eye - xm, eye + x2), dinv)
        else:
            tinv = dinv

        ks, rs = [], []
        for s in range(nb):
            rows = slice(s * c, (s + 1) * c)
            st = st_ref[s, h]
            both = _dot_nt(jnp.concatenate([kt[rows], rt[rows]], axis=0), st)
            ks.append(both[:c])
            rs.append(both[c:])
        ks = ks[0] if nb == 1 else jnp.concatenate(ks, axis=0)
        rs = rs[0] if nb == 1 else jnp.concatenate(rs, axis=0)
        u = _dot(tinv, ks + _dot(a_kk, v_h))
        y = rs + _dot(jnp.concatenate([a_rk, -a_rb], axis=1), jnp.concatenate([v_h, u], axis=0))
        for s in range(nb):
            rows = slice(s * c, (s + 1) * c)
            upd = _dot_tn(jnp.concatenate([v_h[rows], -u[rows]], axis=0),
                          jnp.concatenate([kh[rows], bh[rows]], axis=0))
            st_ref[s, h] = (st_ref[s, h] + upd) * e_incl[s * c + c - 1:s * c + c, :]
        bonus = jnp.sum(rkk[:, cols], axis=-1, keepdims=True) * v_h
        yn = _head_norm(y, RWKV_GN_EPS) * lng_ref[:, cols] + lnb_ref[:, cols]
        o_ref[:, cols] = (yn + bonus) * g_rw[:, cols]


def _rwkv(p_rwkv, shift_all, st_all, l, bsz, t, prm):
    nb, c = _seq_blocking(bsz, t, RWKV_CHUNK, SEQ_BLOCK_ROWS)
    r = nb * c
    nchunk = t // c
    _, incl, strict = _seq_masks(nb, c)
    i = np.arange(r)
    blk = (i[:, None] // 16) == (i[None, :] // 16)
    masks = jnp.asarray(np.stack([incl, strict, blk, np.eye(r, dtype=bool)]).astype(np.float32))
    st_shape = (nb, RWKV_HEADS, RWKV_HD, RWKV_HD)
    sh_shape = (nb, 1, RWKV_PROJ)
    if st_all is None:
        st_in = jnp.zeros((bsz,) + st_shape[1:], F32)
        shift_in = jnp.zeros((bsz,) + sh_shape[1:], F32)
        st_spec = pl.BlockSpec(st_shape, lambda i, j: (i, 0, 0, 0))
        sh_spec = pl.BlockSpec(sh_shape, lambda i, j: (i, 0, 0))
    else:
        st_in, shift_in = st_all, shift_all
        st_spec = pl.BlockSpec((None,) + st_shape, lambda i, j: (l, i, 0, 0, 0))
        sh_spec = pl.BlockSpec((None,) + sh_shape, lambda i, j: (l, i, 0, 0))
    m = bsz * t

    def vec(n):
        return pl.BlockSpec((None, 1, n), lambda i, j: (l, 0, 0))

    def mat(k):
        return pl.BlockSpec((None, k, 1024), lambda i, j: (l, 0, 0))

    return pl.pallas_call(
        functools.partial(_rwkv_kernel, nb=nb, c=c),
        grid=(bsz // nb, nchunk),
        in_specs=[
            pl.BlockSpec((r, RWKV_PROJ), lambda i, j: (i * nchunk + j, 0)),
            vec(RWKV_PROJ), vec(1024), mat(64), vec(1024), mat(64), mat(128),
            vec(1024), vec(1024), vec(1024), vec(1024), vec(1024),
            pl.BlockSpec((4, r, r), lambda i, j: (0, 0, 0)),
            sh_spec, st_spec,
        ],
        out_specs=[
            pl.BlockSpec((r, 1024), lambda i, j: (i * nchunk + j, 0)),
            pl.BlockSpec(sh_shape, lambda i, j: (i, 0, 0)),
            pl.BlockSpec(st_shape, lambda i, j: (i, 0, 0, 0)),
        ],
        out_shape=[
            jax.ShapeDtypeStruct((m, 1024), F32),
            jax.ShapeDtypeStruct((bsz,) + sh_shape[1:], F32),
            jax.ShapeDtypeStruct((bsz,) + st_shape[1:], F32),
        ],
        scratch_shapes=[pltpu.VMEM((nb, 8 + c, RWKV_PROJ), F32)],
        compiler_params=_params(("parallel", "arbitrary")),
        name="rwkv7",
    )(p_rwkv, prm["rwkv_mu"], prm["rwkv_w0"], prm["rwkv_w_w2"], prm["rwkv_a0"], prm["rwkv_w_a2"],
      prm["rwkv_w_g2"], prm["rwkv_k_k"], prm["rwkv_k_a"], prm["rwkv_r_k"], prm["rwkv_lnx_g"],
      prm["rwkv_lnx_b"], masks, shift_in, st_in)


def _merge_kernel(a_ref, b_ref, c_ref, g_ref, x_ref, wb_ref, wo_ref, o_ref):
    merged = None
    for i, br in enumerate((a_ref, b_ref, c_ref)):
        gate = jax.nn.sigmoid(g_ref[:, i * 1024:(i + 1) * 1024])
        term = gate * jnp.dot(br[...].astype(BF16), wb_ref[i], preferred_element_type=F32)
        merged = term if merged is None else merged + term
    o_ref[...] = x_ref[...] + jnp.dot(merged.astype(BF16), wo_ref[...], preferred_element_type=F32)


def _merge(out_a, out_b, out_c, gates, x, l, wb, wo):
    m, d = x.shape
    tm = min(m, 256)
    row = lambda i: (i, 0)
    return pl.pallas_call(
        _merge_kernel,
        grid=(m // tm,),
        in_specs=[
            pl.BlockSpec((tm, d), row), pl.BlockSpec((tm, d), row), pl.BlockSpec((tm, d), row),
            pl.BlockSpec((tm, GATE_PROJ), row), pl.BlockSpec((tm, d), row),
            pl.BlockSpec((None, 3, d, d), lambda i: (l, 0, 0, 0)),
            pl.BlockSpec((None, d, d), lambda i: (l, 0, 0)),
        ],
        out_specs=pl.BlockSpec((tm, d), row),
        out_shape=jax.ShapeDtypeStruct((m, d), F32),
        compiler_params=_params(("parallel",)),
        name="merge",
    )(out_a, out_b, out_c, gates, x, wb, wo)


def _attn_kernel(q_ref, k_ref, v_ref, o_ref, *, nb, c):
    hd = X_HEAD_DIM
    for s in range(nb):
        rows = slice(s * c, (s + 1) * c)
        for h in range(X_HEADS):
            cols = slice(h * hd, (h + 1) * hd)
            sc = _dot_nt(q_ref[rows, cols], k_ref[s, :, cols]) * (X_HEAD_DIM ** -0.5)
            sc = sc - jnp.max(sc, axis=-1, keepdims=True)
            e = jnp.exp(sc)
            pr = e / jnp.sum(e, axis=-1, keepdims=True)
            o_ref[rows, cols] = _dot(pr, v_ref[s, :, cols])


def _attn(q, mk, mv, l, bsz, t):
    if t > ATTN_ROWS:
        nb, c = 1, ATTN_ROWS
    else:
        nb, c = max(1, min(bsz, 32 // t)), t
    r = nb * c
    ntile = t // c
    kv_shape = (nb, N_MEM, 1024)
    if mk.ndim == 3:
        kv_spec = pl.BlockSpec(kv_shape, lambda i, j: (i, 0, 0))
    else:
        kv_spec = pl.BlockSpec((None,) + kv_shape, lambda i, j: (l, i, 0, 0))
    return pl.pallas_call(
        functools.partial(_attn_kernel, nb=nb, c=c),
        grid=(bsz // nb, ntile),
        in_specs=[pl.BlockSpec((r, 1024), lambda i, j: (i * ntile + j, 0)), kv_spec, kv_spec],
        out_specs=pl.BlockSpec((r, 1024), lambda i, j: (i * ntile + j, 0)),
        out_shape=jax.ShapeDtypeStruct((bsz * t, 1024), F32),
        compiler_params=_params(("parallel", "arbitrary")),
        name="xattn_core",
    )(q, mk, mv)


def _trunk(x, prm, states, mem_k, mem_v, bsz, t, offset):
    new = [[] for _ in range(5)]
    for l in range(DEPTH):
        x = _ffn(x, prm["ffa_norm"], l, prm["ffa_w1"], prm["ffa_w3"], prm["ffa_w2"])
        p_ret = _norm_matmul(x, prm["mix_norm"], l, prm["w_ret"], 0, RET_PROJ)
        p_ssm = _norm_matmul(x, prm["mix_norm"], l, prm["w_ssm"], 0, SSM_PROJ_PAD)
        p_rwkv = _norm_matmul(x, prm["mix_norm"], l, prm["w_rwkv"], 0, RWKV_PROJ)
        p_gate = _norm_matmul(x, prm["mix_norm"], l, prm["w_gate"], 0, GATE_PROJ)
        if states is None:
            s_ret = s_ssm = s_conv = s_rwkv = s_shift = None
        else:
            s_ret, s_ssm, s_conv, s_rwkv, s_shift = states
        out_a, ret_new = _retention(p_ret, s_ret, l, bsz, t, offset)
        out_b, conv_new, ssm_new = _ssd(p_ssm, s_conv, s_ssm, l, bsz, t, prm)
        out_c, shift_new, rwkv_new = _rwkv(p_rwkv, s_shift, s_rwkv, l, bsz, t, prm)
        x = _merge(out_a, out_b, out_c, p_gate, x, l, prm["w_branch"], prm["w_out"])
        q = _norm_matmul(x, prm["xattn_norm"], l, prm["xattn_wq"], 0, 1024)
        if isinstance(mem_k, (list, tuple)):
            att = _attn(q, mem_k[l], mem_v[l], l, bsz, t)
        else:
            att = _attn(q, mem_k, mem_v, l, bsz, t)
        x = _matmul_residual(att, l, prm["xattn_wo"], x)
        x = _ffn(x, prm["ffb_norm"], l, prm["ffb_w1"], prm["ffb_w3"], prm["ffb_w2"])
        for lst, val in zip(new, (ret_new, ssm_new, conv_new, rwkv_new, shift_new)):
            lst.append(val)
    y = _final_norm(x, prm["final_norm"])
    return y, tuple(jnp.stack(lst) for lst in new)


def _prep_params(raw):
    p = {}
    for name in ("ffa_w1", "ffa_w3", "ffa_w2", "ffb_w1", "ffb_w3", "ffb_w2", "w_branch", "w_out",
                 "xattn_wq", "xattn_wk", "xattn_wv", "xattn_wo", "rwkv_w_w2", "rwkv_w_a2", "rwkv_w_g2"):
        p[name] = raw[name].astype(BF16)
    w_in = raw["w_in"]
    c0, c1, c2 = RET_PROJ, RET_PROJ + SSM_PROJ, RET_PROJ + SSM_PROJ + RWKV_PROJ
    p["w_ret"] = w_in[:, :, :c0].astype(BF16)
    p["w_ssm"] = jnp.pad(w_in[:, :, c0:c1], ((0, 0), (0, 0), (0, SSM_PROJ_PAD - SSM_PROJ))).astype(BF16)
    p["w_rwkv"] = w_in[:, :, c1:c2].astype(BF16)
    p["w_gate"] = w_in[:, :, c2:].astype(BF16)
    for name in ("ffa_norm", "mix_norm", "ssm_conv_b", "ssm_norm", "rwkv_mu", "rwkv_w0", "rwkv_a0",
                 "rwkv_k_k", "rwkv_k_a", "rwkv_lnx_g", "rwkv_lnx_b", "xattn_norm", "mem_norm", "ffb_norm"):
        p[name] = raw[name][:, None, :]
    p["rwkv_r_k"] = raw["rwkv_r_k"].reshape(DEPTH, 1, 1024)
    pad = ((0, 0), (0, LANES - SSM_HEADS))
    p["ssm_dt_bias"] = jnp.pad(raw["ssm_dt_bias"], pad)[:, None, :]
    p["ssm_a_log"] = jnp.pad(raw["ssm_a_log"], pad)[:, None, :]
    p["ssm_d"] = jnp.repeat(raw["ssm_d"], SSM_HEADDIM, axis=-1)[:, None, :]
    p["ssm_conv_w"] = raw["ssm_conv_w"]
    p["final_norm"] = raw["final_norm"][None, :]
    return p


def _run(x_prompt, x_sample, mem_prompt, states, cache_mem_k, cache_mem_v, raw):
    prm = _prep_params(raw)
    b, t, d = x_prompt.shape
    db, dt_, _ = x_sample.shape
    n_mem = mem_prompt.shape[1]
    mem2 = mem_prompt.reshape(b * n_mem, d)
    mk = [_norm_matmul(mem2, prm["mem_norm"], l, prm["xattn_wk"], 0, 1024) for l in range(DEPTH)]
    mv = [_norm_matmul(mem2, prm["mem_norm"], l, prm["xattn_wv"], 0, 1024) for l in range(DEPTH)]
    mk3 = [a.reshape(b, n_mem, d) for a in mk]
    mv3 = [a.reshape(b, n_mem, d) for a in mv]
    p_mem_k = jnp.stack(mk).reshape(DEPTH, b, n_mem, X_HEADS, X_HEAD_DIM)
    p_mem_v = jnp.stack(mv).reshape(DEPTH, b, n_mem, X_HEADS, X_HEAD_DIM)
    y_p, st_p = _trunk(x_prompt.reshape(b * t, d), prm, None, mk3, mv3, b, t, 0)

    ck = cache_mem_k.reshape(DEPTH, db, n_mem, d)
    cv = cache_mem_v.reshape(DEPTH, db, n_mem, d)
    y_s, st_s = _trunk(x_sample.reshape(db * dt_, d), prm, states, ck, cv, db, dt_, PAST_LEN)
    return (y_p.reshape(b, t, d), y_s.reshape(db, dt_, d)) + st_p + (p_mem_k, p_mem_v) + st_s


def kernel(x_prompt, x_sample, mem_prompt, state_ret, state_ssm, state_conv, state_rwkv, state_shift, cache_mem_k, cache_mem_v, ffa_norm, ffa_w1, ffa_w3, ffa_w2, mix_norm, w_in, ssm_conv_w, ssm_conv_b, ssm_dt_bias, ssm_a_log, ssm_d, ssm_norm, rwkv_mu, rwkv_w0, rwkv_w_w2, rwkv_a0, rwkv_w_a2, rwkv_w_g2, rwkv_k_k, rwkv_k_a, rwkv_r_k, rwkv_lnx_g, rwkv_lnx_b, w_branch, w_out, xattn_norm, mem_norm, xattn_wq, xattn_wk, xattn_wv, xattn_wo, ffb_norm, ffb_w1, ffb_w3, ffb_w2, final_norm):
    raw = dict(ffa_norm=ffa_norm, ffa_w1=ffa_w1, ffa_w3=ffa_w3, ffa_w2=ffa_w2, mix_norm=mix_norm, w_in=w_in,
               ssm_conv_w=ssm_conv_w, ssm_conv_b=ssm_conv_b, ssm_dt_bias=ssm_dt_bias, ssm_a_log=ssm_a_log,
               ssm_d=ssm_d, ssm_norm=ssm_norm, rwkv_mu=rwkv_mu, rwkv_w0=rwkv_w0, rwkv_w_w2=rwkv_w_w2,
               rwkv_a0=rwkv_a0, rwkv_w_a2=rwkv_w_a2, rwkv_w_g2=rwkv_w_g2, rwkv_k_k=rwkv_k_k,
               rwkv_k_a=rwkv_k_a, rwkv_r_k=rwkv_r_k, rwkv_lnx_g=rwkv_lnx_g, rwkv_lnx_b=rwkv_lnx_b,
               w_branch=w_branch, w_out=w_out, xattn_norm=xattn_norm, mem_norm=mem_norm,
               xattn_wq=xattn_wq, xattn_wk=xattn_wk, xattn_wv=xattn_wv, xattn_wo=xattn_wo,
               ffb_norm=ffb_norm, ffb_w1=ffb_w1, ffb_w3=ffb_w3, ffb_w2=ffb_w2, final_norm=final_norm)
    states = (state_ret, state_ssm, state_conv, state_rwkv, state_shift)
    return _run(x_prompt, x_sample, mem_prompt, states, cache_mem_k, cache_mem_v, raw)
```

```python
import functools
import math

import numpy as np
import jax
import jax.numpy as jnp
from jax import lax
from jax.experimental import pallas as pl
from jax.experimental.pallas import tpu as pltpu

F32 = jnp.float32
BF16 = jnp.bfloat16

D_MODEL = 1024
DEPTH = 4
PAST_LEN = 16384
D_FF = 2816
EPS = 1e-6
RET_HEADS = 8
RET_DK = 128
RET_GN_EPS = 1e-5
ROPE_BASE = 10000.0
SSM_HEADS = 16
SSM_HEADDIM = 64
SSM_GROUPS = 2
SSM_STATE = 128
CONV_W = 4
CONV_DIM = 1536
RWKV_HEADS = 16
RWKV_HD = 64
RWKV_PROJ = 3328
RWKV_GN_EPS = 64e-5
N_MEM = 256
X_HEADS = 4
X_HEAD_DIM = 256
RET_PROJ = 4096
SSM_PROJ = 2576
SSM_PROJ_PAD = 2688
GATE_PROJ = 3072
LANES = 128
VMEM_LIMIT = 56 * 1024 * 1024

RET_CHUNK = 128
SSD_CHUNK = 128
RWKV_CHUNK = 64
ATTN_ROWS = 512
SEQ_BLOCK_ROWS = 64


def _params(sem):
    return pltpu.CompilerParams(dimension_semantics=sem, vmem_limit_bytes=VMEM_LIMIT)


def _rms(x, g):
    return x * lax.rsqrt(jnp.mean(x * x, axis=-1, keepdims=True) + EPS) * g


def _head_norm(x, eps):
    mu = jnp.mean(x, axis=-1, keepdims=True)
    xc = x - mu
    return xc * lax.rsqrt(jnp.mean(xc * xc, axis=-1, keepdims=True) + eps)


def _silu(x):
    return x * jax.nn.sigmoid(x)


def _dot(a, b):
    return jnp.dot(a.astype(BF16), b.astype(BF16), preferred_element_type=F32)


def _dot_nt(a, b):
    return lax.dot_general(a.astype(BF16), b.astype(BF16), (((1,), (1,)), ((), ())),
                           preferred_element_type=F32)


def _dot_tn(a, b):
    return lax.dot_general(a.astype(BF16), b.astype(BF16), (((0,), (0,)), ((), ())),
                           preferred_element_type=F32)


def _dot_f32(a, b):
    return jnp.dot(a, b, precision=lax.Precision.HIGHEST, preferred_element_type=F32)


def _seq_blocking(bsz, t, chunk_max, block_rows):
    if t > chunk_max:
        assert t % chunk_max == 0
        return 1, chunk_max
    nb = max(1, min(bsz, block_rows // t))
    assert bsz % nb == 0
    return nb, t


def _seq_masks(nb, c):
    r = nb * c
    i = np.arange(r)
    same = (i[:, None] // c) == (i[None, :] // c)
    incl = same & (i[:, None] >= i[None, :])
    strict = same & (i[:, None] > i[None, :])
    return same, incl, strict


def _nm_kernel(x_ref, g_ref, w_ref, o_ref, h_ref):
    @pl.when(pl.program_id(1) == 0)
    def _():
        h_ref[...] = _rms(x_ref[...], g_ref[...]).astype(BF16)

    o_ref[...] = jnp.dot(h_ref[...], w_ref[...], preferred_element_type=F32)


def _norm_matmul(x, g, l, w, col0, ncols):
    m, d = x.shape
    if ncols % 1024 == 0:
        tn = 1024
        tm = min(m, 1024)
    else:
        tn = ncols
        tm = min(m, 512)
    assert col0 % tn == 0 and m % tm == 0
    cb = col0 // tn
    return pl.pallas_call(
        _nm_kernel,
        grid=(m // tm, ncols // tn),
        in_specs=[
            pl.BlockSpec((tm, d), lambda i, j: (i, 0)),
            pl.BlockSpec((None, 1, d), lambda i, j: (l, 0, 0)),
            pl.BlockSpec((None, d, tn), lambda i, j: (l, 0, cb + j)),
        ],
        out_specs=pl.BlockSpec((tm, tn), lambda i, j: (i, j)),
        out_shape=jax.ShapeDtypeStruct((m, ncols), F32),
        scratch_shapes=[pltpu.VMEM((tm, d), BF16)],
        compiler_params=_params(("parallel", "arbitrary")),
        name="norm_matmul",
    )(x, g, w)


def _mr_kernel(a_ref, w_ref, x_ref, o_ref):
    o_ref[...] = x_ref[...] + jnp.dot(a_ref[...].astype(BF16), w_ref[...],
                                       preferred_element_type=F32)


def _matmul_residual(a, l, w, x):
    m, k = a.shape
    n = w.shape[-1]
    tm = min(m, 512)
    return pl.pallas_call(
        _mr_kernel,
        grid=(m // tm,),
        in_specs=[
            pl.BlockSpec((tm, k), lambda i: (i, 0)),
            pl.BlockSpec((None, k, n), lambda i: (l, 0, 0)),
            pl.BlockSpec((tm, n), lambda i: (i, 0)),
        ],
        out_specs=pl.BlockSpec((tm, n), lambda i: (i, 0)),
        out_shape=jax.ShapeDtypeStruct((m, n), F32),
        compiler_params=_params(("parallel",)),
        name="matmul_residual",
    )(a, w, x)


def _ffn_kernel(x_ref, g_ref, w1_ref, w3_ref, w2_ref, o_ref, h_ref, acc_ref):
    j = pl.program_id(1)

    @pl.when(j == 0)
    def _():
        h_ref[...] = _rms(x_ref[...], g_ref[...]).astype(BF16)
        acc_ref[...] = jnp.zeros_like(acc_ref)

    h = h_ref[...]
    a = jnp.dot(h, w1_ref[...], preferred_element_type=F32)
    b = jnp.dot(h, w3_ref[...], preferred_element_type=F32)
    u = (_silu(a) * b).astype(BF16)
    acc_ref[...] += jnp.dot(u, w2_ref[...], preferred_element_type=F32)

    @pl.when(j == pl.num_programs(1) - 1)
    def _():
        o_ref[...] = x_ref[...] + 0.5 * acc_ref[...]


def _ffn(x, g, l, w1, w3, w2):
    m, d = x.shape
    f = w1.shape[-1]
    tm = min(m, 1024)
    tf = 256
    return pl.pallas_call(
        _ffn_kernel,
        grid=(m // tm, f // tf),
        in_specs=[
            pl.BlockSpec((tm, d), lambda i, j: (i, 0)),
            pl.BlockSpec((None, 1, d), lambda i, j: (l, 0, 0)),
            pl.BlockSpec((None, d, tf), lambda i, j: (l, 0, j)),
            pl.BlockSpec((None, d, tf), lambda i, j: (l, 0, j)),
            pl.BlockSpec((None, tf, d), lambda i, j: (l, j, 0)),
        ],
        out_specs=pl.BlockSpec((tm, d), lambda i, j: (i, 0)),
        out_shape=jax.ShapeDtypeStruct((m, d), F32),
        scratch_shapes=[pltpu.VMEM((tm, d), BF16), pltpu.VMEM((tm, d), F32)],
        compiler_params=_params(("parallel", "arbitrary")),
        name="ffn",
    )(x, g, w1, w3, w2)


def _norm_kernel(x_ref, g_ref, o_ref):
    o_ref[...] = _rms(x_ref[...], g_ref[...])


def _final_norm(x, g):
    m, d = x.shape
    tm = min(m, 1024)
    return pl.pallas_call(
        _norm_kernel,
        grid=(m // tm,),
        in_specs=[pl.BlockSpec((tm, d), lambda i: (i, 0)), pl.BlockSpec((1, d), lambda i: (0, 0))],
        out_specs=pl.BlockSpec((tm, d), lambda i: (i, 0)),
        out_shape=jax.ShapeDtypeStruct((m, d), F32),
        compiler_params=_params(("parallel",)),
        name="final_norm",
    )(x, g)


def _ret_kernel(p_ref, cos_ref, sin_ref, dmat_ref, qdec_ref, kdec_ref, st_in_ref, acc_ref,
                o_ref, st_ref, *, nb, c, sdec):
    del acc_ref
    @pl.when(pl.program_id(1) == 0)
    def _():
        st_ref[...] = st_in_ref[...]

    cosf = cos_ref[...]
    sinf = sin_ref[...]
    hd = RET_DK
    heads = range(RET_HEADS)
    rows = [slice(s * c, (s + 1) * c) for s in range(nb)]
    qr, kr, v = [], [], []
    for h in heads:
        q = p_ref[:, h * hd:(h + 1) * hd]
        k = p_ref[:, 1024 + h * hd:1024 + (h + 1) * hd]
        qr.append(((q * cosf + pltpu.roll(q, hd // 2, 1) * sinf) * (RET_DK ** -0.5)).astype(BF16))
        kr.append(k * cosf + pltpu.roll(k, hd // 2, 1) * sinf)
        v.append(p_ref[:, 2048 + h * hd:2048 + (h + 1) * hd].astype(BF16))
    scores = [_dot_nt(qr[h], kr[h]) * dmat_ref[h] for h in heads]
    inter = [[_dot(qr[h][rw], st_ref[s, h]) for s, rw in enumerate(rows)] for h in heads]
    o = [_dot(scores[h], v[h]) for h in heads]
    for h in heads:
        kd = kr[h] * kdec_ref[h]
        qdec = qdec_ref[h]
        for s, rw in enumerate(rows):
            o_s = o[h][rw] + inter[h][s] * qdec[rw]
            g = p_ref[rw, 3072 + h * hd:3072 + (h + 1) * hd]
            o_ref[rw, h * hd:(h + 1) * hd] = _silu(g) * _head_norm(o_s, RET_GN_EPS)
            st_ref[s, h] = st_ref[s, h] * sdec[h] + _dot_tn(kd[rw], v[h][rw])


def _retention(p_ret, st_all, l, bsz, t, offset, acc):
    nb, c = _seq_blocking(bsz, t, RET_CHUNK, SEQ_BLOCK_ROWS)
    r = nb * c
    nchunk = t // c
    heads = np.arange(RET_HEADS, dtype=np.float64)
    log_g = np.log1p(-np.exp2(-5.0 - heads))
    idx = np.arange(r) % c
    _, incl, _ = _seq_masks(nb, c)
    diff = (idx[:, None] - idx[None, :]).astype(np.float64)
    dmat = np.where(incl[None], np.exp(np.maximum(diff, 0.0)[None] * log_g[:, None, None]), 0.0)
    qdec = np.exp((idx + 1.0)[None, :] * log_g[:, None])
    kdec = np.exp((c - 1.0 - idx)[None, :] * log_g[:, None])
    qdec = np.broadcast_to(qdec[:, :, None], (RET_HEADS, r, RET_DK))
    kdec = np.broadcast_to(kdec[:, :, None], (RET_HEADS, r, RET_DK))
    sdec = tuple(float(x) for x in np.exp(c * log_g))

    half = RET_DK // 2
    freqs = ROPE_BASE ** (-jnp.arange(half, dtype=F32) / half)
    pos = jnp.float32(offset) + jnp.arange(t, dtype=F32)
    ang = pos[:, None] * freqs[None, :]
    cos = jnp.cos(ang)
    sin = jnp.sin(ang)
    cosf = jnp.tile(jnp.concatenate([cos, cos], axis=-1), (nb, 1))
    sinf = jnp.tile(jnp.concatenate([-sin, sin], axis=-1), (nb, 1))

    st_shape = (nb, RET_HEADS, RET_DK, RET_DK)
    if st_all is None:
        st_in = jnp.zeros((bsz,) + st_shape[1:], F32)
        st_spec = pl.BlockSpec(st_shape, lambda i, j: (i, 0, 0, 0))
    else:
        st_in = st_all
        st_spec = pl.BlockSpec((None,) + st_shape, lambda i, j: (l, i, 0, 0, 0))
    m = bsz * t
    return pl.pallas_call(
        functools.partial(_ret_kernel, nb=nb, c=c, sdec=sdec),
        grid=(bsz // nb, nchunk),
        in_specs=[
            pl.BlockSpec((r, RET_PROJ), lambda i, j: (i * nchunk + j, 0)),
            pl.BlockSpec((r, RET_DK), lambda i, j: (j, 0)),
            pl.BlockSpec((r, RET_DK), lambda i, j: (j, 0)),
            pl.BlockSpec((RET_HEADS, r, r), lambda i, j: (0, 0, 0)),
            pl.BlockSpec((RET_HEADS, r, RET_DK), lambda i, j: (0, 0, 0)),
            pl.BlockSpec((RET_HEADS, r, RET_DK), lambda i, j: (0, 0, 0)),
            st_spec,
            pl.BlockSpec(memory_space=pl.ANY),
        ],
        out_specs=[
            pl.BlockSpec((r, 1024), lambda i, j: (i * nchunk + j, 0)),
            pl.BlockSpec((None,) + st_shape, lambda i, j: (l, i, 0, 0, 0)),
        ],
        out_shape=[
            jax.ShapeDtypeStruct((m, 1024), F32),
            jax.ShapeDtypeStruct(acc.shape, F32),
        ],
        input_output_aliases={7: 1},
        compiler_params=_params(("parallel", "arbitrary")),
        name="retention",
    )(p_ret, cosf, sinf, jnp.asarray(dmat, F32), jnp.asarray(qdec, F32), jnp.asarray(kdec, F32), st_in, acc)


def _ssd_kernel(p_ref, cw_ref, cb_ref, dtb_ref, alog_ref, dfull_ref, norm_ref, tri_ref,
                conv_in_ref, st_in_ref, conv_acc_ref, st_acc_ref,
                o_ref, conv_out_ref, st_ref, ext_ref, y_ref, *, nb, c):
    del conv_acc_ref, st_acc_ref
    r = nb * c

    @pl.when(pl.program_id(1) == 0)
    def _():
        st_ref[...] = st_in_ref[...]
        for s in range(nb):
            ext_ref[s, 5:8, :] = conv_in_ref[s]

    for s in range(nb):
        ext_ref[s, 8:8 + c, :] = p_ref[s * c:(s + 1) * c, 1024:1024 + CONV_DIM]
    pieces = []
    for s in range(nb):
        acc = cb_ref[...] + ext_ref[s, 5:5 + c, :] * cw_ref[0:1, :]
        for j in range(1, CONV_W):
            acc = acc + ext_ref[s, 5 + j:5 + j + c, :] * cw_ref[j:j + 1, :]
        pieces.append(acc)
    for s in range(nb):
        conv_out_ref[s] = ext_ref[s, c + 5:c + 8, :]
        ext_ref[s, 0:8, :] = ext_ref[s, c:c + 8, :]
    xbc = _silu(pieces[0] if nb == 1 else jnp.concatenate(pieces, axis=0))
    xs = xbc[:, :1024]
    bm = xbc[:, 1024:1024 + SSM_GROUPS * SSM_STATE]
    cm = xbc[:, 1024 + SSM_GROUPS * SSM_STATE:]

    dt = jax.nn.softplus(p_ref[:, 1024 + CONV_DIM:] + dtb_ref[...])
    a = -jnp.exp(alog_ref[...])
    tri = tri_ref[...]
    cum = _dot_f32(tri, dt * a)
    cum_t = cum.T
    dt_t = dt.T
    mask = tri > 0.5

    hp = SSM_HEADDIM
    hg = SSM_HEADS // SSM_GROUPS
    for g in range(SSM_GROUPS):
        bm_g = bm[:, g * SSM_STATE:(g + 1) * SSM_STATE]
        cm_g = cm[:, g * SSM_STATE:(g + 1) * SSM_STATE]
        cb = _dot_nt(cm_g, bm_g)
        for hh in range(hg):
            h = g * hg + hh
            ci = cum[:, h:h + 1]
            seg = ci - cum_t[h:h + 1, :]
            lmat = jnp.where(mask, jnp.exp(jnp.where(mask, seg, 0.0)), 0.0)
            scores = cb * lmat * dt_t[h:h + 1, :]
            xs_h = xs[:, h * hp:(h + 1) * hp]
            y = _dot(scores, xs_h)
            eci = jnp.exp(ci)
            dt_h = dt[:, h:h + 1]
            for s in range(nb):
                rows = slice(s * c, (s + 1) * c)
                st = st_ref[s, h]
                last = cum[s * c + c - 1:s * c + c, h:h + 1]
                y_s = y[rows] + _dot_nt(cm_g[rows], st) * eci[rows]
                wts = jnp.exp(last - ci[rows]) * dt_h[rows]
                st_ref[s, h] = st * jnp.exp(last) + _dot_tn(xs_h[rows] * wts, bm_g[rows])
                y_ref[rows, h * hp:(h + 1) * hp] = y_s
    y = y_ref[...] + dfull_ref[...] * xs
    z = p_ref[:, :1024]
    o_ref[...] = _rms(y * _silu(z), norm_ref[...])


def _ssd(p_ssm, conv_all, st_all, l, bsz, t, prm, conv_acc, st_acc):
    nb, c = _seq_blocking(bsz, t, SSD_CHUNK, SEQ_BLOCK_ROWS)
    r = nb * c
    nchunk = t // c
    _, incl, _ = _seq_masks(nb, c)
    tri = jnp.asarray(incl.astype(np.float32))
    st_shape = (nb, SSM_HEADS, SSM_HEADDIM, SSM_STATE)
    cv_shape = (nb, CONV_W - 1, CONV_DIM)
    if st_all is None:
        st_in = jnp.zeros((bsz,) + st_shape[1:], F32)
        conv_in = jnp.zeros((bsz,) + cv_shape[1:], F32)
        st_spec = pl.BlockSpec(st_shape, lambda i, j: (i, 0, 0, 0))
        cv_spec = pl.BlockSpec(cv_shape, lambda i, j: (i, 0, 0))
    else:
        st_in, conv_in = st_all, conv_all
        st_spec = pl.BlockSpec((None,) + st_shape, lambda i, j: (l, i, 0, 0, 0))
        cv_spec = pl.BlockSpec((None,) + cv_shape, lambda i, j: (l, i, 0, 0))
    m = bsz * t

    def vec(n):
        return pl.BlockSpec((None, 1, n), lambda i, j: (l, 0, 0))

    return pl.pallas_call(
        functools.partial(_ssd_kernel, nb=nb, c=c),
        grid=(bsz // nb, nchunk),
        in_specs=[
            pl.BlockSpec((r, SSM_PROJ_PAD), lambda i, j: (i * nchunk + j, 0)),
            pl.BlockSpec((None, CONV_W, CONV_DIM), lambda i, j: (l, 0, 0)),
            vec(CONV_DIM), vec(LANES), vec(LANES), vec(1024), vec(1024),
            pl.BlockSpec((r, r), lambda i, j: (0, 0)),
            cv_spec, st_spec,
            pl.BlockSpec(memory_space=pl.ANY), pl.BlockSpec(memory_space=pl.ANY),
        ],
        out_specs=[
            pl.BlockSpec((r, 1024), lambda i, j: (i * nchunk + j, 0)),
            pl.BlockSpec((None,) + cv_shape, lambda i, j: (l, i, 0, 0)),
            pl.BlockSpec((None,) + st_shape, lambda i, j: (l, i, 0, 0, 0)),
        ],
        out_shape=[
            jax.ShapeDtypeStruct((m, 1024), F32),
            jax.ShapeDtypeStruct(conv_acc.shape, F32),
            jax.ShapeDtypeStruct(st_acc.shape, F32),
        ],
        input_output_aliases={10: 1, 11: 2},
        scratch_shapes=[pltpu.VMEM((nb, 8 + c, CONV_DIM), F32), pltpu.VMEM((r, 1024), F32)],
        compiler_params=_params(("parallel", "arbitrary")),
        name="ssd",
    )(p_ssm, prm["ssm_conv_w"], prm["ssm_conv_b"], prm["ssm_dt_bias"], prm["ssm_a_log"],
      prm["ssm_d"], prm["ssm_norm"], tri, conv_in, st_in, conv_acc, st_acc)


def _rwkv_kernel(p_ref, mu_ref, w0_ref, ww2_ref, a0_ref, wa2_ref, wg2_ref, kk_ref, ka_ref,
                 rk_ref, lng_ref, lnb_ref, masks_ref, shift_in_ref, st_in_ref, shift_acc_ref, st_acc_ref,
                 o_ref, shift_out_ref, st_ref, ext_ref, *, nb, c):
    del shift_acc_ref, st_acc_ref
    r = nb * c

    @pl.when(pl.program_id(1) == 0)
    def _():
        st_ref[...] = st_in_ref[...]
        for s in range(nb):
            ext_ref[s, 7:8, :] = shift_in_ref[s]

    x = p_ref[...]
    for s in range(nb):
        ext_ref[s, 8:8 + c, :] = x[s * c:(s + 1) * c]
    prev = [ext_ref[s, 7:7 + c, :] for s in range(nb)]
    prev = prev[0] if nb == 1 else jnp.concatenate(prev, axis=0)
    for s in range(nb):
        ext_ref[s, 0:8, :] = ext_ref[s, c:c + 8, :]
        shift_out_ref[s] = ext_ref[s, 7:8, :]

    mixed = x + (prev - x) * mu_ref[...]
    rr = mixed[:, 0:1024]
    kc = mixed[:, 1024:2048]
    vc = mixed[:, 2048:3072]
    wd = mixed[:, 3072:3136]
    ad = mixed[:, 3136:3200]
    gd = mixed[:, 3200:3328]
    w_log = -jax.nn.softplus(-(w0_ref[...] + _dot(jnp.tanh(wd), ww2_ref[...]))) - 0.5
    lw = -jnp.exp(w_log)
    a_lr = jax.nn.sigmoid(a0_ref[...] + _dot(ad, wa2_ref[...]))
    g_rw = _dot(jax.nn.sigmoid(gd), wg2_ref[...])
    kk = kc * kk_ref[...]
    kmod = kc * (1.0 + (a_lr - 1.0) * ka_ref[...])
    rkk = rr * kmod * rk_ref[...]

    m_incl = masks_ref[0]
    m_strict = masks_ref[1]
    m_blk = masks_ref[2]
    eye = masks_ref[3]
    cl = _dot_f32(m_incl, lw)

    hd = RWKV_HD
    heads = range(RWKV_HEADS)
    seqs = range(nb)
    cols = [slice(h * hd, (h + 1) * hd) for h in heads]
    rows = [slice(s * c, (s + 1) * c) for s in seqs]

    v_h = [vc[:, cs] for cs in cols]
    kt, rt, kh, bh, e_last = [], [], [], [], []
    for cs in cols:
        kk_h = kk[:, cs]
        kk_h = kk_h * lax.rsqrt(jnp.maximum(jnp.sum(kk_h * kk_h, axis=-1, keepdims=True), 1e-24))
        cl_h = cl[:, cs]
        e_incl = jnp.exp(cl_h)
        e_inv = jnp.exp(-cl_h)
        rt.append(rr[:, cs] * e_incl)
        kt.append(kk_h * jnp.exp(cl_h - lw[:, cs]))
        kh.append(kmod[:, cs] * e_inv)
        bh.append(kk_h * a_lr[:, cs] * e_inv)
        e_last.append([e_incl[s * c + c - 1:s * c + c, :] for s in seqs])
    gram = [_dot_nt(jnp.concatenate([kt[h], rt[h]], axis=0), jnp.concatenate([kh[h], bh[h]], axis=0))
            for h in heads]
    both = [[_dot_nt(jnp.concatenate([kt[h][rw], rt[h][rw]], axis=0), st_ref[s, h])
             for s, rw in enumerate(rows)] for h in heads]
    a_kk = [g[:r, :r] * m_strict for g in gram]
    a_kb = [g[:r, r:] * m_strict for g in gram]
    a_rk = [g[r:, :r] * m_incl for g in gram]
    a_rb = [g[r:, r:] * m_incl for g in gram]
    akv = [_dot(a_kk[h], v_h[h]) for h in heads]

    n1 = [a * m_blk for a in a_kb]
    n2 = [_dot(n, n) for n in n1]
    p1 = [_dot(eye - n1[h], eye + n2[h]) for h in heads]
    n4 = [_dot(n, n) for n in n2]
    p2 = [_dot(p1[h], eye + n4[h]) for h in heads]
    n8 = [_dot(n, n) for n in n4]
    tinv = [_dot(p2[h], eye + n8[h]) for h in heads]
    if r > 16:
        dinv = tinv
        xm = [_dot(dinv[h], a_kb[h] - n1[h]) for h in heads]
        x2 = [_dot(x_, x_) for x_ in xm]
        t1 = [_dot(eye - xm[h], eye + x2[h]) for h in heads]
        tinv = [_dot(t1[h], dinv[h]) for h in heads]

    def cat_rows(pieces):
        return pieces[0] if nb == 1 else jnp.concatenate(pieces, axis=0)

    ks = [cat_rows([b[:c] for b in both[h]]) for h in heads]
    rs = [cat_rows([b[c:] for b in both[h]]) for h in heads]
    u = [_dot(tinv[h], ks[h] + akv[h]) for h in heads]
    y = [rs[h] + _dot(jnp.concatenate([a_rk[h], -a_rb[h]], axis=1),
                      jnp.concatenate([v_h[h], u[h]], axis=0)) for h in heads]
    for h in heads:
        for s, rw in enumerate(rows):
            upd = _dot_tn(jnp.concatenate([v_h[h][rw], -u[h][rw]], axis=0),
                          jnp.concatenate([kh[h][rw], bh[h][rw]], axis=0))
            st_ref[s, h] = (st_ref[s, h] + upd) * e_last[h][s]
    for h, cs in enumerate(cols):
        bonus = jnp.sum(rkk[:, cs], axis=-1, keepdims=True) * v_h[h]
        yn = _head_norm(y[h], RWKV_GN_EPS) * lng_ref[:, cs] + lnb_ref[:, cs]
        o_ref[:, cs] = (yn + bonus) * g_rw[:, cs]


def _rwkv(p_rwkv, shift_all, st_all, l, bsz, t, prm, shift_acc, st_acc):
    nb, c = _seq_blocking(bsz, t, RWKV_CHUNK, SEQ_BLOCK_ROWS)
    r = nb * c
    nchunk = t // c
    _, incl, strict = _seq_masks(nb, c)
    i = np.arange(r)
    blk = (i[:, None] // 16) == (i[None, :] // 16)
    masks = jnp.asarray(np.stack([incl, strict, blk, np.eye(r, dtype=bool)]).astype(np.float32))
    st_shape = (nb, RWKV_HEADS, RWKV_HD, RWKV_HD)
    sh_shape = (nb, 1, RWKV_PROJ)
    if st_all is None:
        st_in = jnp.zeros((bsz,) + st_shape[1:], F32)
        shift_in = jnp.zeros((bsz,) + sh_shape[1:], F32)
        st_spec = pl.BlockSpec(st_shape, lambda i, j: (i, 0, 0, 0))
        sh_spec = pl.BlockSpec(sh_shape, lambda i, j: (i, 0, 0))
    else:
        st_in, shift_in = st_all, shift_all
        st_spec = pl.BlockSpec((None,) + st_shape, lambda i, j: (l, i, 0, 0, 0))
        sh_spec = pl.BlockSpec((None,) + sh_shape, lambda i, j: (l, i, 0, 0))
    m = bsz * t

    def vec(n):
        return pl.BlockSpec((None, 1, n), lambda i, j: (l, 0, 0))

    def mat(k):
        return pl.BlockSpec((None, k, 1024), lambda i, j: (l, 0, 0))

    return pl.pallas_call(
        functools.partial(_rwkv_kernel, nb=nb, c=c),
        grid=(bsz // nb, nchunk),
        in_specs=[
            pl.BlockSpec((r, RWKV_PROJ), lambda i, j: (i * nchunk + j, 0)),
            vec(RWKV_PROJ), vec(1024), mat(64), vec(1024), mat(64), mat(128),
            vec(1024), vec(1024), vec(1024), vec(1024), vec(1024),
            pl.BlockSpec((4, r, r), lambda i, j: (0, 0, 0)),
            sh_spec, st_spec,
            pl.BlockSpec(memory_space=pl.ANY), pl.BlockSpec(memory_space=pl.ANY),
        ],
        out_specs=[
            pl.BlockSpec((r, 1024), lambda i, j: (i * nchunk + j, 0)),
            pl.BlockSpec((None,) + sh_shape, lambda i, j: (l, i, 0, 0)),
            pl.BlockSpec((None,) + st_shape, lambda i, j: (l, i, 0, 0, 0)),
        ],
        out_shape=[
            jax.ShapeDtypeStruct((m, 1024), F32),
            jax.ShapeDtypeStruct(shift_acc.shape, F32),
            jax.ShapeDtypeStruct(st_acc.shape, F32),
        ],
        input_output_aliases={15: 1, 16: 2},
        scratch_shapes=[pltpu.VMEM((nb, 8 + c, RWKV_PROJ), F32)],
        compiler_params=_params(("parallel", "arbitrary")),
        name="rwkv7",
    )(p_rwkv, prm["rwkv_mu"], prm["rwkv_w0"], prm["rwkv_w_w2"], prm["rwkv_a0"], prm["rwkv_w_a2"],
      prm["rwkv_w_g2"], prm["rwkv_k_k"], prm["rwkv_k_a"], prm["rwkv_r_k"], prm["rwkv_lnx_g"],
      prm["rwkv_lnx_b"], masks, shift_in, st_in, shift_acc, st_acc)


def _merge_kernel(a_ref, b_ref, c_ref, g_ref, x_ref, wb_ref, wo_ref, o_ref):
    merged = None
    for i, br in enumerate((a_ref, b_ref, c_ref)):
        gate = jax.nn.sigmoid(g_ref[:, i * 1024:(i + 1) * 1024])
        term = gate * jnp.dot(br[...].astype(BF16), wb_ref[i], preferred_element_type=F32)
        merged = term if merged is None else merged + term
    o_ref[...] = x_ref[...] + jnp.dot(merged.astype(BF16), wo_ref[...], preferred_element_type=F32)


def _merge(out_a, out_b, out_c, gates, x, l, wb, wo):
    m, d = x.shape
    tm = min(m, 256)
    row = lambda i: (i, 0)
    return pl.pallas_call(
        _merge_kernel,
        grid=(m // tm,),
        in_specs=[
            pl.BlockSpec((tm, d), row), pl.BlockSpec((tm, d), row), pl.BlockSpec((tm, d), row),
            pl.BlockSpec((tm, GATE_PROJ), row), pl.BlockSpec((tm, d), row),
            pl.BlockSpec((None, 3, d, d), lambda i: (l, 0, 0, 0)),
            pl.BlockSpec((None, d, d), lambda i: (l, 0, 0)),
        ],
        out_specs=pl.BlockSpec((tm, d), row),
        out_shape=jax.ShapeDtypeStruct((m, d), F32),
        compiler_params=_params(("parallel",)),
        name="merge",
    )(out_a, out_b, out_c, gates, x, wb, wo)


def _attn_kernel(q_ref, k_ref, v_ref, o_ref, *, nb, c):
    rows = [slice(s * c, (s + 1) * c) for s in range(nb)]
    sc = [_dot_nt(q_ref[rows[s], :], k_ref[s]) * (X_HEAD_DIM ** -0.5) for s in range(nb)]
    e = [jnp.exp(x - jnp.max(x, axis=-1, keepdims=True)) for x in sc]
    pr = [x * (1.0 / jnp.sum(x, axis=-1, keepdims=True)) for x in e]
    for s in range(nb):
        o_ref[rows[s], :] = _dot(pr[s], v_ref[s])


def _attn(q, mk, mv, l, bsz, t):
    if t > ATTN_ROWS:
        nb, c = 1, ATTN_ROWS
    else:
        nb, c = max(1, min(bsz, SEQ_BLOCK_ROWS // t)), t
    r = nb * c
    ntile = t // c
    hd = X_HEAD_DIM
    if mk.ndim == 3:
        kv_spec = pl.BlockSpec((nb, N_MEM, hd), lambda i, j, h: (i, 0, h))
    else:
        kv_spec = pl.BlockSpec((None, nb, N_MEM, hd), lambda i, j, h: (l, i, 0, h))
    return pl.pallas_call(
        functools.partial(_attn_kernel, nb=nb, c=c),
        grid=(bsz // nb, ntile, X_HEADS),
        in_specs=[pl.BlockSpec((r, hd), lambda i, j, h: (i * ntile + j, h)), kv_spec, kv_spec],
        out_specs=pl.BlockSpec((r, hd), lambda i, j, h: (i * ntile + j, h)),
        out_shape=jax.ShapeDtypeStruct((bsz * t, 1024), F32),
        compiler_params=_params(("parallel", "arbitrary", "arbitrary")),
        name="xattn_core",
    )(q, mk, mv)


def _trunk(x, prm, states, mem_k, mem_v, bsz, t, offset):
    acc_ret = jnp.zeros((DEPTH, bsz, RET_HEADS, RET_DK, RET_DK), F32)
    acc_ssm = jnp.zeros((DEPTH, bsz, SSM_HEADS, SSM_HEADDIM, SSM_STATE), F32)
    acc_conv = jnp.zeros((DEPTH, bsz, CONV_W - 1, CONV_DIM), F32)
    acc_rwkv = jnp.zeros((DEPTH, bsz, RWKV_HEADS, RWKV_HD, RWKV_HD), F32)
    acc_shift = jnp.zeros((DEPTH, bsz, 1, RWKV_PROJ), F32)
    for l in range(DEPTH):
        x = _ffn(x, prm["ffa_norm"], l, prm["ffa_w1"], prm["ffa_w3"], prm["ffa_w2"])
        p_ret = _norm_matmul(x, prm["mix_norm"], l, prm["w_ret"], 0, RET_PROJ)
        p_ssm = _norm_matmul(x, prm["mix_norm"], l, prm["w_ssm"], 0, SSM_PROJ_PAD)
        p_rwkv = _norm_matmul(x, prm["mix_norm"], l, prm["w_rwkv"], 0, RWKV_PROJ)
        p_gate = _norm_matmul(x, prm["mix_norm"], l, prm["w_gate"], 0, GATE_PROJ)
        if states is None:
            s_ret = s_ssm = s_conv = s_rwkv = s_shift = None
        else:
            s_ret, s_ssm, s_conv, s_rwkv, s_shift = states
        out_a, acc_ret = _retention(p_ret, s_ret, l, bsz, t, offset, acc_ret)
        out_b, acc_conv, acc_ssm = _ssd(p_ssm, s_conv, s_ssm, l, bsz, t, prm, acc_conv, acc_ssm)
        out_c, acc_shift, acc_rwkv = _rwkv(p_rwkv, s_shift, s_rwkv, l, bsz, t, prm, acc_shift, acc_rwkv)
        x = _merge(out_a, out_b, out_c, p_gate, x, l, prm["w_branch"], prm["w_out"])
        q = _norm_matmul(x, prm["xattn_norm"], l, prm["xattn_wq"], 0, 1024)
        if isinstance(mem_k, (list, tuple)):
            att = _attn(q, mem_k[l], mem_v[l], l, bsz, t)
        else:
            att = _attn(q, mem_k, mem_v, l, bsz, t)
        x = _matmul_residual(att, l, prm["xattn_wo"], x)
        x = _ffn(x, prm["ffb_norm"], l, prm["ffb_w1"], prm["ffb_w3"], prm["ffb_w2"])
    y = _final_norm(x, prm["final_norm"])
    return y, (acc_ret, acc_ssm, acc_conv, acc_rwkv, acc_shift)


def _prep_params(raw):
    p = {}
    for name in ("ffa_w1", "ffa_w3", "ffa_w2", "ffb_w1", "ffb_w3", "ffb_w2", "w_branch", "w_out",
                 "xattn_wq", "xattn_wk", "xattn_wv", "xattn_wo", "rwkv_w_w2", "rwkv_w_a2", "rwkv_w_g2"):
        p[name] = raw[name].astype(BF16)
    w_in = raw["w_in"]
    c0, c1, c2 = RET_PROJ, RET_PROJ + SSM_PROJ, RET_PROJ + SSM_PROJ + RWKV_PROJ
    p["w_ret"] = w_in[:, :, :c0].astype(BF16)
    p["w_ssm"] = jnp.pad(w_in[:, :, c0:c1], ((0, 0), (0, 0), (0, SSM_PROJ_PAD - SSM_PROJ))).astype(BF16)
    p["w_rwkv"] = w_in[:, :, c1:c2].astype(BF16)
    p["w_gate"] = w_in[:, :, c2:].astype(BF16)
    for name in ("ffa_norm", "mix_norm", "ssm_conv_b", "ssm_norm", "rwkv_mu", "rwkv_w0", "rwkv_a0",
                 "rwkv_k_k", "rwkv_k_a", "rwkv_lnx_g", "rwkv_lnx_b", "xattn_norm", "mem_norm", "ffb_norm"):
        p[name] = raw[name][:, None, :]
    p["rwkv_r_k"] = raw["rwkv_r_k"].reshape(DEPTH, 1, 1024)
    pad = ((0, 0), (0, LANES - SSM_HEADS))
    p["ssm_dt_bias"] = jnp.pad(raw["ssm_dt_bias"], pad)[:, None, :]
    p["ssm_a_log"] = jnp.pad(raw["ssm_a_log"], pad)[:, None, :]
    p["ssm_d"] = jnp.repeat(raw["ssm_d"], SSM_HEADDIM, axis=-1)[:, None, :]
    p["ssm_conv_w"] = raw["ssm_conv_w"]
    p["final_norm"] = raw["final_norm"][None, :]
    return p


def _run(x_prompt, x_sample, mem_prompt, states, cache_mem_k, cache_mem_v, raw):
    prm = _prep_params(raw)
    b, t, d = x_prompt.shape
    db, dt_, _ = x_sample.shape
    n_mem = mem_prompt.shape[1]
    mem2 = mem_prompt.reshape(b * n_mem, d)
    mk = [_norm_matmul(mem2, prm["mem_norm"], l, prm["xattn_wk"], 0, 1024) for l in range(DEPTH)]
    mv = [_norm_matmul(mem2, prm["mem_norm"], l, prm["xattn_wv"], 0, 1024) for l in range(DEPTH)]
    mk3 = [a.reshape(b, n_mem, d) for a in mk]
    mv3 = [a.reshape(b, n_mem, d) for a in mv]
    p_mem_k = jnp.stack(mk).reshape(DEPTH, b, n_mem, X_HEADS, X_HEAD_DIM)
    p_mem_v = jnp.stack(mv).reshape(DEPTH, b, n_mem, X_HEADS, X_HEAD_DIM)
    y_p, st_p = _trunk(x_prompt.reshape(b * t, d), prm, None, mk3, mv3, b, t, 0)

    ck = cache_mem_k.reshape(DEPTH, db, n_mem, d)
    cv = cache_mem_v.reshape(DEPTH, db, n_mem, d)
    y_s, st_s = _trunk(x_sample.reshape(db * dt_, d), prm, states, ck, cv, db, dt_, PAST_LEN)
    return (y_p.reshape(b, t, d), y_s.reshape(db, dt_, d)) + st_p + (p_mem_k, p_mem_v) + st_s


def kernel(x_prompt, x_sample, mem_prompt, state_ret, state_ssm, state_conv, state_rwkv, state_shift, cache_mem_k, cache_mem_v, ffa_norm, ffa_w1, ffa_w3, ffa_w2, mix_norm, w_in, ssm_conv_w, ssm_conv_b, ssm_dt_bias, ssm_a_log, ssm_d, ssm_norm, rwkv_mu, rwkv_w0, rwkv_w_w2, rwkv_a0, rwkv_w_a2, rwkv_w_g2, rwkv_k_k, rwkv_k_a, rwkv_r_k, rwkv_lnx_g, rwkv_lnx_b, w_branch, w_out, xattn_norm, mem_norm, xattn_wq, xattn_wk, xattn_wv, xattn_wo, ffb_norm, ffb_w1, ffb_w3, ffb_w2, final_norm):
    raw = dict(ffa_norm=ffa_norm, ffa_w1=ffa_w1, ffa_w3=ffa_w3, ffa_w2=ffa_w2, mix_norm=mix_norm, w_in=w_in,
               ssm_conv_w=ssm_conv_w, ssm_conv_b=ssm_conv_b, ssm_dt_bias=ssm_dt_bias, ssm_a_log=ssm_a_log,
               ssm_d=ssm_d, ssm_norm=ssm_norm, rwkv_mu=rwkv_mu, rwkv_w0=rwkv_w0, rwkv_w_w2=rwkv_w_w2,
               rwkv_a0=rwkv_a0, rwkv_w_a2=rwkv_w_a2, rwkv_w_g2=rwkv_w_g2, rwkv_k_k=rwkv_k_k,
               rwkv_k_a=rwkv_k_a, rwkv_r_k=rwkv_r_k, rwkv_lnx_g=rwkv_lnx_g, rwkv_lnx_b=rwkv_lnx_b,
               w_branch=w_branch, w_out=w_out, xattn_norm=xattn_norm, mem_norm=mem_norm,
               xattn_wq=xattn_wq, xattn_wk=xattn_wk, xattn_wv=xattn_wv, xattn_wo=xattn_wo,
               ffb_norm=ffb_norm, ffb_w1=ffb_w1, ffb_w3=ffb_w3, ffb_w2=ffb_w2, final_norm=final_norm)
    states = (state_ret, state_ssm, state_conv, state_rwkv, state_shift)
    return _run(x_prompt, x_sample, mem_prompt, states, cache_mem_k, cache_mem_v, raw)
```

```python
import functools
import math

import numpy as np
import jax
import jax.numpy as jnp
from jax import lax
from jax.experimental import pallas as pl
from jax.experimental.pallas import tpu as pltpu

F32 = jnp.float32
BF16 = jnp.bfloat16

D_MODEL = 1024
DEPTH = 4
PAST_LEN = 16384
D_FF = 2816
EPS = 1e-6
RET_HEADS = 8
RET_DK = 128
RET_GN_EPS = 1e-5
ROPE_BASE = 10000.0
SSM_HEADS = 16
SSM_HEADDIM = 64
SSM_GROUPS = 2
SSM_STATE = 128
CONV_W = 4
CONV_DIM = 1536
RWKV_HEADS = 16
RWKV_HD = 64
RWKV_PROJ = 3328
RWKV_GN_EPS = 64e-5
N_MEM = 256
X_HEADS = 4
X_HEAD_DIM = 256
RET_PROJ = 4096
SSM_PROJ = 2576
SSM_PROJ_PAD = 2688
GATE_PROJ = 3072
LANES = 128
VMEM_LIMIT = 56 * 1024 * 1024

RET_CHUNK = 128
SSD_CHUNK = 128
RWKV_CHUNK = 64
ATTN_ROWS = 2048
SEQ_BLOCK_ROWS = 64


def _params(sem):
    return pltpu.CompilerParams(dimension_semantics=sem, vmem_limit_bytes=VMEM_LIMIT)


def _rms(x, g):
    return x * lax.rsqrt(jnp.mean(x * x, axis=-1, keepdims=True) + EPS) * g


def _head_norm(x, eps):
    mu = jnp.mean(x, axis=-1, keepdims=True)
    xc = x - mu
    return xc * lax.rsqrt(jnp.mean(xc * xc, axis=-1, keepdims=True) + eps)


def _silu(x):
    return x * jax.nn.sigmoid(x)


def _dot(a, b):
    return jnp.dot(a.astype(BF16), b.astype(BF16), preferred_element_type=F32)


def _dot_nt(a, b):
    return lax.dot_general(a.astype(BF16), b.astype(BF16), (((1,), (1,)), ((), ())),
                           preferred_element_type=F32)


def _dot_tn(a, b):
    return lax.dot_general(a.astype(BF16), b.astype(BF16), (((0,), (0,)), ((), ())),
                           preferred_element_type=F32)


def _dot_f32(a, b):
    return jnp.dot(a, b, precision=lax.Precision.HIGHEST, preferred_element_type=F32)


def _seq_blocking(bsz, t, chunk_max, block_rows):
    if t > chunk_max:
        assert t % chunk_max == 0
        return 1, chunk_max
    nb = max(1, min(bsz, block_rows // t))
    assert bsz % nb == 0
    return nb, t


def _seq_masks(nb, c):
    r = nb * c
    i = np.arange(r)
    same = (i[:, None] // c) == (i[None, :] // c)
    incl = same & (i[:, None] >= i[None, :])
    strict = same & (i[:, None] > i[None, :])
    return same, incl, strict


def _nm_kernel(x_ref, g_ref, w_ref, o_ref, h_ref):
    @pl.when(pl.program_id(1) == 0)
    def _():
        h_ref[...] = _rms(x_ref[...], g_ref[...]).astype(BF16)

    o_ref[...] = jnp.dot(h_ref[...], w_ref[...], preferred_element_type=F32)


def _norm_matmul(x, g, l, w, col0, ncols):
    m, d = x.shape
    tn = ncols
    tm = min(m, 512)
    assert col0 % tn == 0 and m % tm == 0
    cb = col0 // tn
    return pl.pallas_call(
        _nm_kernel,
        grid=(m // tm, ncols // tn),
        in_specs=[
            pl.BlockSpec((tm, d), lambda i, j: (i, 0)),
            pl.BlockSpec((None, 1, d), lambda i, j: (l, 0, 0)),
            pl.BlockSpec((None, d, tn), lambda i, j: (l, 0, cb + j)),
        ],
        out_specs=pl.BlockSpec((tm, tn), lambda i, j: (i, j)),
        out_shape=jax.ShapeDtypeStruct((m, ncols), F32),
        scratch_shapes=[pltpu.VMEM((tm, d), BF16)],
        compiler_params=_params(("parallel", "arbitrary")),
        name="norm_matmul",
    )(x, g, w)


def _mr_kernel(a_ref, w_ref, x_ref, o_ref):
    o_ref[...] = x_ref[...] + jnp.dot(a_ref[...].astype(BF16), w_ref[...],
                                       preferred_element_type=F32)


def _matmul_residual(a, l, w, x):
    m, k = a.shape
    n = w.shape[-1]
    tm = min(m, 512)
    return pl.pallas_call(
        _mr_kernel,
        grid=(m // tm,),
        in_specs=[
            pl.BlockSpec((tm, k), lambda i: (i, 0)),
            pl.BlockSpec((None, k, n), lambda i: (l, 0, 0)),
            pl.BlockSpec((tm, n), lambda i: (i, 0)),
        ],
        out_specs=pl.BlockSpec((tm, n), lambda i: (i, 0)),
        out_shape=jax.ShapeDtypeStruct((m, n), F32),
        compiler_params=_params(("parallel",)),
        name="matmul_residual",
    )(a, w, x)


def _ffn_kernel(x_ref, g_ref, w1_ref, w3_ref, w2_ref, o_ref, h_ref, acc_ref):
    j = pl.program_id(1)

    @pl.when(j == 0)
    def _():
        h_ref[...] = _rms(x_ref[...], g_ref[...]).astype(BF16)
        acc_ref[...] = jnp.zeros_like(acc_ref)

    h = h_ref[...]
    a = jnp.dot(h, w1_ref[...], preferred_element_type=F32)
    b = jnp.dot(h, w3_ref[...], preferred_element_type=F32)
    u = (_silu(a) * b).astype(BF16)
    acc_ref[...] += jnp.dot(u, w2_ref[...], preferred_element_type=F32)

    @pl.when(j == pl.num_programs(1) - 1)
    def _():
        o_ref[...] = x_ref[...] + 0.5 * acc_ref[...]


def _ffn(x, g, l, w1, w3, w2):
    m, d = x.shape
    f = w1.shape[-1]
    tm = min(m, 1024)
    tf = 256
    return pl.pallas_call(
        _ffn_kernel,
        grid=(m // tm, f // tf),
        in_specs=[
            pl.BlockSpec((tm, d), lambda i, j: (i, 0)),
            pl.BlockSpec((None, 1, d), lambda i, j: (l, 0, 0)),
            pl.BlockSpec((None, d, tf), lambda i, j: (l, 0, j)),
            pl.BlockSpec((None, d, tf), lambda i, j: (l, 0, j)),
            pl.BlockSpec((None, tf, d), lambda i, j: (l, j, 0)),
        ],
        out_specs=pl.BlockSpec((tm, d), lambda i, j: (i, 0)),
        out_shape=jax.ShapeDtypeStruct((m, d), F32),
        scratch_shapes=[pltpu.VMEM((tm, d), BF16), pltpu.VMEM((tm, d), F32)],
        compiler_params=_params(("parallel", "arbitrary")),
        name="ffn",
    )(x, g, w1, w3, w2)


def _norm_kernel(x_ref, g_ref, o_ref):
    o_ref[...] = _rms(x_ref[...], g_ref[...])


def _final_norm(x, g):
    m, d = x.shape
    tm = min(m, 1024)
    return pl.pallas_call(
        _norm_kernel,
        grid=(m // tm,),
        in_specs=[pl.BlockSpec((tm, d), lambda i: (i, 0)), pl.BlockSpec((1, d), lambda i: (0, 0))],
        out_specs=pl.BlockSpec((tm, d), lambda i: (i, 0)),
        out_shape=jax.ShapeDtypeStruct((m, d), F32),
        compiler_params=_params(("parallel",)),
        name="final_norm",
    )(x, g)


def _ret_kernel(p_ref, cos_ref, sin_ref, dmat_ref, qdec_ref, kdec_ref, st_in_ref, acc_ref,
                o_ref, st_ref, *, nb, c, sdec):
    del acc_ref
    @pl.when(pl.program_id(1) == 0)
    def _():
        st_ref[...] = st_in_ref[...]

    cosf = cos_ref[...]
    sinf = sin_ref[...]
    hd = RET_DK
    heads = range(RET_HEADS)
    rows = [slice(s * c, (s + 1) * c) for s in range(nb)]
    qr, kr, v = [], [], []
    for h in heads:
        q = p_ref[:, h * hd:(h + 1) * hd]
        k = p_ref[:, 1024 + h * hd:1024 + (h + 1) * hd]
        qr.append(((q * cosf + pltpu.roll(q, hd // 2, 1) * sinf) * (RET_DK ** -0.5)).astype(BF16))
        kr.append(k * cosf + pltpu.roll(k, hd // 2, 1) * sinf)
        v.append(p_ref[:, 2048 + h * hd:2048 + (h + 1) * hd].astype(BF16))
    scores = [_dot_nt(qr[h], kr[h]) * dmat_ref[h] for h in heads]
    inter = [[_dot(qr[h][rw], st_ref[s, h]) for s, rw in enumerate(rows)] for h in heads]
    o = [_dot(scores[h], v[h]) for h in heads]
    for h in heads:
        kd = kr[h] * kdec_ref[h]
        qdec = qdec_ref[h]
        for s, rw in enumerate(rows):
            o_s = o[h][rw] + inter[h][s] * qdec[rw]
            g = p_ref[rw, 3072 + h * hd:3072 + (h + 1) * hd]
            o_ref[rw, h * hd:(h + 1) * hd] = _silu(g) * _head_norm(o_s, RET_GN_EPS)
            st_ref[s, h] = st_ref[s, h] * sdec[h] + _dot_tn(kd[rw], v[h][rw])


def _retention(p_ret, st_all, l, bsz, t, offset, acc):
    nb, c = _seq_blocking(bsz, t, RET_CHUNK, SEQ_BLOCK_ROWS)
    r = nb * c
    nchunk = t // c
    heads = np.arange(RET_HEADS, dtype=np.float64)
    log_g = np.log1p(-np.exp2(-5.0 - heads))
    idx = np.arange(r) % c
    _, incl, _ = _seq_masks(nb, c)
    diff = (idx[:, None] - idx[None, :]).astype(np.float64)
    dmat = np.where(incl[None], np.exp(np.maximum(diff, 0.0)[None] * log_g[:, None, None]), 0.0)
    qdec = np.exp((idx + 1.0)[None, :] * log_g[:, None])
    kdec = np.exp((c - 1.0 - idx)[None, :] * log_g[:, None])
    qdec = np.broadcast_to(qdec[:, :, None], (RET_HEADS, r, RET_DK))
    kdec = np.broadcast_to(kdec[:, :, None], (RET_HEADS, r, RET_DK))
    sdec = tuple(float(x) for x in np.exp(c * log_g))

    half = RET_DK // 2
    freqs = ROPE_BASE ** (-jnp.arange(half, dtype=F32) / half)
    pos = jnp.float32(offset) + jnp.arange(t, dtype=F32)
    ang = pos[:, None] * freqs[None, :]
    cos = jnp.cos(ang)
    sin = jnp.sin(ang)
    cosf = jnp.tile(jnp.concatenate([cos, cos], axis=-1), (nb, 1))
    sinf = jnp.tile(jnp.concatenate([-sin, sin], axis=-1), (nb, 1))

    st_shape = (nb, RET_HEADS, RET_DK, RET_DK)
    if st_all is None:
        st_in = jnp.zeros((bsz,) + st_shape[1:], F32)
        st_spec = pl.BlockSpec(st_shape, lambda i, j: (i, 0, 0, 0))
    else:
        st_in = st_all
        st_spec = pl.BlockSpec((None,) + st_shape, lambda i, j: (l, i, 0, 0, 0))
    m = bsz * t
    return pl.pallas_call(
        functools.partial(_ret_kernel, nb=nb, c=c, sdec=sdec),
        grid=(bsz // nb, nchunk),
        in_specs=[
            pl.BlockSpec((r, RET_PROJ), lambda i, j: (i * nchunk + j, 0)),
            pl.BlockSpec((r, RET_DK), lambda i, j: (j, 0)),
            pl.BlockSpec((r, RET_DK), lambda i, j: (j, 0)),
            pl.BlockSpec((RET_HEADS, r, r), lambda i, j: (0, 0, 0)),
            pl.BlockSpec((RET_HEADS, r, RET_DK), lambda i, j: (0, 0, 0)),
            pl.BlockSpec((RET_HEADS, r, RET_DK), lambda i, j: (0, 0, 0)),
            st_spec,
            pl.BlockSpec(memory_space=pl.ANY),
        ],
        out_specs=[
            pl.BlockSpec((r, 1024), lambda i, j: (i * nchunk + j, 0)),
            pl.BlockSpec((None,) + st_shape, lambda i, j: (l, i, 0, 0, 0)),
        ],
        out_shape=[
            jax.ShapeDtypeStruct((m, 1024), F32),
            jax.ShapeDtypeStruct(acc.shape, F32),
        ],
        input_output_aliases={7: 1},
        compiler_params=_params(("parallel", "arbitrary")),
        name="retention",
    )(p_ret, cosf, sinf, jnp.asarray(dmat, F32), jnp.asarray(qdec, F32), jnp.asarray(kdec, F32), st_in, acc)


def _ssd_kernel(p_ref, cw_ref, cb_ref, dtb_ref, alog_ref, dfull_ref, norm_ref, tri_ref,
                conv_in_ref, st_in_ref, conv_acc_ref, st_acc_ref,
                o_ref, conv_out_ref, st_ref, ext_ref, y_ref, *, nb, c):
    del conv_acc_ref, st_acc_ref
    r = nb * c

    @pl.when(pl.program_id(1) == 0)
    def _():
        st_ref[...] = st_in_ref[...]
        for s in range(nb):
            ext_ref[s, 5:8, :] = conv_in_ref[s]

    for s in range(nb):
        ext_ref[s, 8:8 + c, :] = p_ref[s * c:(s + 1) * c, 1024:1024 + CONV_DIM]
    pieces = []
    for s in range(nb):
        acc = cb_ref[...] + ext_ref[s, 5:5 + c, :] * cw_ref[0:1, :]
        for j in range(1, CONV_W):
            acc = acc + ext_ref[s, 5 + j:5 + j + c, :] * cw_ref[j:j + 1, :]
        pieces.append(acc)
    for s in range(nb):
        conv_out_ref[s] = ext_ref[s, c + 5:c + 8, :]
        ext_ref[s, 0:8, :] = ext_ref[s, c:c + 8, :]
    xbc = _silu(pieces[0] if nb == 1 else jnp.concatenate(pieces, axis=0))
    xs = xbc[:, :1024]
    bm = xbc[:, 1024:1024 + SSM_GROUPS * SSM_STATE]
    cm = xbc[:, 1024 + SSM_GROUPS * SSM_STATE:]

    dt = jax.nn.softplus(p_ref[:, 1024 + CONV_DIM:] + dtb_ref[...])
    a = -jnp.exp(alog_ref[...])
    tri = tri_ref[...]
    cum = _dot_f32(tri, dt * a)
    cum_t = cum.T
    dt_t = dt.T
    mask = tri > 0.5

    hp = SSM_HEADDIM
    hg = SSM_HEADS // SSM_GROUPS
    for g in range(SSM_GROUPS):
        bm_g = bm[:, g * SSM_STATE:(g + 1) * SSM_STATE]
        cm_g = cm[:, g * SSM_STATE:(g + 1) * SSM_STATE]
        cb = _dot_nt(cm_g, bm_g)
        for hh in range(hg):
            h = g * hg + hh
            ci = cum[:, h:h + 1]
            seg = ci - cum_t[h:h + 1, :]
            lmat = jnp.where(mask, jnp.exp(jnp.where(mask, seg, 0.0)), 0.0)
            scores = cb * lmat * dt_t[h:h + 1, :]
            xs_h = xs[:, h * hp:(h + 1) * hp]
            y = _dot(scores, xs_h)
            eci = jnp.exp(ci)
            dt_h = dt[:, h:h + 1]
            for s in range(nb):
                rows = slice(s * c, (s + 1) * c)
                st = st_ref[s, h]
                last = cum[s * c + c - 1:s * c + c, h:h + 1]
                y_s = y[rows] + _dot_nt(cm_g[rows], st) * eci[rows]
                wts = jnp.exp(last - ci[rows]) * dt_h[rows]
                st_ref[s, h] = st * jnp.exp(last) + _dot_tn(xs_h[rows] * wts, bm_g[rows])
                y_ref[rows, h * hp:(h + 1) * hp] = y_s
    y = y_ref[...] + dfull_ref[...] * xs
    z = p_ref[:, :1024]
    o_ref[...] = _rms(y * _silu(z), norm_ref[...])


def _ssd(p_ssm, conv_all, st_all, l, bsz, t, prm, conv_acc, st_acc):
    nb, c = _seq_blocking(bsz, t, SSD_CHUNK, SEQ_BLOCK_ROWS)
    r = nb * c
    nchunk = t // c
    _, incl, _ = _seq_masks(nb, c)
    tri = jnp.asarray(incl.astype(np.float32))
    st_shape = (nb, SSM_HEADS, SSM_HEADDIM, SSM_STATE)
    cv_shape = (nb, CONV_W - 1, CONV_DIM)
    if st_all is None:
        st_in = jnp.zeros((bsz,) + st_shape[1:], F32)
        conv_in = jnp.zeros((bsz,) + cv_shape[1:], F32)
        st_spec = pl.BlockSpec(st_shape, lambda i, j: (i, 0, 0, 0))
        cv_spec = pl.BlockSpec(cv_shape, lambda i, j: (i, 0, 0))
    else:
        st_in, conv_in = st_all, conv_all
        st_spec = pl.BlockSpec((None,) + st_shape, lambda i, j: (l, i, 0, 0, 0))
        cv_spec = pl.BlockSpec((None,) + cv_shape, lambda i, j: (l, i, 0, 0))
    m = bsz * t

    def vec(n):
        return pl.BlockSpec((None, 1, n), lambda i, j: (l, 0, 0))

    return pl.pallas_call(
        functools.partial(_ssd_kernel, nb=nb, c=c),
        grid=(bsz // nb, nchunk),
        in_specs=[
            pl.BlockSpec((r, SSM_PROJ_PAD), lambda i, j: (i * nchunk + j, 0)),
            pl.BlockSpec((None, CONV_W, CONV_DIM), lambda i, j: (l, 0, 0)),
            vec(CONV_DIM), vec(LANES), vec(LANES), vec(1024), vec(1024),
            pl.BlockSpec((r, r), lambda i, j: (0, 0)),
            cv_spec, st_spec,
            pl.BlockSpec(memory_space=pl.ANY), pl.BlockSpec(memory_space=pl.ANY),
        ],
        out_specs=[
            pl.BlockSpec((r, 1024), lambda i, j: (i * nchunk + j, 0)),
            pl.BlockSpec((None,) + cv_shape, lambda i, j: (l, i, 0, 0)),
            pl.BlockSpec((None,) + st_shape, lambda i, j: (l, i, 0, 0, 0)),
        ],
        out_shape=[
            jax.ShapeDtypeStruct((m, 1024), F32),
            jax.ShapeDtypeStruct(conv_acc.shape, F32),
            jax.ShapeDtypeStruct(st_acc.shape, F32),
        ],
        input_output_aliases={10: 1, 11: 2},
        scratch_shapes=[pltpu.VMEM((nb, 8 + c, CONV_DIM), F32), pltpu.VMEM((r, 1024), F32)],
        compiler_params=_params(("parallel", "arbitrary")),
        name="ssd",
    )(p_ssm, prm["ssm_conv_w"], prm["ssm_conv_b"], prm["ssm_dt_bias"], prm["ssm_a_log"],
      prm["ssm_d"], prm["ssm_norm"], tri, conv_in, st_in, conv_acc, st_acc)


def _rwkv_kernel(p_ref, mu_ref, w0_ref, ww2_ref, a0_ref, wa2_ref, wg2_ref, kk_ref, ka_ref,
                 rk_ref, lng_ref, lnb_ref, masks_ref, shift_in_ref, st_in_ref, shift_acc_ref, st_acc_ref,
                 o_ref, shift_out_ref, st_ref, ext_ref, bd_ref, *, nb, c):
    del shift_acc_ref, st_acc_ref
    r = nb * c
    hd = RWKV_HD
    npair = RWKV_HEADS // 2
    pairs = range(npair)
    seqs = range(nb)
    rows = [slice(s * c, (s + 1) * c) for s in seqs]
    lane = lax.broadcasted_iota(jnp.int32, (1, 2 * hd), 1)
    left = lane < hd
    sub = lax.broadcasted_iota(jnp.int32, (2 * hd, 1), 0)
    bd_mask = (sub < hd) == left

    @pl.when(pl.program_id(1) == 0)
    def _():
        zero = jnp.zeros((hd, hd), F32)
        for s in seqs:
            ext_ref[s, 7:8, :] = shift_in_ref[s]
            for p in pairs:
                top = jnp.concatenate([st_in_ref[s, 2 * p], zero], axis=1)
                bot = jnp.concatenate([zero, st_in_ref[s, 2 * p + 1]], axis=1)
                bd_ref[s, p] = jnp.concatenate([top, bot], axis=0)

    x = p_ref[...]
    for s in range(nb):
        ext_ref[s, 8:8 + c, :] = x[s * c:(s + 1) * c]
    prev = [ext_ref[s, 7:7 + c, :] for s in range(nb)]
    prev = prev[0] if nb == 1 else jnp.concatenate(prev, axis=0)
    for s in range(nb):
        ext_ref[s, 0:8, :] = ext_ref[s, c:c + 8, :]
        shift_out_ref[s] = ext_ref[s, 7:8, :]

    mixed = x + (prev - x) * mu_ref[...]
    rr = mixed[:, 0:1024]
    kc = mixed[:, 1024:2048]
    vc = mixed[:, 2048:3072]
    wd = mixed[:, 3072:3136]
    ad = mixed[:, 3136:3200]
    gd = mixed[:, 3200:3328]
    w_log = -jax.nn.softplus(-(w0_ref[...] + _dot(jnp.tanh(wd), ww2_ref[...]))) - 0.5
    lw = -jnp.exp(w_log)
    a_lr = jax.nn.sigmoid(a0_ref[...] + _dot(ad, wa2_ref[...]))
    g_rw = _dot(jax.nn.sigmoid(gd), wg2_ref[...])
    kk = kc * kk_ref[...]
    kmod = kc * (1.0 + (a_lr - 1.0) * ka_ref[...])
    rkk = rr * kmod * rk_ref[...]

    m_incl = masks_ref[0]
    m_strict = masks_ref[1]
    m_blk = masks_ref[2]
    eye = masks_ref[3]
    cl = _dot_f32(m_incl[:, :r], lw)

    def half_sum(a):
        sl = jnp.sum(jnp.where(left, a, 0.0), axis=-1, keepdims=True)
        sr = jnp.sum(jnp.where(left, 0.0, a), axis=-1, keepdims=True)
        return jnp.where(left, sl, sr)

    def bd(a):
        ab = a.astype(BF16)
        zero = jnp.zeros_like(ab)
        return jnp.concatenate([jnp.where(left, ab, zero), jnp.where(left, zero, ab)], axis=0)

    def lp_dot(a, b):
        return jnp.dot(a.astype(BF16), bd(b), preferred_element_type=F32)

    def cat_rows(pieces):
        return pieces[0] if nb == 1 else jnp.concatenate(pieces, axis=0)

    cols = [slice(p * 2 * hd, (p + 1) * 2 * hd) for p in pairs]
    v_p = [vc[:, cs] for cs in cols]
    kt, rt, kh, bh, e_last = [], [], [], [], []
    for cs in cols:
        kk_p = kk[:, cs]
        kk_p = kk_p * lax.rsqrt(jnp.maximum(half_sum(kk_p * kk_p), 1e-24))
        cl_p = cl[:, cs]
        e_incl = jnp.exp(cl_p)
        e_inv = jnp.exp(-cl_p)
        rt.append(rr[:, cs] * e_incl)
        kt.append(kk_p * jnp.exp(cl_p - lw[:, cs]))
        kh.append(kmod[:, cs] * e_inv)
        bh.append(kk_p * a_lr[:, cs] * e_inv)
        e_last.append([e_incl[s * c + c - 1:s * c + c, :] for s in seqs])
    lhs = [jnp.concatenate([kt[p], rt[p]], axis=0).astype(BF16) for p in pairs]
    gk = [_dot_nt(lhs[p], bd(kh[p])) for p in pairs]
    gb = [_dot_nt(lhs[p], bd(bh[p])) for p in pairs]
    both = [[_dot_nt(jnp.concatenate([kt[p][rw], rt[p][rw]], axis=0) if nb > 1 else lhs[p],
                     bd_ref[s, p]) for s, rw in enumerate(rows)] for p in pairs]
    a_kk = [g[:r] * m_strict for g in gk]
    a_rk = [g[r:] * m_incl for g in gk]
    a_kb = [g[:r] * m_strict for g in gb]
    a_rb = [g[r:] * m_incl for g in gb]
    akv = [lp_dot(a_kk[p], v_p[p]) for p in pairs]

    n1 = [a * m_blk for a in a_kb]
    n2 = [lp_dot(n, n) for n in n1]
    p1 = [lp_dot(eye - n1[p], eye + n2[p]) for p in pairs]
    n4 = [lp_dot(n, n) for n in n2]
    p2 = [lp_dot(p1[p], eye + n4[p]) for p in pairs]
    n8 = [lp_dot(n, n) for n in n4]
    dinv = [lp_dot(p2[p], eye + n8[p]) for p in pairs]
    xm = [lp_dot(dinv[p], a_kb[p] - n1[p]) for p in pairs]
    x2 = [lp_dot(x_, x_) for x_ in xm]
    t1 = [lp_dot(eye - xm[p], eye + x2[p]) for p in pairs]
    tinv = [lp_dot(t1[p], dinv[p]) for p in pairs]

    ks = [cat_rows([b[:c] for b in both[p]]) for p in pairs]
    rs = [cat_rows([b[c:] for b in both[p]]) for p in pairs]
    u = [lp_dot(tinv[p], ks[p] + akv[p]) for p in pairs]
    y = [rs[p] + lp_dot(a_rk[p], v_p[p]) - lp_dot(a_rb[p], u[p]) for p in pairs]
    for p in pairs:
        for s, rw in enumerate(rows):
            upd = _dot_tn(jnp.concatenate([v_p[p][rw], -u[p][rw]], axis=0),
                          jnp.concatenate([kh[p][rw], bh[p][rw]], axis=0))
            bd_ref[s, p] = (bd_ref[s, p] + jnp.where(bd_mask, upd, 0.0)) * e_last[p][s]
    for p, cs in enumerate(cols):
        bonus = half_sum(rkk[:, cs]) * v_p[p]
        yc = y[p] - half_sum(y[p]) * (1.0 / hd)
        yn = yc * lax.rsqrt(half_sum(yc * yc) * (1.0 / hd) + RWKV_GN_EPS)
        o_ref[:, cs] = (yn * lng_ref[:, cs] + lnb_ref[:, cs] + bonus) * g_rw[:, cs]

    @pl.when(pl.program_id(1) == pl.num_programs(1) - 1)
    def _():
        for s in seqs:
            for p in pairs:
                blk = bd_ref[s, p]
                st_ref[s, 2 * p] = blk[:hd, :hd]
                st_ref[s, 2 * p + 1] = blk[hd:, hd:]


def _rwkv(p_rwkv, shift_all, st_all, l, bsz, t, prm, shift_acc, st_acc):
    nb, c = _seq_blocking(bsz, t, RWKV_CHUNK, SEQ_BLOCK_ROWS)
    r = nb * c
    nchunk = t // c
    assert 2 * r == LANES, "the lane-paired layout holds two (r, r) matrices side by side"
    _, incl, strict = _seq_masks(nb, c)
    i = np.arange(r)
    blk = (i[:, None] // 16) == (i[None, :] // 16)
    masks = np.stack([incl, strict, blk, np.eye(r, dtype=bool)]).astype(np.float32)
    masks = jnp.asarray(np.tile(masks, (1, 1, 2)))
    st_shape = (nb, RWKV_HEADS, RWKV_HD, RWKV_HD)
    sh_shape = (nb, 1, RWKV_PROJ)
    if st_all is None:
        st_in = jnp.zeros((bsz,) + st_shape[1:], F32)
        shift_in = jnp.zeros((bsz,) + sh_shape[1:], F32)
        st_spec = pl.BlockSpec(st_shape, lambda i, j: (i, 0, 0, 0))
        sh_spec = pl.BlockSpec(sh_shape, lambda i, j: (i, 0, 0))
    else:
        st_in, shift_in = st_all, shift_all
        st_spec = pl.BlockSpec((None,) + st_shape, lambda i, j: (l, i, 0, 0, 0))
        sh_spec = pl.BlockSpec((None,) + sh_shape, lambda i, j: (l, i, 0, 0))
    m = bsz * t

    def vec(n):
        return pl.BlockSpec((None, 1, n), lambda i, j: (l, 0, 0))

    def mat(k):
        return pl.BlockSpec((None, k, 1024), lambda i, j: (l, 0, 0))

    return pl.pallas_call(
        functools.partial(_rwkv_kernel, nb=nb, c=c),
        grid=(bsz // nb, nchunk),
        in_specs=[
            pl.BlockSpec((r, RWKV_PROJ), lambda i, j: (i * nchunk + j, 0)),
            vec(RWKV_PROJ), vec(1024), mat(64), vec(1024), mat(64), mat(128),
            vec(1024), vec(1024), vec(1024), vec(1024), vec(1024),
            pl.BlockSpec((4, r, 2 * r), lambda i, j: (0, 0, 0)),
            sh_spec, st_spec,
            pl.BlockSpec(memory_space=pl.ANY), pl.BlockSpec(memory_space=pl.ANY),
        ],
        out_specs=[
            pl.BlockSpec((r, 1024), lambda i, j: (i * nchunk + j, 0)),
            pl.BlockSpec((None,) + sh_shape, lambda i, j: (l, i, 0, 0)),
            pl.BlockSpec((None,) + st_shape, lambda i, j: (l, i, 0, 0, 0)),
        ],
        out_shape=[
            jax.ShapeDtypeStruct((m, 1024), F32),
            jax.ShapeDtypeStruct(shift_acc.shape, F32),
            jax.ShapeDtypeStruct(st_acc.shape, F32),
        ],
        input_output_aliases={15: 1, 16: 2},
        scratch_shapes=[pltpu.VMEM((nb, 8 + c, RWKV_PROJ), F32),
                        pltpu.VMEM((nb, RWKV_HEADS // 2, 2 * RWKV_HD, 2 * RWKV_HD), F32)],
        compiler_params=_params(("parallel", "arbitrary")),
        name="rwkv7",
    )(p_rwkv, prm["rwkv_mu"], prm["rwkv_w0"], prm["rwkv_w_w2"], prm["rwkv_a0"], prm["rwkv_w_a2"],
      prm["rwkv_w_g2"], prm["rwkv_k_k"], prm["rwkv_k_a"], prm["rwkv_r_k"], prm["rwkv_lnx_g"],
      prm["rwkv_lnx_b"], masks, shift_in, st_in, shift_acc, st_acc)


def _merge_kernel(a_ref, b_ref, c_ref, g_ref, x_ref, wb_ref, wo_ref, o_ref):
    merged = None
    for i, br in enumerate((a_ref, b_ref, c_ref)):
        gate = jax.nn.sigmoid(g_ref[:, i * 1024:(i + 1) * 1024])
        term = gate * jnp.dot(br[...].astype(BF16), wb_ref[i], preferred_element_type=F32)
        merged = term if merged is None else merged + term
    o_ref[...] = x_ref[...] + jnp.dot(merged.astype(BF16), wo_ref[...], preferred_element_type=F32)


def _merge(out_a, out_b, out_c, gates, x, l, wb, wo):
    m, d = x.shape
    tm = min(m, 256)
    row = lambda i: (i, 0)
    return pl.pallas_call(
        _merge_kernel,
        grid=(m // tm,),
        in_specs=[
            pl.BlockSpec((tm, d), row), pl.BlockSpec((tm, d), row), pl.BlockSpec((tm, d), row),
            pl.BlockSpec((tm, GATE_PROJ), row), pl.BlockSpec((tm, d), row),
            pl.BlockSpec((None, 3, d, d), lambda i: (l, 0, 0, 0)),
            pl.BlockSpec((None, d, d), lambda i: (l, 0, 0)),
        ],
        out_specs=pl.BlockSpec((tm, d), row),
        out_shape=jax.ShapeDtypeStruct((m, d), F32),
        compiler_params=_params(("parallel",)),
        name="merge",
    )(out_a, out_b, out_c, gates, x, wb, wo)


def _attn_kernel(q_ref, k_ref, v_ref, o_ref, *, nb, c):
    rows = [slice(s * c, (s + 1) * c) for s in range(nb)]
    sc = [_dot_nt(q_ref[rows[s], :], k_ref[s]) * (X_HEAD_DIM ** -0.5) for s in range(nb)]
    e = [jnp.exp(x - jnp.max(x, axis=-1, keepdims=True)) for x in sc]
    pr = [x * (1.0 / jnp.sum(x, axis=-1, keepdims=True)) for x in e]
    for s in range(nb):
        o_ref[rows[s], :] = _dot(pr[s], v_ref[s])


def _attn(q, mk, mv, l, bsz, t):
    if t > ATTN_ROWS:
        nb, c = 1, ATTN_ROWS
    else:
        nb, c = max(1, min(bsz, SEQ_BLOCK_ROWS // t)), t
    r = nb * c
    ntile = t // c
    hd = X_HEAD_DIM
    if mk.ndim == 3:
        kv_spec = pl.BlockSpec((nb, N_MEM, hd), lambda i, j, h: (i, 0, h))
    else:
        kv_spec = pl.BlockSpec((None, nb, N_MEM, hd), lambda i, j, h: (l, i, 0, h))
    return pl.pallas_call(
        functools.partial(_attn_kernel, nb=nb, c=c),
        grid=(bsz // nb, ntile, X_HEADS),
        in_specs=[pl.BlockSpec((r, hd), lambda i, j, h: (i * ntile + j, h)), kv_spec, kv_spec],
        out_specs=pl.BlockSpec((r, hd), lambda i, j, h: (i * ntile + j, h)),
        out_shape=jax.ShapeDtypeStruct((bsz * t, 1024), F32),
        compiler_params=_params(("parallel", "arbitrary", "arbitrary")),
        name="xattn_core",
    )(q, mk, mv)


def _trunk(x, prm, states, mem_k, mem_v, bsz, t, offset):
    acc_ret = jnp.zeros((DEPTH, bsz, RET_HEADS, RET_DK, RET_DK), F32)
    acc_ssm = jnp.zeros((DEPTH, bsz, SSM_HEADS, SSM_HEADDIM, SSM_STATE), F32)
    acc_conv = jnp.zeros((DEPTH, bsz, CONV_W - 1, CONV_DIM), F32)
    acc_rwkv = jnp.zeros((DEPTH, bsz, RWKV_HEADS, RWKV_HD, RWKV_HD), F32)
    acc_shift = jnp.zeros((DEPTH, bsz, 1, RWKV_PROJ), F32)
    for l in range(DEPTH):
        x = _ffn(x, prm["ffa_norm"], l, prm["ffa_w1"], prm["ffa_w3"], prm["ffa_w2"])
        p_ret = _norm_matmul(x, prm["mix_norm"], l, prm["w_ret"], 0, RET_PROJ)
        p_ssm = _norm_matmul(x, prm["mix_norm"], l, prm["w_ssm"], 0, SSM_PROJ_PAD)
        p_rwkv = _norm_matmul(x, prm["mix_norm"], l, prm["w_rwkv"], 0, RWKV_PROJ)
        p_gate = _norm_matmul(x, prm["mix_norm"], l, prm["w_gate"], 0, GATE_PROJ)
        if states is None:
            s_ret = s_ssm = s_conv = s_rwkv = s_shift = None
        else:
            s_ret, s_ssm, s_conv, s_rwkv, s_shift = states
        out_a, acc_ret = _retention(p_ret, s_ret, l, bsz, t, offset, acc_ret)
        out_b, acc_conv, acc_ssm = _ssd(p_ssm, s_conv, s_ssm, l, bsz, t, prm, acc_conv, acc_ssm)
        out_c, acc_shift, acc_rwkv = _rwkv(p_rwkv, s_shift, s_rwkv, l, bsz, t, prm, acc_shift, acc_rwkv)
        x = _merge(out_a, out_b, out_c, p_gate, x, l, prm["w_branch"], prm["w_out"])
        q = _norm_matmul(x, prm["xattn_norm"], l, prm["xattn_wq"], 0, 1024)
        if isinstance(mem_k, (list, tuple)):
            att = _attn(q, mem_k[l], mem_v[l], l, bsz, t)
        else:
            att = _attn(q, mem_k, mem_v, l, bsz, t)
        x = _matmul_residual(att, l, prm["xattn_wo"], x)
        x = _ffn(x, prm["ffb_norm"], l, prm["ffb_w1"], prm["ffb_w3"], prm["ffb_w2"])
    y = _final_norm(x, prm["final_norm"])
    return y, (acc_ret, acc_ssm, acc_conv, acc_rwkv, acc_shift)


def _prep_params(raw):
    p = {}
    for name in ("ffa_w1", "ffa_w3", "ffa_w2", "ffb_w1", "ffb_w3", "ffb_w2", "w_branch", "w_out",
                 "xattn_wq", "xattn_wk", "xattn_wv", "xattn_wo", "rwkv_w_w2", "rwkv_w_a2", "rwkv_w_g2"):
        p[name] = raw[name].astype(BF16)
    w_in = raw["w_in"]
    c0, c1, c2 = RET_PROJ, RET_PROJ + SSM_PROJ, RET_PROJ + SSM_PROJ + RWKV_PROJ
    p["w_ret"] = w_in[:, :, :c0].astype(BF16)
    p["w_ssm"] = jnp.pad(w_in[:, :, c0:c1], ((0, 0), (0, 0), (0, SSM_PROJ_PAD - SSM_PROJ))).astype(BF16)
    p["w_rwkv"] = w_in[:, :, c1:c2].astype(BF16)
    p["w_gate"] = w_in[:, :, c2:].astype(BF16)
    for name in ("ffa_norm", "mix_norm", "ssm_conv_b", "ssm_norm", "rwkv_mu", "rwkv_w0", "rwkv_a0",
                 "rwkv_k_k", "rwkv_k_a", "rwkv_lnx_g", "rwkv_lnx_b", "xattn_norm", "mem_norm", "ffb_norm"):
        p[name] = raw[name][:, None, :]
    p["rwkv_r_k"] = raw["rwkv_r_k"].reshape(DEPTH, 1, 1024)
    pad = ((0, 0), (0, LANES - SSM_HEADS))
    p["ssm_dt_bias"] = jnp.pad(raw["ssm_dt_bias"], pad)[:, None, :]
    p["ssm_a_log"] = jnp.pad(raw["ssm_a_log"], pad)[:, None, :]
    p["ssm_d"] = jnp.repeat(raw["ssm_d"], SSM_HEADDIM, axis=-1)[:, None, :]
    p["ssm_conv_w"] = raw["ssm_conv_w"]
    p["final_norm"] = raw["final_norm"][None, :]
    return p


def _run(x_prompt, x_sample, mem_prompt, states, cache_mem_k, cache_mem_v, raw):
    prm = _prep_params(raw)
    b, t, d = x_prompt.shape
    db, dt_, _ = x_sample.shape
    n_mem = mem_prompt.shape[1]
    mem2 = mem_prompt.reshape(b * n_mem, d)
    mk = [_norm_matmul(mem2, prm["mem_norm"], l, prm["xattn_wk"], 0, 1024) for l in range(DEPTH)]
    mv = [_norm_matmul(mem2, prm["mem_norm"], l, prm["xattn_wv"], 0, 1024) for l in range(DEPTH)]
    mk3 = [a.reshape(b, n_mem, d) for a in mk]
    mv3 = [a.reshape(b, n_mem, d) for a in mv]
    p_mem_k = jnp.stack(mk).reshape(DEPTH, b, n_mem, X_HEADS, X_HEAD_DIM)
    p_mem_v = jnp.stack(mv).reshape(DEPTH, b, n_mem, X_HEADS, X_HEAD_DIM)
    y_p, st_p = _trunk(x_prompt.reshape(b * t, d), prm, None, mk3, mv3, b, t, 0)

    ck = cache_mem_k.reshape(DEPTH, db, n_mem, d)
    cv = cache_mem_v.reshape(DEPTH, db, n_mem, d)
    y_s, st_s = _trunk(x_sample.reshape(db * dt_, d), prm, states, ck, cv, db, dt_, PAST_LEN)
    return (y_p.reshape(b, t, d), y_s.reshape(db, dt_, d)) + st_p + (p_mem_k, p_mem_v) + st_s


def kernel(x_prompt, x_sample, mem_prompt, state_ret, state_ssm, state_conv, state_rwkv, state_shift, cache_mem_k, cache_mem_v, ffa_norm, ffa_w1, ffa_w3, ffa_w2, mix_norm, w_in, ssm_conv_w, ssm_conv_b, ssm_dt_bias, ssm_a_log, ssm_d, ssm_norm, rwkv_mu, rwkv_w0, rwkv_w_w2, rwkv_a0, rwkv_w_a2, rwkv_w_g2, rwkv_k_k, rwkv_k_a, rwkv_r_k, rwkv_lnx_g, rwkv_lnx_b, w_branch, w_out, xattn_norm, mem_norm, xattn_wq, xattn_wk, xattn_wv, xattn_wo, ffb_norm, ffb_w1, ffb_w3, ffb_w2, final_norm):
    raw = dict(ffa_norm=ffa_norm, ffa_w1=ffa_w1, ffa_w3=ffa_w3, ffa_w2=ffa_w2, mix_norm=mix_norm, w_in=w_in,
               ssm_conv_w=ssm_conv_w, ssm_conv_b=ssm_conv_b, ssm_dt_bias=ssm_dt_bias, ssm_a_log=ssm_a_log,
               ssm_d=ssm_d, ssm_norm=ssm_norm, rwkv_mu=rwkv_mu, rwkv_w0=rwkv_w0, rwkv_w_w2=rwkv_w_w2,
               rwkv_a0=rwkv_a0, rwkv_w_a2=rwkv_w_a2, rwkv_w_g2=rwkv_w_g2, rwkv_k_k=rwkv_k_k,
               rwkv_k_a=rwkv_k_a, rwkv_r_k=rwkv_r_k, rwkv_lnx_g=rwkv_lnx_g, rwkv_lnx_b=rwkv_lnx_b,
               w_branch=w_branch, w_out=w_out, xattn_norm=xattn_norm, mem_norm=mem_norm,
               xattn_wq=xattn_wq, xattn_wk=xattn_wk, xattn_wv=xattn_wv, xattn_wo=xattn_wo,
               ffb_norm=ffb_norm, ffb_w1=ffb_w1, ffb_w3=ffb_w3, ffb_w2=ffb_w2, final_norm=final_norm)
    states = (state_ret, state_ssm, state_conv, state_rwkv, state_shift)
    return _run(x_prompt, x_sample, mem_prompt, states, cache_mem_k, cache_mem_v, raw)
```

```python
import functools
import math

import numpy as np
import jax
import jax.numpy as jnp
from jax import lax
from jax.experimental import pallas as pl
from jax.experimental.pallas import tpu as pltpu

F32 = jnp.float32
BF16 = jnp.bfloat16

D_MODEL = 1024
DEPTH = 4
PAST_LEN = 16384
D_FF = 2816
EPS = 1e-6
RET_HEADS = 8
RET_DK = 128
RET_GN_EPS = 1e-5
ROPE_BASE = 10000.0
SSM_HEADS = 16
SSM_HEADDIM = 64
SSM_GROUPS = 2
SSM_STATE = 128
CONV_W = 4
CONV_DIM = 1536
RWKV_HEADS = 16
RWKV_HD = 64
RWKV_PROJ = 3328
RWKV_GN_EPS = 64e-5
N_MEM = 256
X_HEADS = 4
X_HEAD_DIM = 256
RET_PROJ = 4096
SSM_PROJ = 2576
SSM_PROJ_PAD = 2688
GATE_PROJ = 3072
LANES = 128
VMEM_LIMIT = 56 * 1024 * 1024

RET_CHUNK = 128
SSD_CHUNK = 128
RWKV_CHUNK = 64
ATTN_ROWS = 2048
SEQ_BLOCK_ROWS = 64


def _params(sem):
    return pltpu.CompilerParams(dimension_semantics=sem, vmem_limit_bytes=VMEM_LIMIT)


def _rms(x, g):
    return x * lax.rsqrt(jnp.mean(x * x, axis=-1, keepdims=True) + EPS) * g


def _head_norm(x, eps):
    mu = jnp.mean(x, axis=-1, keepdims=True)
    xc = x - mu
    return xc * lax.rsqrt(jnp.mean(xc * xc, axis=-1, keepdims=True) + eps)


def _silu(x):
    return x * jax.nn.sigmoid(x)


def _dot(a, b):
    return jnp.dot(a.astype(BF16), b.astype(BF16), preferred_element_type=F32)


def _dot_nt(a, b):
    return lax.dot_general(a.astype(BF16), b.astype(BF16), (((1,), (1,)), ((), ())),
                           preferred_element_type=F32)


def _dot_tn(a, b):
    return lax.dot_general(a.astype(BF16), b.astype(BF16), (((0,), (0,)), ((), ())),
                           preferred_element_type=F32)


def _dot_f32(a, b):
    return jnp.dot(a, b, precision=lax.Precision.HIGHEST, preferred_element_type=F32)


def _seq_blocking(bsz, t, chunk_max, block_rows):
    if t > chunk_max:
        assert t % chunk_max == 0
        return 1, chunk_max
    nb = max(1, min(bsz, block_rows // t))
    assert bsz % nb == 0
    return nb, t


def _seq_masks(nb, c):
    r = nb * c
    i = np.arange(r)
    same = (i[:, None] // c) == (i[None, :] // c)
    incl = same & (i[:, None] >= i[None, :])
    strict = same & (i[:, None] > i[None, :])
    return same, incl, strict


def _nm_kernel(x_ref, g_ref, w_ref, o_ref, h_ref):
    @pl.when(pl.program_id(1) == 0)
    def _():
        h_ref[...] = _rms(x_ref[...], g_ref[...]).astype(BF16)

    o_ref[...] = jnp.dot(h_ref[...], w_ref[...], preferred_element_type=F32)


def _norm_matmul(x, g, l, w, col0, ncols):
    m, d = x.shape
    tn = ncols
    tm = min(m, 512)
    assert col0 % tn == 0 and m % tm == 0
    cb = col0 // tn
    return pl.pallas_call(
        _nm_kernel,
        grid=(m // tm, ncols // tn),
        in_specs=[
            pl.BlockSpec((tm, d), lambda i, j: (i, 0)),
            pl.BlockSpec((None, 1, d), lambda i, j: (l, 0, 0)),
            pl.BlockSpec((None, d, tn), lambda i, j: (l, 0, cb + j)),
        ],
        out_specs=pl.BlockSpec((tm, tn), lambda i, j: (i, j)),
        out_shape=jax.ShapeDtypeStruct((m, ncols), F32),
        scratch_shapes=[pltpu.VMEM((tm, d), BF16)],
        compiler_params=_params(("parallel", "arbitrary")),
        name="norm_matmul",
    )(x, g, w)


def _mr_kernel(a_ref, w_ref, x_ref, o_ref):
    o_ref[...] = x_ref[...] + jnp.dot(a_ref[...].astype(BF16), w_ref[...],
                                       preferred_element_type=F32)


def _matmul_residual(a, l, w, x):
    m, k = a.shape
    n = w.shape[-1]
    tm = min(m, 512)
    return pl.pallas_call(
        _mr_kernel,
        grid=(m // tm,),
        in_specs=[
            pl.BlockSpec((tm, k), lambda i: (i, 0)),
            pl.BlockSpec((None, k, n), lambda i: (l, 0, 0)),
            pl.BlockSpec((tm, n), lambda i: (i, 0)),
        ],
        out_specs=pl.BlockSpec((tm, n), lambda i: (i, 0)),
        out_shape=jax.ShapeDtypeStruct((m, n), F32),
        compiler_params=_params(("parallel",)),
        name="matmul_residual",
    )(a, w, x)


def _ffn_kernel(x_ref, g_ref, w1_ref, w3_ref, w2_ref, o_ref):
    x = x_ref[...]
    h = _rms(x, g_ref[...]).astype(BF16)
    a = jnp.dot(h, w1_ref[...], preferred_element_type=F32)
    b = jnp.dot(h, w3_ref[...], preferred_element_type=F32)
    u = (_silu(a) * b).astype(BF16)
    o_ref[...] = x + 0.5 * jnp.dot(u, w2_ref[...], preferred_element_type=F32)


def _ffn(x, g, l, w1, w3, w2):
    m, d = x.shape
    f = w1.shape[-1]
    tm = min(m, 512)
    once = pl.Buffered(1)
    return pl.pallas_call(
        _ffn_kernel,
        grid=(m // tm,),
        in_specs=[
            pl.BlockSpec((tm, d), lambda i: (i, 0)),
            pl.BlockSpec((None, 1, d), lambda i: (l, 0, 0)),
            pl.BlockSpec((None, d, f), lambda i: (l, 0, 0), pipeline_mode=once),
            pl.BlockSpec((None, d, f), lambda i: (l, 0, 0), pipeline_mode=once),
            pl.BlockSpec((None, f, d), lambda i: (l, 0, 0), pipeline_mode=once),
        ],
        out_specs=pl.BlockSpec((tm, d), lambda i: (i, 0)),
        out_shape=jax.ShapeDtypeStruct((m, d), F32),
        compiler_params=_params(("parallel",)),
        name="ffn",
    )(x, g, w1, w3, w2)


def _norm_kernel(x_ref, g_ref, o_ref):
    o_ref[...] = _rms(x_ref[...], g_ref[...])


def _final_norm(x, g):
    m, d = x.shape
    tm = min(m, 1024)
    return pl.pallas_call(
        _norm_kernel,
        grid=(m // tm,),
        in_specs=[pl.BlockSpec((tm, d), lambda i: (i, 0)), pl.BlockSpec((1, d), lambda i: (0, 0))],
        out_specs=pl.BlockSpec((tm, d), lambda i: (i, 0)),
        out_shape=jax.ShapeDtypeStruct((m, d), F32),
        compiler_params=_params(("parallel",)),
        name="final_norm",
    )(x, g)


def _ret_kernel(p_ref, cos_ref, sin_ref, dmat_ref, qdec_ref, kdec_ref, st_in_ref, acc_ref,
                o_ref, st_ref, *, nb, c, sdec):
    del acc_ref
    @pl.when(pl.program_id(1) == 0)
    def _():
        st_ref[...] = st_in_ref[...]

    cosf = cos_ref[...]
    sinf = sin_ref[...]
    hd = RET_DK
    heads = range(RET_HEADS)
    rows = [slice(s * c, (s + 1) * c) for s in range(nb)]
    qr, kr, v = [], [], []
    for h in heads:
        q = p_ref[:, h * hd:(h + 1) * hd]
        k = p_ref[:, 1024 + h * hd:1024 + (h + 1) * hd]
        qr.append(((q * cosf + pltpu.roll(q, hd // 2, 1) * sinf) * (RET_DK ** -0.5)).astype(BF16))
        kr.append(k * cosf + pltpu.roll(k, hd // 2, 1) * sinf)
        v.append(p_ref[:, 2048 + h * hd:2048 + (h + 1) * hd].astype(BF16))
    scores = [_dot_nt(qr[h], kr[h]) * dmat_ref[h] for h in heads]
    inter = [[_dot(qr[h][rw], st_ref[s, h]) for s, rw in enumerate(rows)] for h in heads]
    o = [_dot(scores[h], v[h]) for h in heads]
    for h in heads:
        kd = kr[h] * kdec_ref[h]
        qdec = qdec_ref[h]
        for s, rw in enumerate(rows):
            o_s = o[h][rw] + inter[h][s] * qdec[rw]
            g = p_ref[rw, 3072 + h * hd:3072 + (h + 1) * hd]
            o_ref[rw, h * hd:(h + 1) * hd] = _silu(g) * _head_norm(o_s, RET_GN_EPS)
            st_ref[s, h] = st_ref[s, h] * sdec[h] + _dot_tn(kd[rw], v[h][rw])


def _retention(p_ret, st_all, l, bsz, t, offset, acc):
    nb, c = _seq_blocking(bsz, t, RET_CHUNK, SEQ_BLOCK_ROWS)
    r = nb * c
    nchunk = t // c
    heads = np.arange(RET_HEADS, dtype=np.float64)
    log_g = np.log1p(-np.exp2(-5.0 - heads))
    idx = np.arange(r) % c
    _, incl, _ = _seq_masks(nb, c)
    diff = (idx[:, None] - idx[None, :]).astype(np.float64)
    dmat = np.where(incl[None], np.exp(np.maximum(diff, 0.0)[None] * log_g[:, None, None]), 0.0)
    qdec = np.exp((idx + 1.0)[None, :] * log_g[:, None])
    kdec = np.exp((c - 1.0 - idx)[None, :] * log_g[:, None])
    qdec = np.broadcast_to(qdec[:, :, None], (RET_HEADS, r, RET_DK))
    kdec = np.broadcast_to(kdec[:, :, None], (RET_HEADS, r, RET_DK))
    sdec = tuple(float(x) for x in np.exp(c * log_g))

    half = RET_DK // 2
    freqs = ROPE_BASE ** (-jnp.arange(half, dtype=F32) / half)
    pos = jnp.float32(offset) + jnp.arange(t, dtype=F32)
    ang = pos[:, None] * freqs[None, :]
    cos = jnp.cos(ang)
    sin = jnp.sin(ang)
    cosf = jnp.tile(jnp.concatenate([cos, cos], axis=-1), (nb, 1))
    sinf = jnp.tile(jnp.concatenate([-sin, sin], axis=-1), (nb, 1))

    st_shape = (nb, RET_HEADS, RET_DK, RET_DK)
    if st_all is None:
        st_in = jnp.zeros((bsz,) + st_shape[1:], F32)
        st_spec = pl.BlockSpec(st_shape, lambda i, j: (i, 0, 0, 0))
    else:
        st_in = st_all
        st_spec = pl.BlockSpec((None,) + st_shape, lambda i, j: (l, i, 0, 0, 0))
    m = bsz * t
    return pl.pallas_call(
        functools.partial(_ret_kernel, nb=nb, c=c, sdec=sdec),
        grid=(bsz // nb, nchunk),
        in_specs=[
            pl.BlockSpec((r, RET_PROJ), lambda i, j: (i * nchunk + j, 0)),
            pl.BlockSpec((r, RET_DK), lambda i, j: (j, 0)),
            pl.BlockSpec((r, RET_DK), lambda i, j: (j, 0)),
            pl.BlockSpec((RET_HEADS, r, r), lambda i, j: (0, 0, 0)),
            pl.BlockSpec((RET_HEADS, r, RET_DK), lambda i, j: (0, 0, 0)),
            pl.BlockSpec((RET_HEADS, r, RET_DK), lambda i, j: (0, 0, 0)),
            st_spec,
            pl.BlockSpec(memory_space=pl.ANY),
        ],
        out_specs=[
            pl.BlockSpec((r, 1024), lambda i, j: (i * nchunk + j, 0)),
            pl.BlockSpec((None,) + st_shape, lambda i, j: (l, i, 0, 0, 0)),
        ],
        out_shape=[
            jax.ShapeDtypeStruct((m, 1024), F32),
            jax.ShapeDtypeStruct(acc.shape, F32),
        ],
        input_output_aliases={7: 1},
        compiler_params=_params(("parallel", "arbitrary")),
        name="retention",
    )(p_ret, cosf, sinf, jnp.asarray(dmat, F32), jnp.asarray(qdec, F32), jnp.asarray(kdec, F32), st_in, acc)


def _ssd_kernel(p_ref, cw_ref, cb_ref, dtb_ref, alog_ref, dfull_ref, norm_ref, tri_ref, mask2_ref,
                e64_ref, er_ref, conv_in_ref, st_in_ref, conv_acc_ref, st_acc_ref,
                o_ref, conv_out_ref, st_ref, ext_ref, stt_ref, *, nb, c):
    del conv_acc_ref, st_acc_ref
    r = nb * c
    hp = SSM_HEADDIM
    npair = SSM_HEADS // 2
    pairs = range(npair)
    seqs = range(nb)
    rows = [slice(s * c, (s + 1) * c) for s in seqs]
    cols = [slice(q * 2 * hp, (q + 1) * 2 * hp) for q in pairs]
    left = lax.broadcasted_iota(jnp.int32, (1, 2 * hp), 1) < hp

    @pl.when(pl.program_id(1) == 0)
    def _():
        for s in seqs:
            ext_ref[s, 5:8, :] = conv_in_ref[s]
            for q in pairs:
                both = jnp.concatenate([st_in_ref[s, 2 * q], st_in_ref[s, 2 * q + 1]], axis=0)
                stt_ref[s, q] = both.T

    for s in range(nb):
        ext_ref[s, 8:8 + c, :] = p_ref[s * c:(s + 1) * c, 1024:1024 + CONV_DIM]
    pieces = []
    for s in range(nb):
        acc = cb_ref[...] + ext_ref[s, 5:5 + c, :] * cw_ref[0:1, :]
        for j in range(1, CONV_W):
            acc = acc + ext_ref[s, 5 + j:5 + j + c, :] * cw_ref[j:j + 1, :]
        pieces.append(acc)
    for s in range(nb):
        conv_out_ref[s] = ext_ref[s, c + 5:c + 8, :]
        ext_ref[s, 0:8, :] = ext_ref[s, c:c + 8, :]
    xbc = _silu(pieces[0] if nb == 1 else jnp.concatenate(pieces, axis=0))
    xs = xbc[:, :1024]
    bm = xbc[:, 1024:1024 + SSM_GROUPS * SSM_STATE]
    cm = xbc[:, 1024 + SSM_GROUPS * SSM_STATE:]

    dt = jax.nn.softplus(p_ref[:, 1024 + CONV_DIM:] + dtb_ref[...])
    a = -jnp.exp(alog_ref[...])
    tri = tri_ref[...]
    cum = _dot_f32(tri, dt * a)
    cum_t = cum.T

    def expand(v, e_ref):
        hi = v.astype(BF16)
        r1 = v - hi.astype(F32)
        mid = r1.astype(BF16)
        lo = (r1 - mid.astype(F32)).astype(BF16)
        return jnp.dot(jnp.concatenate([hi, mid, lo], axis=1), e_ref[...], preferred_element_type=F32)

    def bd(v):
        vb = v.astype(BF16)
        zero = jnp.zeros_like(vb)
        return jnp.concatenate([jnp.where(left, vb, zero), jnp.where(left, zero, vb)], axis=0)

    def cat_rows(pieces):
        return pieces[0] if nb == 1 else jnp.concatenate(pieces, axis=0)

    cum_full = expand(cum, e64_ref)
    cum_wide = cum_full if r == hp else expand(cum, er_ref)
    xdt = xs * expand(dt, e64_ref)
    last = [cum_full[s * c + c - 1:s * c + c, :] for s in seqs]
    xw = cat_rows([xdt[rw] * jnp.exp(last[s] - cum_full[rw]) for s, rw in enumerate(rows)])
    eci = jnp.exp(cum_full)
    mask2 = mask2_ref[...] > 0.5

    hg = SSM_HEADS // SSM_GROUPS
    bm_g = [bm[:, g * SSM_STATE:(g + 1) * SSM_STATE] for g in range(SSM_GROUPS)]
    cm_g = [cm[:, g * SSM_STATE:(g + 1) * SSM_STATE] for g in range(SSM_GROUPS)]
    cb2 = []
    for g in range(SSM_GROUPS):
        cb = _dot_nt(cm_g[g], bm_g[g])
        cb2.append(jnp.concatenate([cb, cb], axis=1))
    grp = [2 * q // hg for q in pairs]
    inter = [cat_rows([_dot(cm_g[grp[q]][rw], stt_ref[s, q]) for s, rw in enumerate(rows)]) for q in pairs]
    sc = []
    for q in pairs:
        row2 = jnp.concatenate([cum_t[2 * q:2 * q + 1, :], cum_t[2 * q + 1:2 * q + 2, :]], axis=1)
        seg = cum_wide[:, q * 2 * r:(q + 1) * 2 * r] - row2
        sc.append(cb2[grp[q]] * jnp.where(mask2, jnp.exp(seg), 0.0))
    y_p = [jnp.dot(sc[q].astype(BF16), bd(xdt[:, cols[q]]), preferred_element_type=F32)
           + inter[q] * eci[:, cols[q]] for q in pairs]
    for q in pairs:
        for s, rw in enumerate(rows):
            upd = _dot_tn(bm_g[grp[q]][rw], xw[rw, cols[q]])
            stt_ref[s, q] = stt_ref[s, q] * jnp.exp(last[s][:, cols[q]]) + upd
    y = jnp.concatenate(y_p, axis=1) + dfull_ref[...] * xs
    z = p_ref[:, :1024]
    o_ref[...] = _rms(y * _silu(z), norm_ref[...])

    @pl.when(pl.program_id(1) == pl.num_programs(1) - 1)
    def _():
        for s in seqs:
            for q in pairs:
                both = stt_ref[s, q].T
                st_ref[s, 2 * q] = both[:hp]
                st_ref[s, 2 * q + 1] = both[hp:]


def _ssd(p_ssm, conv_all, st_all, l, bsz, t, prm, conv_acc, st_acc):
    nb, c = _seq_blocking(bsz, t, SSD_CHUNK, SEQ_BLOCK_ROWS)
    r = nb * c
    nchunk = t // c
    _, incl, _ = _seq_masks(nb, c)
    tri = jnp.asarray(incl.astype(np.float32))
    mask2 = jnp.asarray(np.tile(incl.astype(np.float32), (1, 2)))

    def expander(width):
        e = np.zeros((LANES, SSM_HEADS * width), np.float32)
        for h in range(SSM_HEADS):
            e[h, h * width:(h + 1) * width] = 1.0
        return jnp.asarray(np.concatenate([e, e, e], axis=0), BF16)

    e64 = expander(SSM_HEADDIM)
    e_r = expander(r)
    st_shape = (nb, SSM_HEADS, SSM_HEADDIM, SSM_STATE)
    cv_shape = (nb, CONV_W - 1, CONV_DIM)
    if st_all is None:
        st_in = jnp.zeros((bsz,) + st_shape[1:], F32)
        conv_in = jnp.zeros((bsz,) + cv_shape[1:], F32)
        st_spec = pl.BlockSpec(st_shape, lambda i, j: (i, 0, 0, 0))
        cv_spec = pl.BlockSpec(cv_shape, lambda i, j: (i, 0, 0))
    else:
        st_in, conv_in = st_all, conv_all
        st_spec = pl.BlockSpec((None,) + st_shape, lambda i, j: (l, i, 0, 0, 0))
        cv_spec = pl.BlockSpec((None,) + cv_shape, lambda i, j: (l, i, 0, 0))
    m = bsz * t

    def vec(n):
        return pl.BlockSpec((None, 1, n), lambda i, j: (l, 0, 0))

    return pl.pallas_call(
        functools.partial(_ssd_kernel, nb=nb, c=c),
        grid=(bsz // nb, nchunk),
        in_specs=[
            pl.BlockSpec((r, SSM_PROJ_PAD), lambda i, j: (i * nchunk + j, 0)),
            pl.BlockSpec((None, CONV_W, CONV_DIM), lambda i, j: (l, 0, 0)),
            vec(CONV_DIM), vec(LANES), vec(LANES), vec(1024), vec(1024),
            pl.BlockSpec((r, r), lambda i, j: (0, 0)),
            pl.BlockSpec((r, 2 * r), lambda i, j: (0, 0)),
            pl.BlockSpec(e64.shape, lambda i, j: (0, 0)),
            pl.BlockSpec(e_r.shape, lambda i, j: (0, 0)),
            cv_spec, st_spec,
            pl.BlockSpec(memory_space=pl.ANY), pl.BlockSpec(memory_space=pl.ANY),
        ],
        out_specs=[
            pl.BlockSpec((r, 1024), lambda i, j: (i * nchunk + j, 0)),
            pl.BlockSpec((None,) + cv_shape, lambda i, j: (l, i, 0, 0)),
            pl.BlockSpec((None,) + st_shape, lambda i, j: (l, i, 0, 0, 0)),
        ],
        out_shape=[
            jax.ShapeDtypeStruct((m, 1024), F32),
            jax.ShapeDtypeStruct(conv_acc.shape, F32),
            jax.ShapeDtypeStruct(st_acc.shape, F32),
        ],
        input_output_aliases={13: 1, 14: 2},
        scratch_shapes=[pltpu.VMEM((nb, 8 + c, CONV_DIM), F32),
                        pltpu.VMEM((nb, SSM_HEADS // 2, SSM_STATE, 2 * SSM_HEADDIM), F32)],
        compiler_params=_params(("parallel", "arbitrary")),
        name="ssd",
    )(p_ssm, prm["ssm_conv_w"], prm["ssm_conv_b"], prm["ssm_dt_bias"], prm["ssm_a_log"],
      prm["ssm_d"], prm["ssm_norm"], tri, mask2, e64, e_r, conv_in, st_in, conv_acc, st_acc)


def _rwkv_kernel(p_ref, mu_ref, w0_ref, ww2_ref, a0_ref, wa2_ref, wg2_ref, kk_ref, ka_ref,
                 rk_ref, lng_ref, lnb_ref, masks_ref, shift_in_ref, st_in_ref, shift_acc_ref, st_acc_ref,
                 o_ref, shift_out_ref, st_ref, ext_ref, bd_ref, *, nb, c):
    del shift_acc_ref, st_acc_ref
    r = nb * c
    hd = RWKV_HD
    npair = RWKV_HEADS // 2
    pairs = range(npair)
    seqs = range(nb)
    rows = [slice(s * c, (s + 1) * c) for s in seqs]
    lane = lax.broadcasted_iota(jnp.int32, (1, 2 * hd), 1)
    left = lane < hd
    sub = lax.broadcasted_iota(jnp.int32, (2 * hd, 1), 0)
    bd_mask = (sub < hd) == left

    @pl.when(pl.program_id(1) == 0)
    def _():
        zero = jnp.zeros((hd, hd), F32)
        for s in seqs:
            ext_ref[s, 7:8, :] = shift_in_ref[s]
            for p in pairs:
                top = jnp.concatenate([st_in_ref[s, 2 * p], zero], axis=1)
                bot = jnp.concatenate([zero, st_in_ref[s, 2 * p + 1]], axis=1)
                bd_ref[s, p] = jnp.concatenate([top, bot], axis=0)

    x = p_ref[...]
    for s in range(nb):
        ext_ref[s, 8:8 + c, :] = x[s * c:(s + 1) * c]
    prev = [ext_ref[s, 7:7 + c, :] for s in range(nb)]
    prev = prev[0] if nb == 1 else jnp.concatenate(prev, axis=0)
    for s in range(nb):
        ext_ref[s, 0:8, :] = ext_ref[s, c:c + 8, :]
        shift_out_ref[s] = ext_ref[s, 7:8, :]

    mixed = x + (prev - x) * mu_ref[...]
    rr = mixed[:, 0:1024]
    kc = mixed[:, 1024:2048]
    vc = mixed[:, 2048:3072]
    wd = mixed[:, 3072:3136]
    ad = mixed[:, 3136:3200]
    gd = mixed[:, 3200:3328]
    w_log = -jax.nn.softplus(-(w0_ref[...] + _dot(jnp.tanh(wd), ww2_ref[...]))) - 0.5
    lw = -jnp.exp(w_log)
    a_lr = jax.nn.sigmoid(a0_ref[...] + _dot(ad, wa2_ref[...]))
    g_rw = _dot(jax.nn.sigmoid(gd), wg2_ref[...])
    kk = kc * kk_ref[...]
    kmod = kc * (1.0 + (a_lr - 1.0) * ka_ref[...])
    rkk = rr * kmod * rk_ref[...]

    m_incl = masks_ref[0]
    m_strict = masks_ref[1]
    m_blk = masks_ref[2]
    eye = masks_ref[3]
    cl = _dot_f32(m_incl[:, :r], lw)

    def half_sum(a):
        sl = jnp.sum(jnp.where(left, a, 0.0), axis=-1, keepdims=True)
        sr = jnp.sum(jnp.where(left, 0.0, a), axis=-1, keepdims=True)
        return jnp.where(left, sl, sr)

    def bd(a):
        ab = a.astype(BF16)
        zero = jnp.zeros_like(ab)
        return jnp.concatenate([jnp.where(left, ab, zero), jnp.where(left, zero, ab)], axis=0)

    def lp_dot(a, b):
        return jnp.dot(a.astype(BF16), bd(b), preferred_element_type=F32)

    def cat_rows(pieces):
        return pieces[0] if nb == 1 else jnp.concatenate(pieces, axis=0)

    cols = [slice(p * 2 * hd, (p + 1) * 2 * hd) for p in pairs]
    v_p = [vc[:, cs] for cs in cols]
    kt, rt, kh, bh, e_last = [], [], [], [], []
    for cs in cols:
        kk_p = kk[:, cs]
        kk_p = kk_p * lax.rsqrt(jnp.maximum(half_sum(kk_p * kk_p), 1e-24))
        cl_p = cl[:, cs]
        e_incl = jnp.exp(cl_p)
        e_inv = jnp.exp(-cl_p)
        rt.append(rr[:, cs] * e_incl)
        kt.append(kk_p * jnp.exp(cl_p - lw[:, cs]))
        kh.append(kmod[:, cs] * e_inv)
        bh.append(kk_p * a_lr[:, cs] * e_inv)
        e_last.append([e_incl[s * c + c - 1:s * c + c, :] for s in seqs])
    lhs = [jnp.concatenate([kt[p], rt[p]], axis=0).astype(BF16) for p in pairs]
    gk = [_dot_nt(lhs[p], bd(kh[p])) for p in pairs]
    gb = [_dot_nt(lhs[p], bd(bh[p])) for p in pairs]
    both = [[_dot_nt(jnp.concatenate([kt[p][rw], rt[p][rw]], axis=0) if nb > 1 else lhs[p],
                     bd_ref[s, p]) for s, rw in enumerate(rows)] for p in pairs]
    a_kk = [g[:r] * m_strict for g in gk]
    a_rk = [g[r:] * m_incl for g in gk]
    a_kb = [g[:r] * m_strict for g in gb]
    a_rb = [g[r:] * m_incl for g in gb]
    akv = [lp_dot(a_kk[p], v_p[p]) for p in pairs]

    n1 = [a * m_blk for a in a_kb]
    n2 = [lp_dot(n, n) for n in n1]
    p1 = [lp_dot(eye - n1[p], eye + n2[p]) for p in pairs]
    n4 = [lp_dot(n, n) for n in n2]
    p2 = [lp_dot(p1[p], eye + n4[p]) for p in pairs]
    n8 = [lp_dot(n, n) for n in n4]
    dinv = [lp_dot(p2[p], eye + n8[p]) for p in pairs]
    xm = [lp_dot(dinv[p], a_kb[p] - n1[p]) for p in pairs]
    x2 = [lp_dot(x_, x_) for x_ in xm]
    t1 = [lp_dot(eye - xm[p], eye + x2[p]) for p in pairs]
    tinv = [lp_dot(t1[p], dinv[p]) for p in pairs]

    ks = [cat_rows([b[:c] for b in both[p]]) for p in pairs]
    rs = [cat_rows([b[c:] for b in both[p]]) for p in pairs]
    u = [lp_dot(tinv[p], ks[p] + akv[p]) for p in pairs]
    y = [rs[p] + lp_dot(a_rk[p], v_p[p]) - lp_dot(a_rb[p], u[p]) for p in pairs]
    for p in pairs:
        for s, rw in enumerate(rows):
            upd = _dot_tn(jnp.concatenate([v_p[p][rw], -u[p][rw]], axis=0),
                          jnp.concatenate([kh[p][rw], bh[p][rw]], axis=0))
            bd_ref[s, p] = (bd_ref[s, p] + jnp.where(bd_mask, upd, 0.0)) * e_last[p][s]
    for p, cs in enumerate(cols):
        bonus = half_sum(rkk[:, cs]) * v_p[p]
        yc = y[p] - half_sum(y[p]) * (1.0 / hd)
        yn = yc * lax.rsqrt(half_sum(yc * yc) * (1.0 / hd) + RWKV_GN_EPS)
        o_ref[:, cs] = (yn * lng_ref[:, cs] + lnb_ref[:, cs] + bonus) * g_rw[:, cs]

    @pl.when(pl.program_id(1) == pl.num_programs(1) - 1)
    def _():
        for s in seqs:
            for p in pairs:
                blk = bd_ref[s, p]
                st_ref[s, 2 * p] = blk[:hd, :hd]
                st_ref[s, 2 * p + 1] = blk[hd:, hd:]


def _rwkv(p_rwkv, shift_all, st_all, l, bsz, t, prm, shift_acc, st_acc):
    nb, c = _seq_blocking(bsz, t, RWKV_CHUNK, SEQ_BLOCK_ROWS)
    r = nb * c
    nchunk = t // c
    assert 2 * r == LANES, "the lane-paired layout holds two (r, r) matrices side by side"
    _, incl, strict = _seq_masks(nb, c)
    i = np.arange(r)
    blk = (i[:, None] // 16) == (i[None, :] // 16)
    masks = np.stack([incl, strict, blk, np.eye(r, dtype=bool)]).astype(np.float32)
    masks = jnp.asarray(np.tile(masks, (1, 1, 2)))
    st_shape = (nb, RWKV_HEADS, RWKV_HD, RWKV_HD)
    sh_shape = (nb, 1, RWKV_PROJ)
    if st_all is None:
        st_in = jnp.zeros((bsz,) + st_shape[1:], F32)
        shift_in = jnp.zeros((bsz,) + sh_shape[1:], F32)
        st_spec = pl.BlockSpec(st_shape, lambda i, j: (i, 0, 0, 0))
        sh_spec = pl.BlockSpec(sh_shape, lambda i, j: (i, 0, 0))
    else:
        st_in, shift_in = st_all, shift_all
        st_spec = pl.BlockSpec((None,) + st_shape, lambda i, j: (l, i, 0, 0, 0))
        sh_spec = pl.BlockSpec((None,) + sh_shape, lambda i, j: (l, i, 0, 0))
    m = bsz * t

    def vec(n):
        return pl.BlockSpec((None, 1, n), lambda i, j: (l, 0, 0))

    def mat(k):
        return pl.BlockSpec((None, k, 1024), lambda i, j: (l, 0, 0))

    return pl.pallas_call(
        functools.partial(_rwkv_kernel, nb=nb, c=c),
        grid=(bsz // nb, nchunk),
        in_specs=[
            pl.BlockSpec((r, RWKV_PROJ), lambda i, j: (i * nchunk + j, 0)),
            vec(RWKV_PROJ), vec(1024), mat(64), vec(1024), mat(64), mat(128),
            vec(1024), vec(1024), vec(1024), vec(1024), vec(1024),
            pl.BlockSpec((4, r, 2 * r), lambda i, j: (0, 0, 0)),
            sh_spec, st_spec,
            pl.BlockSpec(memory_space=pl.ANY), pl.BlockSpec(memory_space=pl.ANY),
        ],
        out_specs=[
            pl.BlockSpec((r, 1024), lambda i, j: (i * nchunk + j, 0)),
            pl.BlockSpec((None,) + sh_shape, lambda i, j: (l, i, 0, 0)),
            pl.BlockSpec((None,) + st_shape, lambda i, j: (l, i, 0, 0, 0)),
        ],
        out_shape=[
            jax.ShapeDtypeStruct((m, 1024), F32),
            jax.ShapeDtypeStruct(shift_acc.shape, F32),
            jax.ShapeDtypeStruct(st_acc.shape, F32),
        ],
        input_output_aliases={15: 1, 16: 2},
        scratch_shapes=[pltpu.VMEM((nb, 8 + c, RWKV_PROJ), F32),
                        pltpu.VMEM((nb, RWKV_HEADS // 2, 2 * RWKV_HD, 2 * RWKV_HD), F32)],
        compiler_params=_params(("parallel", "arbitrary")),
        name="rwkv7",
    )(p_rwkv, prm["rwkv_mu"], prm["rwkv_w0"], prm["rwkv_w_w2"], prm["rwkv_a0"], prm["rwkv_w_a2"],
      prm["rwkv_w_g2"], prm["rwkv_k_k"], prm["rwkv_k_a"], prm["rwkv_r_k"], prm["rwkv_lnx_g"],
      prm["rwkv_lnx_b"], masks, shift_in, st_in, shift_acc, st_acc)


def _merge_kernel(a_ref, b_ref, c_ref, g_ref, x_ref, wb_ref, wo_ref, o_ref):
    merged = None
    for i, br in enumerate((a_ref, b_ref, c_ref)):
        gate = jax.nn.sigmoid(g_ref[:, i * 1024:(i + 1) * 1024])
        term = gate * jnp.dot(br[...].astype(BF16), wb_ref[i], preferred_element_type=F32)
        merged = term if merged is None else merged + term
    o_ref[...] = x_ref[...] + jnp.dot(merged.astype(BF16), wo_ref[...], preferred_element_type=F32)


def _merge(out_a, out_b, out_c, gates, x, l, wb, wo):
    m, d = x.shape
    tm = min(m, 256)
    row = lambda i: (i, 0)
    return pl.pallas_call(
        _merge_kernel,
        grid=(m // tm,),
        in_specs=[
            pl.BlockSpec((tm, d), row), pl.BlockSpec((tm, d), row), pl.BlockSpec((tm, d), row),
            pl.BlockSpec((tm, GATE_PROJ), row), pl.BlockSpec((tm, d), row),
            pl.BlockSpec((None, 3, d, d), lambda i: (l, 0, 0, 0)),
            pl.BlockSpec((None, d, d), lambda i: (l, 0, 0)),
        ],
        out_specs=pl.BlockSpec((tm, d), row),
        out_shape=jax.ShapeDtypeStruct((m, d), F32),
        compiler_params=_params(("parallel",)),
        name="merge",
    )(out_a, out_b, out_c, gates, x, wb, wo)


def _attn_kernel(q_ref, k_ref, v_ref, o_ref, *, nb, c):
    rows = [slice(s * c, (s + 1) * c) for s in range(nb)]
    sc = [_dot_nt(q_ref[rows[s], :], k_ref[s]) * (X_HEAD_DIM ** -0.5) for s in range(nb)]
    e = [jnp.exp(x - jnp.max(x, axis=-1, keepdims=True)) for x in sc]
    pr = [x * (1.0 / jnp.sum(x, axis=-1, keepdims=True)) for x in e]
    for s in range(nb):
        o_ref[rows[s], :] = _dot(pr[s], v_ref[s])


def _attn(q, mk, mv, l, bsz, t):
    if t > ATTN_ROWS:
        nb, c = 1, ATTN_ROWS
    else:
        nb, c = max(1, min(bsz, SEQ_BLOCK_ROWS // t)), t
    r = nb * c
    ntile = t // c
    hd = X_HEAD_DIM
    if mk.ndim == 3:
        kv_spec = pl.BlockSpec((nb, N_MEM, hd), lambda i, j, h: (i, 0, h))
    else:
        kv_spec = pl.BlockSpec((None, nb, N_MEM, hd), lambda i, j, h: (l, i, 0, h))
    return pl.pallas_call(
        functools.partial(_attn_kernel, nb=nb, c=c),
        grid=(bsz // nb, ntile, X_HEADS),
        in_specs=[pl.BlockSpec((r, hd), lambda i, j, h: (i * ntile + j, h)), kv_spec, kv_spec],
        out_specs=pl.BlockSpec((r, hd), lambda i, j, h: (i * ntile + j, h)),
        out_shape=jax.ShapeDtypeStruct((bsz * t, 1024), F32),
        compiler_params=_params(("parallel", "arbitrary", "arbitrary")),
        name="xattn_core",
    )(q, mk, mv)


def _trunk(x, prm, states, mem_k, mem_v, bsz, t, offset):
    acc_ret = jnp.zeros((DEPTH, bsz, RET_HEADS, RET_DK, RET_DK), F32)
    acc_ssm = jnp.zeros((DEPTH, bsz, SSM_HEADS, SSM_HEADDIM, SSM_STATE), F32)
    acc_conv = jnp.zeros((DEPTH, bsz, CONV_W - 1, CONV_DIM), F32)
    acc_rwkv = jnp.zeros((DEPTH, bsz, RWKV_HEADS, RWKV_HD, RWKV_HD), F32)
    acc_shift = jnp.zeros((DEPTH, bsz, 1, RWKV_PROJ), F32)
    for l in range(DEPTH):
        x = _ffn(x, prm["ffa_norm"], l, prm["ffa_w1"], prm["ffa_w3"], prm["ffa_w2"])
        p_ret = _norm_matmul(x, prm["mix_norm"], l, prm["w_ret"], 0, RET_PROJ)
        p_ssm = _norm_matmul(x, prm["mix_norm"], l, prm["w_ssm"], 0, SSM_PROJ_PAD)
        p_rwkv = _norm_matmul(x, prm["mix_norm"], l, prm["w_rwkv"], 0, RWKV_PROJ)
        p_gate = _norm_matmul(x, prm["mix_norm"], l, prm["w_gate"], 0, GATE_PROJ)
        if states is None:
            s_ret = s_ssm = s_conv = s_rwkv = s_shift = None
        else:
            s_ret, s_ssm, s_conv, s_rwkv, s_shift = states
        out_a, acc_ret = _retention(p_ret, s_ret, l, bsz, t, offset, acc_ret)
        out_b, acc_conv, acc_ssm = _ssd(p_ssm, s_conv, s_ssm, l, bsz, t, prm, acc_conv, acc_ssm)
        out_c, acc_shift, acc_rwkv = _rwkv(p_rwkv, s_shift, s_rwkv, l, bsz, t, prm, acc_shift, acc_rwkv)
        x = _merge(out_a, out_b, out_c, p_gate, x, l, prm["w_branch"], prm["w_out"])
        q = _norm_matmul(x, prm["xattn_norm"], l, prm["xattn_wq"], 0, 1024)
        if isinstance(mem_k, (list, tuple)):
            att = _attn(q, mem_k[l], mem_v[l], l, bsz, t)
        else:
            att = _attn(q, mem_k, mem_v, l, bsz, t)
        x = _matmul_residual(att, l, prm["xattn_wo"], x)
        x = _ffn(x, prm["ffb_norm"], l, prm["ffb_w1"], prm["ffb_w3"], prm["ffb_w2"])
    y = _final_norm(x, prm["final_norm"])
    return y, (acc_ret, acc_ssm, acc_conv, acc_rwkv, acc_shift)


def _prep_params(raw):
    p = {}
    for name in ("ffa_w1", "ffa_w3", "ffa_w2", "ffb_w1", "ffb_w3", "ffb_w2", "w_branch", "w_out",
                 "xattn_wq", "xattn_wk", "xattn_wv", "xattn_wo", "rwkv_w_w2", "rwkv_w_a2", "rwkv_w_g2"):
        p[name] = raw[name].astype(BF16)
    w_in = raw["w_in"]
    c0, c1, c2 = RET_PROJ, RET_PROJ + SSM_PROJ, RET_PROJ + SSM_PROJ + RWKV_PROJ
    p["w_ret"] = w_in[:, :, :c0].astype(BF16)
    p["w_ssm"] = jnp.pad(w_in[:, :, c0:c1], ((0, 0), (0, 0), (0, SSM_PROJ_PAD - SSM_PROJ))).astype(BF16)
    p["w_rwkv"] = w_in[:, :, c1:c2].astype(BF16)
    p["w_gate"] = w_in[:, :, c2:].astype(BF16)
    for name in ("ffa_norm", "mix_norm", "ssm_conv_b", "ssm_norm", "rwkv_mu", "rwkv_w0", "rwkv_a0",
                 "rwkv_k_k", "rwkv_k_a", "rwkv_lnx_g", "rwkv_lnx_b", "xattn_norm", "mem_norm", "ffb_norm"):
        p[name] = raw[name][:, None, :]
    p["rwkv_r_k"] = raw["rwkv_r_k"].reshape(DEPTH, 1, 1024)
    pad = ((0, 0), (0, LANES - SSM_HEADS))
    p["ssm_dt_bias"] = jnp.pad(raw["ssm_dt_bias"], pad)[:, None, :]
    p["ssm_a_log"] = jnp.pad(raw["ssm_a_log"], pad)[:, None, :]
    p["ssm_d"] = jnp.repeat(raw["ssm_d"], SSM_HEADDIM, axis=-1)[:, None, :]
    p["ssm_conv_w"] = raw["ssm_conv_w"]
    p["final_norm"] = raw["final_norm"][None, :]
    return p


def _run(x_prompt, x_sample, mem_prompt, states, cache_mem_k, cache_mem_v, raw):
    prm = _prep_params(raw)
    b, t, d = x_prompt.shape
    db, dt_, _ = x_sample.shape
    n_mem = mem_prompt.shape[1]
    mem2 = mem_prompt.reshape(b * n_mem, d)
    mk = [_norm_matmul(mem2, prm["mem_norm"], l, prm["xattn_wk"], 0, 1024) for l in range(DEPTH)]
    mv = [_norm_matmul(mem2, prm["mem_norm"], l, prm["xattn_wv"], 0, 1024) for l in range(DEPTH)]
    mk3 = [a.reshape(b, n_mem, d) for a in mk]
    mv3 = [a.reshape(b, n_mem, d) for a in mv]
    p_mem_k = jnp.stack(mk).reshape(DEPTH, b, n_mem, X_HEADS, X_HEAD_DIM)
    p_mem_v = jnp.stack(mv).reshape(DEPTH, b, n_mem, X_HEADS, X_HEAD_DIM)
    y_p, st_p = _trunk(x_prompt.reshape(b * t, d), prm, None, mk3, mv3, b, t, 0)

    ck = cache_mem_k.reshape(DEPTH, db, n_mem, d)
    cv = cache_mem_v.reshape(DEPTH, db, n_mem, d)
    y_s, st_s = _trunk(x_sample.reshape(db * dt_, d), prm, states, ck, cv, db, dt_, PAST_LEN)
    return (y_p.reshape(b, t, d), y_s.reshape(db, dt_, d)) + st_p + (p_mem_k, p_mem_v) + st_s


def kernel(x_prompt, x_sample, mem_prompt, state_ret, state_ssm, state_conv, state_rwkv, state_shift, cache_mem_k, cache_mem_v, ffa_norm, ffa_w1, ffa_w3, ffa_w2, mix_norm, w_in, ssm_conv_w, ssm_conv_b, ssm_dt_bias, ssm_a_log, ssm_d, ssm_norm, rwkv_mu, rwkv_w0, rwkv_w_w2, rwkv_a0, rwkv_w_a2, rwkv_w_g2, rwkv_k_k, rwkv_k_a, rwkv_r_k, rwkv_lnx_g, rwkv_lnx_b, w_branch, w_out, xattn_norm, mem_norm, xattn_wq, xattn_wk, xattn_wv, xattn_wo, ffb_norm, ffb_w1, ffb_w3, ffb_w2, final_norm):
    raw = dict(ffa_norm=ffa_norm, ffa_w1=ffa_w1, ffa_w3=ffa_w3, ffa_w2=ffa_w2, mix_norm=mix_norm, w_in=w_in,
               ssm_conv_w=ssm_conv_w, ssm_conv_b=ssm_conv_b, ssm_dt_bias=ssm_dt_bias, ssm_a_log=ssm_a_log,
               ssm_d=ssm_d, ssm_norm=ssm_norm, rwkv_mu=rwkv_mu, rwkv_w0=rwkv_w0, rwkv_w_w2=rwkv_w_w2,
               rwkv_a0=rwkv_a0, rwkv_w_a2=rwkv_w_a2, rwkv_w_g2=rwkv_w_g2, rwkv_k_k=rwkv_k_k,
               rwkv_k_a=rwkv_k_a, rwkv_r_k=rwkv_r_k, rwkv_lnx_g=rwkv_lnx_g, rwkv_lnx_b=rwkv_lnx_b,
               w_branch=w_branch, w_out=w_out, xattn_norm=xattn_norm, mem_norm=mem_norm,
               xattn_wq=xattn_wq, xattn_wk=xattn_wk, xattn_wv=xattn_wv, xattn_wo=xattn_wo,
               ffb_norm=ffb_norm, ffb_w1=ffb_w1, ffb_w3=ffb_w3, ffb_w2=ffb_w2, final_norm=final_norm)
    states = (state_ret, state_ssm, state_conv, state_rwkv, state_shift)
    return _run(x_prompt, x_sample, mem_prompt, states, cache_mem_k, cache_mem_v, raw)
```

```python
import functools
import math

import numpy as np
import jax
import jax.numpy as jnp
from jax import lax
from jax.experimental import pallas as pl
from jax.experimental.pallas import tpu as pltpu

F32 = jnp.float32
BF16 = jnp.bfloat16

D_MODEL = 1024
DEPTH = 4
PAST_LEN = 16384
D_FF = 2816
EPS = 1e-6
RET_HEADS = 8
RET_DK = 128
RET_GN_EPS = 1e-5
ROPE_BASE = 10000.0
SSM_HEADS = 16
SSM_HEADDIM = 64
SSM_GROUPS = 2
SSM_STATE = 128
CONV_W = 4
CONV_DIM = 1536
RWKV_HEADS = 16
RWKV_HD = 64
RWKV_PROJ = 3328
RWKV_GN_EPS = 64e-5
N_MEM = 256
X_HEADS = 4
X_HEAD_DIM = 256
RET_PROJ = 4096
SSM_PROJ = 2576
SSM_PROJ_PAD = 2688
GATE_PROJ = 3072
LANES = 128
VMEM_LIMIT = 56 * 1024 * 1024

RET_CHUNK = 128
SSD_CHUNK = 128
RWKV_CHUNK = 64
ATTN_ROWS = 2048
SEQ_BLOCK_ROWS = 64


def _params(sem):
    return pltpu.CompilerParams(dimension_semantics=sem, vmem_limit_bytes=VMEM_LIMIT)


def _rms(x, g):
    return x * lax.rsqrt(jnp.mean(x * x, axis=-1, keepdims=True) + EPS) * g


def _head_norm(x, eps):
    mu = jnp.mean(x, axis=-1, keepdims=True)
    xc = x - mu
    return xc * lax.rsqrt(jnp.mean(xc * xc, axis=-1, keepdims=True) + eps)


def _silu(x):
    return x * jax.nn.sigmoid(x)


def _dot(a, b):
    return jnp.dot(a.astype(BF16), b.astype(BF16), preferred_element_type=F32)


def _dot_nt(a, b):
    return lax.dot_general(a.astype(BF16), b.astype(BF16), (((1,), (1,)), ((), ())),
                           preferred_element_type=F32)


def _dot_tn(a, b):
    return lax.dot_general(a.astype(BF16), b.astype(BF16), (((0,), (0,)), ((), ())),
                           preferred_element_type=F32)


def _dot_f32(a, b):
    return jnp.dot(a, b, precision=lax.Precision.HIGHEST, preferred_element_type=F32)


def _seq_blocking(bsz, t, chunk_max, block_rows):
    if t > chunk_max:
        assert t % chunk_max == 0
        return 1, chunk_max
    nb = max(1, min(bsz, block_rows // t))
    assert bsz % nb == 0
    return nb, t


def _seq_masks(nb, c):
    r = nb * c
    i = np.arange(r)
    same = (i[:, None] // c) == (i[None, :] // c)
    incl = same & (i[:, None] >= i[None, :])
    strict = same & (i[:, None] > i[None, :])
    return same, incl, strict


def _nm_kernel(x_ref, g_ref, w_ref, o_ref, h_ref):
    @pl.when(pl.program_id(1) == 0)
    def _():
        h_ref[...] = _rms(x_ref[...], g_ref[...]).astype(BF16)

    o_ref[...] = jnp.dot(h_ref[...], w_ref[...], preferred_element_type=F32)


def _norm_matmul(x, g, l, w, col0, ncols):
    m, d = x.shape
    tn = ncols
    tm = min(m, 512)
    assert col0 % tn == 0 and m % tm == 0
    cb = col0 // tn
    return pl.pallas_call(
        _nm_kernel,
        grid=(m // tm, ncols // tn),
        in_specs=[
            pl.BlockSpec((tm, d), lambda i, j: (i, 0)),
            pl.BlockSpec((None, 1, d), lambda i, j: (l, 0, 0)),
            pl.BlockSpec((None, d, tn), lambda i, j: (l, 0, cb + j)),
        ],
        out_specs=pl.BlockSpec((tm, tn), lambda i, j: (i, j)),
        out_shape=jax.ShapeDtypeStruct((m, ncols), F32),
        scratch_shapes=[pltpu.VMEM((tm, d), BF16)],
        compiler_params=_params(("parallel", "arbitrary")),
        name="norm_matmul",
    )(x, g, w)


def _ffn_kernel(*refs, pre, final):
    x_ref, g_ref, w1_ref, w3_ref, w2_ref = refs[:5]
    extra = list(refs[5:-1])
    o_ref = refs[-1]
    x = x_ref[...]
    if pre:
        att_ref, wo_ref = extra[:2]
        extra = extra[2:]
        x = x + jnp.dot(att_ref[...].astype(BF16), wo_ref[...], preferred_element_type=F32)
    h = _rms(x, g_ref[...]).astype(BF16)
    a = jnp.dot(h, w1_ref[...], preferred_element_type=F32)
    b = jnp.dot(h, w3_ref[...], preferred_element_type=F32)
    u = (_silu(a) * b).astype(BF16)
    y = x + 0.5 * jnp.dot(u, w2_ref[...], preferred_element_type=F32)
    o_ref[...] = _rms(y, extra[0][...]) if final else y


def _ffn(x, g, l, w1, w3, w2, att=None, wo=None, final_g=None):
    m, d = x.shape
    f = w1.shape[-1]
    tm = min(m, 512)
    once = pl.Buffered(1)
    row = lambda i: (i, 0)
    in_specs = [
        pl.BlockSpec((tm, d), row),
        pl.BlockSpec((None, 1, d), lambda i: (l, 0, 0)),
        pl.BlockSpec((None, d, f), lambda i: (l, 0, 0), pipeline_mode=once),
        pl.BlockSpec((None, d, f), lambda i: (l, 0, 0), pipeline_mode=once),
        pl.BlockSpec((None, f, d), lambda i: (l, 0, 0), pipeline_mode=once),
    ]
    args = [x, g, w1, w3, w2]
    if att is not None:
        in_specs += [pl.BlockSpec((tm, d), row), pl.BlockSpec((None, d, d), lambda i: (l, 0, 0), pipeline_mode=once)]
        args += [att, wo]
    if final_g is not None:
        in_specs.append(pl.BlockSpec((1, d), lambda i: (0, 0)))
        args.append(final_g)
    return pl.pallas_call(
        functools.partial(_ffn_kernel, pre=att is not None, final=final_g is not None),
        grid=(m // tm,),
        in_specs=in_specs,
        out_specs=pl.BlockSpec((tm, d), row),
        out_shape=jax.ShapeDtypeStruct((m, d), F32),
        compiler_params=_params(("parallel",)),
        name="ffn",
    )(*args)


def _ret_kernel(p_ref, cos_ref, sin_ref, dmat_ref, qdec_ref, kdec_ref, st_in_ref, acc_ref,
                o_ref, st_ref, *, nb, c, sdec):
    del acc_ref
    @pl.when(pl.program_id(1) == 0)
    def _():
        st_ref[...] = st_in_ref[...]

    cosf = cos_ref[...]
    sinf = sin_ref[...]
    hd = RET_DK
    heads = range(RET_HEADS)
    rows = [slice(s * c, (s + 1) * c) for s in range(nb)]
    qr, kr, v = [], [], []
    for h in heads:
        q = p_ref[:, h * hd:(h + 1) * hd]
        k = p_ref[:, 1024 + h * hd:1024 + (h + 1) * hd]
        qr.append(((q * cosf + pltpu.roll(q, hd // 2, 1) * sinf) * (RET_DK ** -0.5)).astype(BF16))
        kr.append(k * cosf + pltpu.roll(k, hd // 2, 1) * sinf)
        v.append(p_ref[:, 2048 + h * hd:2048 + (h + 1) * hd].astype(BF16))
    scores = [_dot_nt(qr[h], kr[h]) * dmat_ref[h] for h in heads]
    inter = [[_dot(qr[h][rw], st_ref[s, h]) for s, rw in enumerate(rows)] for h in heads]
    o = [_dot(scores[h], v[h]) for h in heads]
    for h in heads:
        kd = kr[h] * kdec_ref[h]
        qdec = qdec_ref[h]
        for s, rw in enumerate(rows):
            o_s = o[h][rw] + inter[h][s] * qdec[rw]
            g = p_ref[rw, 3072 + h * hd:3072 + (h + 1) * hd]
            o_ref[rw, h * hd:(h + 1) * hd] = _silu(g) * _head_norm(o_s, RET_GN_EPS)
            st_ref[s, h] = st_ref[s, h] * sdec[h] + _dot_tn(kd[rw], v[h][rw])


def _retention(p_ret, st_all, l, bsz, t, offset, acc):
    nb, c = _seq_blocking(bsz, t, RET_CHUNK, SEQ_BLOCK_ROWS)
    r = nb * c
    nchunk = t // c
    heads = np.arange(RET_HEADS, dtype=np.float64)
    log_g = np.log1p(-np.exp2(-5.0 - heads))
    idx = np.arange(r) % c
    _, incl, _ = _seq_masks(nb, c)
    diff = (idx[:, None] - idx[None, :]).astype(np.float64)
    dmat = np.where(incl[None], np.exp(np.maximum(diff, 0.0)[None] * log_g[:, None, None]), 0.0)
    qdec = np.exp((idx + 1.0)[None, :] * log_g[:, None])
    kdec = np.exp((c - 1.0 - idx)[None, :] * log_g[:, None])
    qdec = np.broadcast_to(qdec[:, :, None], (RET_HEADS, r, RET_DK))
    kdec = np.broadcast_to(kdec[:, :, None], (RET_HEADS, r, RET_DK))
    sdec = tuple(float(x) for x in np.exp(c * log_g))

    half = RET_DK // 2
    freqs = ROPE_BASE ** (-jnp.arange(half, dtype=F32) / half)
    pos = jnp.float32(offset) + jnp.arange(t, dtype=F32)
    ang = pos[:, None] * freqs[None, :]
    cos = jnp.cos(ang)
    sin = jnp.sin(ang)
    cosf = jnp.tile(jnp.concatenate([cos, cos], axis=-1), (nb, 1))
    sinf = jnp.tile(jnp.concatenate([-sin, sin], axis=-1), (nb, 1))

    st_shape = (nb, RET_HEADS, RET_DK, RET_DK)
    if st_all is None:
        st_in = jnp.zeros((bsz,) + st_shape[1:], F32)
        st_spec = pl.BlockSpec(st_shape, lambda i, j: (i, 0, 0, 0))
    else:
        st_in = st_all
        st_spec = pl.BlockSpec((None,) + st_shape, lambda i, j: (l, i, 0, 0, 0))
    m = bsz * t
    return pl.pallas_call(
        functools.partial(_ret_kernel, nb=nb, c=c, sdec=sdec),
        grid=(bsz // nb, nchunk),
        in_specs=[
            pl.BlockSpec((r, RET_PROJ), lambda i, j: (i * nchunk + j, 0)),
            pl.BlockSpec((r, RET_DK), lambda i, j: (j, 0)),
            pl.BlockSpec((r, RET_DK), lambda i, j: (j, 0)),
            pl.BlockSpec((RET_HEADS, r, r), lambda i, j: (0, 0, 0)),
            pl.BlockSpec((RET_HEADS, r, RET_DK), lambda i, j: (0, 0, 0)),
            pl.BlockSpec((RET_HEADS, r, RET_DK), lambda i, j: (0, 0, 0)),
            st_spec,
            pl.BlockSpec(memory_space=pl.ANY),
        ],
        out_specs=[
            pl.BlockSpec((r, 1024), lambda i, j: (i * nchunk + j, 0)),
            pl.BlockSpec((None,) + st_shape, lambda i, j: (l, i, 0, 0, 0)),
        ],
        out_shape=[
            jax.ShapeDtypeStruct((m, 1024), F32),
            jax.ShapeDtypeStruct(acc.shape, F32),
        ],
        input_output_aliases={7: 1},
        compiler_params=_params(("parallel", "arbitrary")),
        name="retention",
    )(p_ret, cosf, sinf, jnp.asarray(dmat, F32), jnp.asarray(qdec, F32), jnp.asarray(kdec, F32), st_in, acc)


def _ssd_kernel(p_ref, cw_ref, cb_ref, dtb_ref, alog_ref, dfull_ref, norm_ref, tri_ref, mask2_ref,
                e64_ref, er_ref, conv_in_ref, st_in_ref, conv_acc_ref, st_acc_ref,
                o_ref, conv_out_ref, st_ref, ext_ref, stt_ref, *, nb, c):
    del conv_acc_ref, st_acc_ref
    r = nb * c
    hp = SSM_HEADDIM
    npair = SSM_HEADS // 2
    pairs = range(npair)
    seqs = range(nb)
    rows = [slice(s * c, (s + 1) * c) for s in seqs]
    cols = [slice(q * 2 * hp, (q + 1) * 2 * hp) for q in pairs]
    left = lax.broadcasted_iota(jnp.int32, (1, 2 * hp), 1) < hp

    @pl.when(pl.program_id(1) == 0)
    def _():
        for s in seqs:
            ext_ref[s, 5:8, :] = conv_in_ref[s]
            for q in pairs:
                both = jnp.concatenate([st_in_ref[s, 2 * q], st_in_ref[s, 2 * q + 1]], axis=0)
                stt_ref[s, q] = both.T

    for s in range(nb):
        ext_ref[s, 8:8 + c, :] = p_ref[s * c:(s + 1) * c, 1024:1024 + CONV_DIM]
    pieces = []
    for s in range(nb):
        acc = cb_ref[...] + ext_ref[s, 5:5 + c, :] * cw_ref[0:1, :]
        for j in range(1, CONV_W):
            acc = acc + ext_ref[s, 5 + j:5 + j + c, :] * cw_ref[j:j + 1, :]
        pieces.append(acc)
    for s in range(nb):
        conv_out_ref[s] = ext_ref[s, c + 5:c + 8, :]
        ext_ref[s, 0:8, :] = ext_ref[s, c:c + 8, :]
    xbc = _silu(pieces[0] if nb == 1 else jnp.concatenate(pieces, axis=0))
    xs = xbc[:, :1024]
    bm = xbc[:, 1024:1024 + SSM_GROUPS * SSM_STATE]
    cm = xbc[:, 1024 + SSM_GROUPS * SSM_STATE:]

    dt = jax.nn.softplus(p_ref[:, 1024 + CONV_DIM:] + dtb_ref[...])
    a = -jnp.exp(alog_ref[...])
    tri = tri_ref[...]
    cum = _dot_f32(tri, dt * a)
    cum_t = cum.T

    def expand(v, e_ref):
        hi = v.astype(BF16)
        r1 = v - hi.astype(F32)
        mid = r1.astype(BF16)
        lo = (r1 - mid.astype(F32)).astype(BF16)
        return jnp.dot(jnp.concatenate([hi, mid, lo], axis=1), e_ref[...], preferred_element_type=F32)

    def bd(v):
        vb = v.astype(BF16)
        zero = jnp.zeros_like(vb)
        return jnp.concatenate([jnp.where(left, vb, zero), jnp.where(left, zero, vb)], axis=0)

    def cat_rows(pieces):
        return pieces[0] if nb == 1 else jnp.concatenate(pieces, axis=0)

    cum_full = expand(cum, e64_ref)
    cum_wide = cum_full if r == hp else expand(cum, er_ref)
    xdt = xs * expand(dt, e64_ref)
    last = [cum_full[s * c + c - 1:s * c + c, :] for s in seqs]
    xw = cat_rows([xdt[rw] * jnp.exp(last[s] - cum_full[rw]) for s, rw in enumerate(rows)])
    eci = jnp.exp(cum_full)
    mask2 = mask2_ref[...] > 0.5

    hg = SSM_HEADS // SSM_GROUPS
    bm_g = [bm[:, g * SSM_STATE:(g + 1) * SSM_STATE] for g in range(SSM_GROUPS)]
    cm_g = [cm[:, g * SSM_STATE:(g + 1) * SSM_STATE] for g in range(SSM_GROUPS)]
    cb2 = []
    for g in range(SSM_GROUPS):
        cb = _dot_nt(cm_g[g], bm_g[g])
        cb2.append(jnp.concatenate([cb, cb], axis=1))
    grp = [2 * q // hg for q in pairs]
    inter = [cat_rows([_dot(cm_g[grp[q]][rw], stt_ref[s, q]) for s, rw in enumerate(rows)]) for q in pairs]
    sc = []
    for q in pairs:
        row2 = jnp.concatenate([cum_t[2 * q:2 * q + 1, :], cum_t[2 * q + 1:2 * q + 2, :]], axis=1)
        seg = cum_wide[:, q * 2 * r:(q + 1) * 2 * r] - row2
        sc.append(cb2[grp[q]] * jnp.where(mask2, jnp.exp(seg), 0.0))
    y_p = [jnp.dot(sc[q].astype(BF16), bd(xdt[:, cols[q]]), preferred_element_type=F32)
           + inter[q] * eci[:, cols[q]] for q in pairs]
    for q in pairs:
        for s, rw in enumerate(rows):
            upd = _dot_tn(bm_g[grp[q]][rw], xw[rw, cols[q]])
            stt_ref[s, q] = stt_ref[s, q] * jnp.exp(last[s][:, cols[q]]) + upd
    y = jnp.concatenate(y_p, axis=1) + dfull_ref[...] * xs
    z = p_ref[:, :1024]
    o_ref[...] = _rms(y * _silu(z), norm_ref[...])

    @pl.when(pl.program_id(1) == pl.num_programs(1) - 1)
    def _():
        for s in seqs:
            for q in pairs:
                both = stt_ref[s, q].T
                st_ref[s, 2 * q] = both[:hp]
                st_ref[s, 2 * q + 1] = both[hp:]


def _ssd(p_ssm, conv_all, st_all, l, bsz, t, prm, conv_acc, st_acc):
    nb, c = _seq_blocking(bsz, t, SSD_CHUNK, SEQ_BLOCK_ROWS)
    r = nb * c
    nchunk = t // c
    _, incl, _ = _seq_masks(nb, c)
    tri = jnp.asarray(incl.astype(np.float32))
    mask2 = jnp.asarray(np.tile(incl.astype(np.float32), (1, 2)))

    def expander(width):
        e = np.zeros((LANES, SSM_HEADS * width), np.float32)
        for h in range(SSM_HEADS):
            e[h, h * width:(h + 1) * width] = 1.0
        return jnp.asarray(np.concatenate([e, e, e], axis=0), BF16)

    e64 = expander(SSM_HEADDIM)
    e_r = expander(r)
    st_shape = (nb, SSM_HEADS, SSM_HEADDIM, SSM_STATE)
    cv_shape = (nb, CONV_W - 1, CONV_DIM)
    if st_all is None:
        st_in = jnp.zeros((bsz,) + st_shape[1:], F32)
        conv_in = jnp.zeros((bsz,) + cv_shape[1:], F32)
        st_spec = pl.BlockSpec(st_shape, lambda i, j: (i, 0, 0, 0))
        cv_spec = pl.BlockSpec(cv_shape, lambda i, j: (i, 0, 0))
    else:
        st_in, conv_in = st_all, conv_all
        st_spec = pl.BlockSpec((None,) + st_shape, lambda i, j: (l, i, 0, 0, 0))
        cv_spec = pl.BlockSpec((None,) + cv_shape, lambda i, j: (l, i, 0, 0))
    m = bsz * t

    def vec(n):
        return pl.BlockSpec((None, 1, n), lambda i, j: (l, 0, 0))

    return pl.pallas_call(
        functools.partial(_ssd_kernel, nb=nb, c=c),
        grid=(bsz // nb, nchunk),
        in_specs=[
            pl.BlockSpec((r, SSM_PROJ_PAD), lambda i, j: (i * nchunk + j, 0)),
            pl.BlockSpec((None, CONV_W, CONV_DIM), lambda i, j: (l, 0, 0)),
            vec(CONV_DIM), vec(LANES), vec(LANES), vec(1024), vec(1024),
            pl.BlockSpec((r, r), lambda i, j: (0, 0)),
            pl.BlockSpec((r, 2 * r), lambda i, j: (0, 0)),
            pl.BlockSpec(e64.shape, lambda i, j: (0, 0)),
            pl.BlockSpec(e_r.shape, lambda i, j: (0, 0)),
            cv_spec, st_spec,
            pl.BlockSpec(memory_space=pl.ANY), pl.BlockSpec(memory_space=pl.ANY),
        ],
        out_specs=[
            pl.BlockSpec((r, 1024), lambda i, j: (i * nchunk + j, 0)),
            pl.BlockSpec((None,) + cv_shape, lambda i, j: (l, i, 0, 0)),
            pl.BlockSpec((None,) + st_shape, lambda i, j: (l, i, 0, 0, 0)),
        ],
        out_shape=[
            jax.ShapeDtypeStruct((m, 1024), F32),
            jax.ShapeDtypeStruct(conv_acc.shape, F32),
            jax.ShapeDtypeStruct(st_acc.shape, F32),
        ],
        input_output_aliases={13: 1, 14: 2},
        scratch_shapes=[pltpu.VMEM((nb, 8 + c, CONV_DIM), F32),
                        pltpu.VMEM((nb, SSM_HEADS // 2, SSM_STATE, 2 * SSM_HEADDIM), F32)],
        compiler_params=_params(("parallel", "arbitrary")),
        name="ssd",
    )(p_ssm, prm["ssm_conv_w"], prm["ssm_conv_b"], prm["ssm_dt_bias"], prm["ssm_a_log"],
      prm["ssm_d"], prm["ssm_norm"], tri, mask2, e64, e_r, conv_in, st_in, conv_acc, st_acc)


def _rwkv_kernel(p_ref, mu_ref, w0_ref, ww2_ref, a0_ref, wa2_ref, wg2_ref, kk_ref, ka_ref,
                 rk_ref, lng_ref, lnb_ref, masks_ref, shift_in_ref, st_in_ref, shift_acc_ref, st_acc_ref,
                 o_ref, shift_out_ref, st_ref, ext_ref, bd_ref, *, nb, c):
    del shift_acc_ref, st_acc_ref
    r = nb * c
    hd = RWKV_HD
    npair = RWKV_HEADS // 2
    pairs = range(npair)
    seqs = range(nb)
    rows = [slice(s * c, (s + 1) * c) for s in seqs]
    lane = lax.broadcasted_iota(jnp.int32, (1, 2 * hd), 1)
    left = lane < hd
    sub = lax.broadcasted_iota(jnp.int32, (2 * hd, 1), 0)
    bd_mask = (sub < hd) == left

    @pl.when(pl.program_id(1) == 0)
    def _():
        zero = jnp.zeros((hd, hd), F32)
        for s in seqs:
            ext_ref[s, 7:8, :] = shift_in_ref[s]
            for p in pairs:
                top = jnp.concatenate([st_in_ref[s, 2 * p], zero], axis=1)
                bot = jnp.concatenate([zero, st_in_ref[s, 2 * p + 1]], axis=1)
                bd_ref[s, p] = jnp.concatenate([top, bot], axis=0)

    x = p_ref[...]
    for s in range(nb):
        ext_ref[s, 8:8 + c, :] = x[s * c:(s + 1) * c]
    prev = [ext_ref[s, 7:7 + c, :] for s in range(nb)]
    prev = prev[0] if nb == 1 else jnp.concatenate(prev, axis=0)
    for s in range(nb):
        ext_ref[s, 0:8, :] = ext_ref[s, c:c + 8, :]
        shift_out_ref[s] = ext_ref[s, 7:8, :]

    mixed = x + (prev - x) * mu_ref[...]
    rr = mixed[:, 0:1024]
    kc = mixed[:, 1024:2048]
    vc = mixed[:, 2048:3072]
    wd = mixed[:, 3072:3136]
    ad = mixed[:, 3136:3200]
    gd = mixed[:, 3200:3328]
    w_log = -jax.nn.softplus(-(w0_ref[...] + _dot(jnp.tanh(wd), ww2_ref[...]))) - 0.5
    lw = -jnp.exp(w_log)
    a_lr = jax.nn.sigmoid(a0_ref[...] + _dot(ad, wa2_ref[...]))
    g_rw = _dot(jax.nn.sigmoid(gd), wg2_ref[...])
    kk = kc * kk_ref[...]
    kmod = kc * (1.0 + (a_lr - 1.0) * ka_ref[...])
    rkk = rr * kmod * rk_ref[...]

    m_incl = masks_ref[0]
    m_strict = masks_ref[1]
    m_blk = masks_ref[2]
    eye = masks_ref[3]
    cl = _dot_f32(m_incl[:, :r], lw)

    def half_sum(a):
        sl = jnp.sum(jnp.where(left, a, 0.0), axis=-1, keepdims=True)
        sr = jnp.sum(jnp.where(left, 0.0, a), axis=-1, keepdims=True)
        return jnp.where(left, sl, sr)

    def bd(a):
        ab = a.astype(BF16)
        zero = jnp.zeros_like(ab)
        return jnp.concatenate([jnp.where(left, ab, zero), jnp.where(left, zero, ab)], axis=0)

    def lp_dot(a, b):
        return jnp.dot(a.astype(BF16), bd(b), preferred_element_type=F32)

    def cat_rows(pieces):
        return pieces[0] if nb == 1 else jnp.concatenate(pieces, axis=0)

    cols = [slice(p * 2 * hd, (p + 1) * 2 * hd) for p in pairs]
    v_p = [vc[:, cs] for cs in cols]
    kt, rt, kh, bh, e_last = [], [], [], [], []
    for cs in cols:
        kk_p = kk[:, cs]
        kk_p = kk_p * lax.rsqrt(jnp.maximum(half_sum(kk_p * kk_p), 1e-24))
        cl_p = cl[:, cs]
        e_incl = jnp.exp(cl_p)
        e_inv = jnp.exp(-cl_p)
        rt.append(rr[:, cs] * e_incl)
        kt.append(kk_p * jnp.exp(cl_p - lw[:, cs]))
        kh.append(kmod[:, cs] * e_inv)
        bh.append(kk_p * a_lr[:, cs] * e_inv)
        e_last.append([e_incl[s * c + c - 1:s * c + c, :] for s in seqs])
    lhs = [jnp.concatenate([kt[p], rt[p]], axis=0).astype(BF16) for p in pairs]
    gk = [_dot_nt(lhs[p], bd(kh[p])) for p in pairs]
    gb = [_dot_nt(lhs[p], bd(bh[p])) for p in pairs]
    both = [[_dot_nt(jnp.concatenate([kt[p][rw], rt[p][rw]], axis=0) if nb > 1 else lhs[p],
                     bd_ref[s, p]) for s, rw in enumerate(rows)] for p in pairs]
    a_kk = [g[:r] * m_strict for g in gk]
    a_rk = [g[r:] * m_incl for g in gk]
    a_kb = [g[:r] * m_strict for g in gb]
    a_rb = [g[r:] * m_incl for g in gb]
    akv = [lp_dot(a_kk[p], v_p[p]) for p in pairs]

    n1 = [a * m_blk for a in a_kb]
    n2 = [lp_dot(n, n) for n in n1]
    p1 = [lp_dot(eye - n1[p], eye + n2[p]) for p in pairs]
    n4 = [lp_dot(n, n) for n in n2]
    p2 = [lp_dot(p1[p], eye + n4[p]) for p in pairs]
    n8 = [lp_dot(n, n) for n in n4]
    dinv = [lp_dot(p2[p], eye + n8[p]) for p in pairs]
    xm = [lp_dot(dinv[p], a_kb[p] - n1[p]) for p in pairs]
    x2 = [lp_dot(x_, x_) for x_ in xm]
    t1 = [lp_dot(eye - xm[p], eye + x2[p]) for p in pairs]
    tinv = [lp_dot(t1[p], dinv[p]) for p in pairs]

    ks = [cat_rows([b[:c] for b in both[p]]) for p in pairs]
    rs = [cat_rows([b[c:] for b in both[p]]) for p in pairs]
    u = [lp_dot(tinv[p], ks[p] + akv[p]) for p in pairs]
    y = [rs[p] + lp_dot(a_rk[p], v_p[p]) - lp_dot(a_rb[p], u[p]) for p in pairs]
    for p in pairs:
        for s, rw in enumerate(rows):
            upd = _dot_tn(jnp.concatenate([v_p[p][rw], -u[p][rw]], axis=0),
                          jnp.concatenate([kh[p][rw], bh[p][rw]], axis=0))
            bd_ref[s, p] = (bd_ref[s, p] + jnp.where(bd_mask, upd, 0.0)) * e_last[p][s]
    for p, cs in enumerate(cols):
        bonus = half_sum(rkk[:, cs]) * v_p[p]
        yc = y[p] - half_sum(y[p]) * (1.0 / hd)
        yn = yc * lax.rsqrt(half_sum(yc * yc) * (1.0 / hd) + RWKV_GN_EPS)
        o_ref[:, cs] = (yn * lng_ref[:, cs] + lnb_ref[:, cs] + bonus) * g_rw[:, cs]

    @pl.when(pl.program_id(1) == pl.num_programs(1) - 1)
    def _():
        for s in seqs:
            for p in pairs:
                blk = bd_ref[s, p]
                st_ref[s, 2 * p] = blk[:hd, :hd]
                st_ref[s, 2 * p + 1] = blk[hd:, hd:]


def _rwkv(p_rwkv, shift_all, st_all, l, bsz, t, prm, shift_acc, st_acc):
    nb, c = _seq_blocking(bsz, t, RWKV_CHUNK, SEQ_BLOCK_ROWS)
    r = nb * c
    nchunk = t // c
    assert 2 * r == LANES, "the lane-paired layout holds two (r, r) matrices side by side"
    _, incl, strict = _seq_masks(nb, c)
    i = np.arange(r)
    blk = (i[:, None] // 16) == (i[None, :] // 16)
    masks = np.stack([incl, strict, blk, np.eye(r, dtype=bool)]).astype(np.float32)
    masks = jnp.asarray(np.tile(masks, (1, 1, 2)))
    st_shape = (nb, RWKV_HEADS, RWKV_HD, RWKV_HD)
    sh_shape = (nb, 1, RWKV_PROJ)
    if st_all is None:
        st_in = jnp.zeros((bsz,) + st_shape[1:], F32)
        shift_in = jnp.zeros((bsz,) + sh_shape[1:], F32)
        st_spec = pl.BlockSpec(st_shape, lambda i, j: (i, 0, 0, 0))
        sh_spec = pl.BlockSpec(sh_shape, lambda i, j: (i, 0, 0))
    else:
        st_in, shift_in = st_all, shift_all
        st_spec = pl.BlockSpec((None,) + st_shape, lambda i, j: (l, i, 0, 0, 0))
        sh_spec = pl.BlockSpec((None,) + sh_shape, lambda i, j: (l, i, 0, 0))
    m = bsz * t

    def vec(n):
        return pl.BlockSpec((None, 1, n), lambda i, j: (l, 0, 0))

    def mat(k):
        return pl.BlockSpec((None, k, 1024), lambda i, j: (l, 0, 0))

    return pl.pallas_call(
        functools.partial(_rwkv_kernel, nb=nb, c=c),
        grid=(bsz // nb, nchunk),
        in_specs=[
            pl.BlockSpec((r, RWKV_PROJ), lambda i, j: (i * nchunk + j, 0)),
            vec(RWKV_PROJ), vec(1024), mat(64), vec(1024), mat(64), mat(128),
            vec(1024), vec(1024), vec(1024), vec(1024), vec(1024),
            pl.BlockSpec((4, r, 2 * r), lambda i, j: (0, 0, 0)),
            sh_spec, st_spec,
            pl.BlockSpec(memory_space=pl.ANY), pl.BlockSpec(memory_space=pl.ANY),
        ],
        out_specs=[
            pl.BlockSpec((r, 1024), lambda i, j: (i * nchunk + j, 0)),
            pl.BlockSpec((None,) + sh_shape, lambda i, j: (l, i, 0, 0)),
            pl.BlockSpec((None,) + st_shape, lambda i, j: (l, i, 0, 0, 0)),
        ],
        out_shape=[
            jax.ShapeDtypeStruct((m, 1024), F32),
            jax.ShapeDtypeStruct(shift_acc.shape, F32),
            jax.ShapeDtypeStruct(st_acc.shape, F32),
        ],
        input_output_aliases={15: 1, 16: 2},
        scratch_shapes=[pltpu.VMEM((nb, 8 + c, RWKV_PROJ), F32),
                        pltpu.VMEM((nb, RWKV_HEADS // 2, 2 * RWKV_HD, 2 * RWKV_HD), F32)],
        compiler_params=_params(("parallel", "arbitrary")),
        name="rwkv7",
    )(p_rwkv, prm["rwkv_mu"], prm["rwkv_w0"], prm["rwkv_w_w2"], prm["rwkv_a0"], prm["rwkv_w_a2"],
      prm["rwkv_w_g2"], prm["rwkv_k_k"], prm["rwkv_k_a"], prm["rwkv_r_k"], prm["rwkv_lnx_g"],
      prm["rwkv_lnx_b"], masks, shift_in, st_in, shift_acc, st_acc)


def _merge_kernel(a_ref, b_ref, c_ref, g_ref, x_ref, wb_ref, wo_ref, qn_ref, wq_ref, o_ref, q_ref):
    merged = None
    for i, br in enumerate((a_ref, b_ref, c_ref)):
        gate = jax.nn.sigmoid(g_ref[:, i * 1024:(i + 1) * 1024])
        term = gate * jnp.dot(br[...].astype(BF16), wb_ref[i], preferred_element_type=F32)
        merged = term if merged is None else merged + term
    x = x_ref[...] + jnp.dot(merged.astype(BF16), wo_ref[...], preferred_element_type=F32)
    o_ref[...] = x
    q_ref[...] = jnp.dot(_rms(x, qn_ref[...]).astype(BF16), wq_ref[...], preferred_element_type=F32)


def _merge(out_a, out_b, out_c, gates, x, l, wb, wo, qn, wq):
    m, d = x.shape
    tm = min(m, 256)
    row = lambda i: (i, 0)
    once = pl.Buffered(1)
    return pl.pallas_call(
        _merge_kernel,
        grid=(m // tm,),
        in_specs=[
            pl.BlockSpec((tm, d), row), pl.BlockSpec((tm, d), row), pl.BlockSpec((tm, d), row),
            pl.BlockSpec((tm, GATE_PROJ), row), pl.BlockSpec((tm, d), row),
            pl.BlockSpec((None, 3, d, d), lambda i: (l, 0, 0, 0), pipeline_mode=once),
            pl.BlockSpec((None, d, d), lambda i: (l, 0, 0), pipeline_mode=once),
            pl.BlockSpec((None, 1, d), lambda i: (l, 0, 0)),
            pl.BlockSpec((None, d, d), lambda i: (l, 0, 0), pipeline_mode=once),
        ],
        out_specs=[pl.BlockSpec((tm, d), row), pl.BlockSpec((tm, d), row)],
        out_shape=[jax.ShapeDtypeStruct((m, d), F32), jax.ShapeDtypeStruct((m, d), F32)],
        compiler_params=_params(("parallel",)),
        name="merge",
    )(out_a, out_b, out_c, gates, x, wb, wo, qn, wq)


def _attn_kernel(q_ref, k_ref, v_ref, o_ref, *, nb, c):
    rows = [slice(s * c, (s + 1) * c) for s in range(nb)]
    sc = [_dot_nt(q_ref[rows[s], :], k_ref[s]) * (X_HEAD_DIM ** -0.5) for s in range(nb)]
    e = [jnp.exp(x - jnp.max(x, axis=-1, keepdims=True)) for x in sc]
    pr = [x * (1.0 / jnp.sum(x, axis=-1, keepdims=True)) for x in e]
    for s in range(nb):
        o_ref[rows[s], :] = _dot(pr[s], v_ref[s])


def _attn_cache_kernel(q_ref, k_hbm, v_hbm, o_ref, kbuf, vbuf, sem, *, l, nb, c):
    nh = pl.num_programs(1)
    step = pl.program_id(0) * nh + pl.program_id(1)
    nsteps = pl.num_programs(0) * nh
    slot = step % 2

    def copies(at_step, sl):
        blk, h = at_step // nh, at_step % nh
        src = (l, pl.ds(blk * nb, nb), slice(None), h, slice(None))
        return (pltpu.make_async_copy(k_hbm.at[src], kbuf.at[sl], sem.at[0, sl]),
                pltpu.make_async_copy(v_hbm.at[src], vbuf.at[sl], sem.at[1, sl]))

    @pl.when(step == 0)
    def _():
        for cp in copies(step, slot):
            cp.start()

    @pl.when(step + 1 < nsteps)
    def _():
        for cp in copies(step + 1, 1 - slot):
            cp.start()

    for cp in copies(step, slot):
        cp.wait()
    _attn_kernel(q_ref, kbuf.at[slot], vbuf.at[slot], o_ref, nb=nb, c=c)


def _attn_cache(q, ck, cv, l, bsz, t):
    nb, c = max(1, min(bsz, SEQ_BLOCK_ROWS // t)), t
    r = nb * c
    hd = X_HEAD_DIM
    return pl.pallas_call(
        functools.partial(_attn_cache_kernel, l=l, nb=nb, c=c),
        grid=(bsz // nb, X_HEADS),
        in_specs=[pl.BlockSpec((r, hd), lambda i, h: (i, h)),
                  pl.BlockSpec(memory_space=pl.ANY), pl.BlockSpec(memory_space=pl.ANY)],
        out_specs=pl.BlockSpec((r, hd), lambda i, h: (i, h)),
        out_shape=jax.ShapeDtypeStruct((bsz * t, 1024), F32),
        scratch_shapes=[pltpu.VMEM((2, nb, N_MEM, hd), F32), pltpu.VMEM((2, nb, N_MEM, hd), F32),
                        pltpu.SemaphoreType.DMA((2, 2))],
        compiler_params=_params(("arbitrary", "arbitrary")),
        name="xattn_cache",
    )(q, ck, cv)


def _attn(q, mk, mv, l, bsz, t):
    if t > ATTN_ROWS:
        nb, c = 1, ATTN_ROWS
    else:
        nb, c = max(1, min(bsz, SEQ_BLOCK_ROWS // t)), t
    r = nb * c
    ntile = t // c
    hd = X_HEAD_DIM
    if mk.ndim == 3:
        kv_spec = pl.BlockSpec((nb, N_MEM, hd), lambda i, j, h: (i, 0, h))
    else:
        kv_spec = pl.BlockSpec((None, nb, N_MEM, hd), lambda i, j, h: (l, i, 0, h))
    return pl.pallas_call(
        functools.partial(_attn_kernel, nb=nb, c=c),
        grid=(bsz // nb, ntile, X_HEADS),
        in_specs=[pl.BlockSpec((r, hd), lambda i, j, h: (i * ntile + j, h)), kv_spec, kv_spec],
        out_specs=pl.BlockSpec((r, hd), lambda i, j, h: (i * ntile + j, h)),
        out_shape=jax.ShapeDtypeStruct((bsz * t, 1024), F32),
        compiler_params=_params(("parallel", "arbitrary", "arbitrary")),
        name="xattn_core",
    )(q, mk, mv)


def _trunk(x, prm, states, mem_k, mem_v, bsz, t, offset):
    acc_ret = jnp.zeros((DEPTH, bsz, RET_HEADS, RET_DK, RET_DK), F32)
    acc_ssm = jnp.zeros((DEPTH, bsz, SSM_HEADS, SSM_HEADDIM, SSM_STATE), F32)
    acc_conv = jnp.zeros((DEPTH, bsz, CONV_W - 1, CONV_DIM), F32)
    acc_rwkv = jnp.zeros((DEPTH, bsz, RWKV_HEADS, RWKV_HD, RWKV_HD), F32)
    acc_shift = jnp.zeros((DEPTH, bsz, 1, RWKV_PROJ), F32)
    for l in range(DEPTH):
        x = _ffn(x, prm["ffa_norm"], l, prm["ffa_w1"], prm["ffa_w3"], prm["ffa_w2"])
        p_ret = _norm_matmul(x, prm["mix_norm"], l, prm["w_ret"], 0, RET_PROJ)
        p_ssm = _norm_matmul(x, prm["mix_norm"], l, prm["w_ssm"], 0, SSM_PROJ_PAD)
        p_rwkv = _norm_matmul(x, prm["mix_norm"], l, prm["w_rwkv"], 0, RWKV_PROJ)
        p_gate = _norm_matmul(x, prm["mix_norm"], l, prm["w_gate"], 0, GATE_PROJ)
        if states is None:
            s_ret = s_ssm = s_conv = s_rwkv = s_shift = None
        else:
            s_ret, s_ssm, s_conv, s_rwkv, s_shift = states
        out_a, acc_ret = _retention(p_ret, s_ret, l, bsz, t, offset, acc_ret)
        out_b, acc_conv, acc_ssm = _ssd(p_ssm, s_conv, s_ssm, l, bsz, t, prm, acc_conv, acc_ssm)
        out_c, acc_shift, acc_rwkv = _rwkv(p_rwkv, s_shift, s_rwkv, l, bsz, t, prm, acc_shift, acc_rwkv)
        x, q = _merge(out_a, out_b, out_c, p_gate, x, l, prm["w_branch"], prm["w_out"],
                      prm["xattn_norm"], prm["xattn_wq"])
        if isinstance(mem_k, (list, tuple)):
            att = _attn(q, mem_k[l], mem_v[l], l, bsz, t)
        else:
            att = _attn_cache(q, mem_k, mem_v, l, bsz, t)
        x = _ffn(x, prm["ffb_norm"], l, prm["ffb_w1"], prm["ffb_w3"], prm["ffb_w2"], att=att, wo=prm["xattn_wo"],
                 final_g=prm["final_norm"] if l == DEPTH - 1 else None)
    return x, (acc_ret, acc_ssm, acc_conv, acc_rwkv, acc_shift)


def _prep_params(raw):
    p = {}
    for name in ("ffa_w1", "ffa_w3", "ffa_w2", "ffb_w1", "ffb_w3", "ffb_w2", "w_branch", "w_out",
                 "xattn_wq", "xattn_wk", "xattn_wv", "xattn_wo", "rwkv_w_w2", "rwkv_w_a2", "rwkv_w_g2"):
        p[name] = raw[name].astype(BF16)
    w_in = raw["w_in"]
    c0, c1, c2 = RET_PROJ, RET_PROJ + SSM_PROJ, RET_PROJ + SSM_PROJ + RWKV_PROJ
    p["w_ret"] = w_in[:, :, :c0].astype(BF16)
    p["w_ssm"] = jnp.pad(w_in[:, :, c0:c1], ((0, 0), (0, 0), (0, SSM_PROJ_PAD - SSM_PROJ))).astype(BF16)
    p["w_rwkv"] = w_in[:, :, c1:c2].astype(BF16)
    p["w_gate"] = w_in[:, :, c2:].astype(BF16)
    for name in ("ffa_norm", "mix_norm", "ssm_conv_b", "ssm_norm", "rwkv_mu", "rwkv_w0", "rwkv_a0",
                 "rwkv_k_k", "rwkv_k_a", "rwkv_lnx_g", "rwkv_lnx_b", "xattn_norm", "mem_norm", "ffb_norm"):
        p[name] = raw[name][:, None, :]
    p["rwkv_r_k"] = raw["rwkv_r_k"].reshape(DEPTH, 1, 1024)
    pad = ((0, 0), (0, LANES - SSM_HEADS))
    p["ssm_dt_bias"] = jnp.pad(raw["ssm_dt_bias"], pad)[:, None, :]
    p["ssm_a_log"] = jnp.pad(raw["ssm_a_log"], pad)[:, None, :]
    p["ssm_d"] = jnp.repeat(raw["ssm_d"], SSM_HEADDIM, axis=-1)[:, None, :]
    p["ssm_conv_w"] = raw["ssm_conv_w"]
    p["final_norm"] = raw["final_norm"][None, :]
    return p


def _run(x_prompt, x_sample, mem_prompt, states, cache_mem_k, cache_mem_v, raw):
    prm = _prep_params(raw)
    b, t, d = x_prompt.shape
    db, dt_, _ = x_sample.shape
    n_mem = mem_prompt.shape[1]
    mem2 = mem_prompt.reshape(b * n_mem, d)
    mk = [_norm_matmul(mem2, prm["mem_norm"], l, prm["xattn_wk"], 0, 1024) for l in range(DEPTH)]
    mv = [_norm_matmul(mem2, prm["mem_norm"], l, prm["xattn_wv"], 0, 1024) for l in range(DEPTH)]
    mk3 = [a.reshape(b, n_mem, d) for a in mk]
    mv3 = [a.reshape(b, n_mem, d) for a in mv]
    p_mem_k = jnp.stack(mk).reshape(DEPTH, b, n_mem, X_HEADS, X_HEAD_DIM)
    p_mem_v = jnp.stack(mv).reshape(DEPTH, b, n_mem, X_HEADS, X_HEAD_DIM)
    y_p, st_p = _trunk(x_prompt.reshape(b * t, d), prm, None, mk3, mv3, b, t, 0)

    y_s, st_s = _trunk(x_sample.reshape(db * dt_, d), prm, states, cache_mem_k, cache_mem_v, db, dt_, PAST_LEN)
    return (y_p.reshape(b, t, d), y_s.reshape(db, dt_, d)) + st_p + (p_mem_k, p_mem_v) + st_s


def kernel(x_prompt, x_sample, mem_prompt, state_ret, state_ssm, state_conv, state_rwkv, state_shift, cache_mem_k, cache_mem_v, ffa_norm, ffa_w1, ffa_w3, ffa_w2, mix_norm, w_in, ssm_conv_w, ssm_conv_b, ssm_dt_bias, ssm_a_log, ssm_d, ssm_norm, rwkv_mu, rwkv_w0, rwkv_w_w2, rwkv_a0, rwkv_w_a2, rwkv_w_g2, rwkv_k_k, rwkv_k_a, rwkv_r_k, rwkv_lnx_g, rwkv_lnx_b, w_branch, w_out, xattn_norm, mem_norm, xattn_wq, xattn_wk, xattn_wv, xattn_wo, ffb_norm, ffb_w1, ffb_w3, ffb_w2, final_norm):
    raw = dict(ffa_norm=ffa_norm, ffa_w1=ffa_w1, ffa_w3=ffa_w3, ffa_w2=ffa_w2, mix_norm=mix_norm, w_in=w_in,
               ssm_conv_w=ssm_conv_w, ssm_conv_b=ssm_conv_b, ssm_dt_bias=ssm_dt_bias, ssm_a_log=ssm_a_log,
               ssm_d=ssm_d, ssm_norm=ssm_norm, rwkv_mu=rwkv_mu, rwkv_w0=rwkv_w0, rwkv_w_w2=rwkv_w_w2,
               rwkv_a0=rwkv_a0, rwkv_w_a2=rwkv_w_a2, rwkv_w_g2=rwkv_w_g2, rwkv_k_k=rwkv_k_k,
               rwkv_k_a=rwkv_k_a, rwkv_r_k=rwkv_r_k, rwkv_lnx_g=rwkv_lnx_g, rwkv_lnx_b=rwkv_lnx_b,
               w_branch=w_branch, w_out=w_out, xattn_norm=xattn_norm, mem_norm=mem_norm,
               xattn_wq=xattn_wq, xattn_wk=xattn_wk, xattn_wv=xattn_wv, xattn_wo=xattn_wo,
               ffb_norm=ffb_norm, ffb_w1=ffb_w1, ffb_w3=ffb_w3, ffb_w2=ffb_w2, final_norm=final_norm)
    states = (state_ret, state_ssm, state_conv, state_rwkv, state_shift)
    return _run(x_prompt, x_sample, mem_prompt, states, cache_mem_k, cache_mem_v, raw)
```

```python
import functools
import math

import numpy as np
import jax
import jax.numpy as jnp
from jax import lax
from jax.experimental import pallas as pl
from jax.experimental.pallas import tpu as pltpu

F32 = jnp.float32
BF16 = jnp.bfloat16

D_MODEL = 1024
DEPTH = 4
PAST_LEN = 16384
D_FF = 2816
EPS = 1e-6
RET_HEADS = 8
RET_DK = 128
RET_GN_EPS = 1e-5
ROPE_BASE = 10000.0
SSM_HEADS = 16
SSM_HEADDIM = 64
SSM_GROUPS = 2
SSM_STATE = 128
CONV_W = 4
CONV_DIM = 1536
RWKV_HEADS = 16
RWKV_HD = 64
RWKV_PROJ = 3328
RWKV_GN_EPS = 64e-5
N_MEM = 256
X_HEADS = 4
X_HEAD_DIM = 256
RET_PROJ = 4096
SSM_PROJ = 2576
SSM_PROJ_PAD = 2688
GATE_PROJ = 3072
LANES = 128
VMEM_LIMIT = 56 * 1024 * 1024

RET_CHUNK = 128
SSD_CHUNK = 128
RWKV_CHUNK = 64
ATTN_ROWS = 2048
SEQ_BLOCK_ROWS = 64


def _params(sem):
    return pltpu.CompilerParams(dimension_semantics=sem, vmem_limit_bytes=VMEM_LIMIT)


def _rms(x, g):
    return x * lax.rsqrt(jnp.mean(x * x, axis=-1, keepdims=True) + EPS) * g


def _head_norm(x, eps):
    mu = jnp.mean(x, axis=-1, keepdims=True)
    xc = x - mu
    return xc * lax.rsqrt(jnp.mean(xc * xc, axis=-1, keepdims=True) + eps)


def _silu(x):
    return x * jax.nn.sigmoid(x)


def _dot(a, b):
    return jnp.dot(a.astype(BF16), b.astype(BF16), preferred_element_type=F32)


def _dot_nt(a, b):
    return lax.dot_general(a.astype(BF16), b.astype(BF16), (((1,), (1,)), ((), ())),
                           preferred_element_type=F32)


def _dot_tn(a, b):
    return lax.dot_general(a.astype(BF16), b.astype(BF16), (((0,), (0,)), ((), ())),
                           preferred_element_type=F32)


def _dot_f32(a, b):
    return jnp.dot(a, b, precision=lax.Precision.HIGHEST, preferred_element_type=F32)


def _seq_blocking(bsz, t, chunk_max, block_rows):
    if t > chunk_max:
        assert t % chunk_max == 0
        return 1, chunk_max
    nb = max(1, min(bsz, block_rows // t))
    assert bsz % nb == 0
    return nb, t


def _seq_masks(nb, c):
    r = nb * c
    i = np.arange(r)
    same = (i[:, None] // c) == (i[None, :] // c)
    incl = same & (i[:, None] >= i[None, :])
    strict = same & (i[:, None] > i[None, :])
    return same, incl, strict


def _nm_kernel(x_ref, g_ref, w_ref, o_ref, h_ref):
    @pl.when(pl.program_id(1) == 0)
    def _():
        h_ref[...] = _rms(x_ref[...], g_ref[...]).astype(BF16)

    o_ref[...] = jnp.dot(h_ref[...], w_ref[...], preferred_element_type=F32)


def _norm_matmul(x, g, l, w, col0, ncols):
    m, d = x.shape
    tn = ncols
    tm = min(m, 512)
    assert col0 % tn == 0 and m % tm == 0
    cb = col0 // tn
    return pl.pallas_call(
        _nm_kernel,
        grid=(m // tm, ncols // tn),
        in_specs=[
            pl.BlockSpec((tm, d), lambda i, j: (i, 0)),
            pl.BlockSpec((None, 1, d), lambda i, j: (l, 0, 0)),
            pl.BlockSpec((None, d, tn), lambda i, j: (l, 0, cb + j)),
        ],
        out_specs=pl.BlockSpec((tm, tn), lambda i, j: (i, j)),
        out_shape=jax.ShapeDtypeStruct((m, ncols), F32),
        scratch_shapes=[pltpu.VMEM((tm, d), BF16)],
        compiler_params=_params(("parallel", "arbitrary")),
        name="norm_matmul",
    )(x, g, w)


def _ffn_kernel(*refs, pre, final):
    x_ref, g_ref, w1_ref, w3_ref, w2_ref = refs[:5]
    extra = list(refs[5:-1])
    o_ref = refs[-1]
    x = x_ref[...]
    if pre:
        att_ref, wo_ref = extra[:2]
        extra = extra[2:]
        x = x + jnp.dot(att_ref[...].astype(BF16), wo_ref[...], preferred_element_type=F32)
    h = _rms(x, g_ref[...]).astype(BF16)
    a = jnp.dot(h, w1_ref[...], preferred_element_type=F32)
    b = jnp.dot(h, w3_ref[...], preferred_element_type=F32)
    u = (_silu(a) * b).astype(BF16)
    y = x + 0.5 * jnp.dot(u, w2_ref[...], preferred_element_type=F32)
    o_ref[...] = _rms(y, extra[0][...]) if final else y


def _ffn(x, g, l, w1, w3, w2, att=None, wo=None, final_g=None):
    m, d = x.shape
    f = w1.shape[-1]
    tm = min(m, 512)
    once = pl.Buffered(1)
    row = lambda i: (i, 0)
    in_specs = [
        pl.BlockSpec((tm, d), row),
        pl.BlockSpec((None, 1, d), lambda i: (l, 0, 0)),
        pl.BlockSpec((None, d, f), lambda i: (l, 0, 0), pipeline_mode=once),
        pl.BlockSpec((None, d, f), lambda i: (l, 0, 0), pipeline_mode=once),
        pl.BlockSpec((None, f, d), lambda i: (l, 0, 0), pipeline_mode=once),
    ]
    args = [x, g, w1, w3, w2]
    if att is not None:
        in_specs += [pl.BlockSpec((tm, d), row), pl.BlockSpec((None, d, d), lambda i: (l, 0, 0), pipeline_mode=once)]
        args += [att, wo]
    if final_g is not None:
        in_specs.append(pl.BlockSpec((1, d), lambda i: (0, 0)))
        args.append(final_g)
    return pl.pallas_call(
        functools.partial(_ffn_kernel, pre=att is not None, final=final_g is not None),
        grid=(m // tm,),
        in_specs=in_specs,
        out_specs=pl.BlockSpec((tm, d), row),
        out_shape=jax.ShapeDtypeStruct((m, d), F32),
        compiler_params=_params(("parallel",)),
        name="ffn",
    )(*args)


def _ret_kernel(p_ref, cos_ref, sin_ref, dmat_ref, qdec_ref, kdec_ref, st_in_ref, acc_ref,
                o_ref, st_ref, *, nb, c, sdec):
    del acc_ref
    @pl.when(pl.program_id(1) == 0)
    def _():
        st_ref[...] = st_in_ref[...]

    cosf = cos_ref[...]
    sinf = sin_ref[...]
    hd = RET_DK
    heads = range(RET_HEADS)
    rows = [slice(s * c, (s + 1) * c) for s in range(nb)]
    qr, kr, v = [], [], []
    for h in heads:
        q = p_ref[:, h * hd:(h + 1) * hd]
        k = p_ref[:, 1024 + h * hd:1024 + (h + 1) * hd]
        qr.append(((q * cosf + pltpu.roll(q, hd // 2, 1) * sinf) * (RET_DK ** -0.5)).astype(BF16))
        kr.append(k * cosf + pltpu.roll(k, hd // 2, 1) * sinf)
        v.append(p_ref[:, 2048 + h * hd:2048 + (h + 1) * hd].astype(BF16))
    scores = [_dot_nt(qr[h], kr[h]) * dmat_ref[h] for h in heads]
    inter = [[_dot(qr[h][rw], st_ref[s, h]) for s, rw in enumerate(rows)] for h in heads]
    o = [_dot(scores[h], v[h]) for h in heads]
    for h in heads:
        kd = kr[h] * kdec_ref[h]
        qdec = qdec_ref[h]
        for s, rw in enumerate(rows):
            o_s = o[h][rw] + inter[h][s] * qdec[rw]
            g = p_ref[rw, 3072 + h * hd:3072 + (h + 1) * hd]
            o_ref[rw, h * hd:(h + 1) * hd] = _silu(g) * _head_norm(o_s, RET_GN_EPS)
            st_ref[s, h] = st_ref[s, h] * sdec[h] + _dot_tn(kd[rw], v[h][rw])


def _retention(p_ret, st_all, l, bsz, t, offset, acc):
    nb, c = _seq_blocking(bsz, t, RET_CHUNK, SEQ_BLOCK_ROWS)
    r = nb * c
    nchunk = t // c
    heads = np.arange(RET_HEADS, dtype=np.float64)
    log_g = np.log1p(-np.exp2(-5.0 - heads))
    idx = np.arange(r) % c
    _, incl, _ = _seq_masks(nb, c)
    diff = (idx[:, None] - idx[None, :]).astype(np.float64)
    dmat = np.where(incl[None], np.exp(np.maximum(diff, 0.0)[None] * log_g[:, None, None]), 0.0)
    qdec = np.exp((idx + 1.0)[None, :] * log_g[:, None])
    kdec = np.exp((c - 1.0 - idx)[None, :] * log_g[:, None])
    qdec = np.broadcast_to(qdec[:, :, None], (RET_HEADS, r, RET_DK))
    kdec = np.broadcast_to(kdec[:, :, None], (RET_HEADS, r, RET_DK))
    sdec = tuple(float(x) for x in np.exp(c * log_g))

    half = RET_DK // 2
    freqs = ROPE_BASE ** (-jnp.arange(half, dtype=F32) / half)
    pos = jnp.float32(offset) + jnp.arange(t, dtype=F32)
    ang = pos[:, None] * freqs[None, :]
    cos = jnp.cos(ang)
    sin = jnp.sin(ang)
    cosf = jnp.tile(jnp.concatenate([cos, cos], axis=-1), (nb, 1))
    sinf = jnp.tile(jnp.concatenate([-sin, sin], axis=-1), (nb, 1))

    st_shape = (nb, RET_HEADS, RET_DK, RET_DK)
    if st_all is None:
        st_in = jnp.zeros((bsz,) + st_shape[1:], F32)
        st_spec = pl.BlockSpec(st_shape, lambda i, j: (i, 0, 0, 0))
    else:
        st_in = st_all
        st_spec = pl.BlockSpec((None,) + st_shape, lambda i, j: (l, i, 0, 0, 0))
    m = bsz * t
    return pl.pallas_call(
        functools.partial(_ret_kernel, nb=nb, c=c, sdec=sdec),
        grid=(bsz // nb, nchunk),
        in_specs=[
            pl.BlockSpec((r, RET_PROJ), lambda i, j: (i * nchunk + j, 0)),
            pl.BlockSpec((r, RET_DK), lambda i, j: (j, 0)),
            pl.BlockSpec((r, RET_DK), lambda i, j: (j, 0)),
            pl.BlockSpec((RET_HEADS, r, r), lambda i, j: (0, 0, 0)),
            pl.BlockSpec((RET_HEADS, r, RET_DK), lambda i, j: (0, 0, 0)),
            pl.BlockSpec((RET_HEADS, r, RET_DK), lambda i, j: (0, 0, 0)),
            st_spec,
            pl.BlockSpec(memory_space=pl.ANY),
        ],
        out_specs=[
            pl.BlockSpec((r, 1024), lambda i, j: (i * nchunk + j, 0)),
            pl.BlockSpec((None,) + st_shape, lambda i, j: (l, i, 0, 0, 0)),
        ],
        out_shape=[
            jax.ShapeDtypeStruct((m, 1024), F32),
            jax.ShapeDtypeStruct(acc.shape, F32),
        ],
        input_output_aliases={7: 1},
        compiler_params=_params(("parallel", "arbitrary")),
        name="retention",
    )(p_ret, cosf, sinf, jnp.asarray(dmat, F32), jnp.asarray(qdec, F32), jnp.asarray(kdec, F32), st_in, acc)


def _ssd_kernel(p_ref, cw_ref, cb_ref, dtb_ref, alog_ref, dfull_ref, norm_ref, tri_ref, mask2_ref,
                e64_ref, er_ref, conv_in_ref, st_in_ref, conv_acc_ref, st_acc_ref,
                o_ref, conv_out_ref, st_ref, ext_ref, stt_ref, *, nb, c):
    del conv_acc_ref, st_acc_ref
    r = nb * c
    hp = SSM_HEADDIM
    npair = SSM_HEADS // 2
    pairs = range(npair)
    seqs = range(nb)
    rows = [slice(s * c, (s + 1) * c) for s in seqs]
    cols = [slice(q * 2 * hp, (q + 1) * 2 * hp) for q in pairs]
    left = lax.broadcasted_iota(jnp.int32, (1, 2 * hp), 1) < hp

    @pl.when(pl.program_id(1) == 0)
    def _():
        for s in seqs:
            ext_ref[s, 5:8, :] = conv_in_ref[s]
            for q in pairs:
                both = jnp.concatenate([st_in_ref[s, 2 * q], st_in_ref[s, 2 * q + 1]], axis=0)
                stt_ref[s, q] = both.T

    for s in range(nb):
        ext_ref[s, 8:8 + c, :] = p_ref[s * c:(s + 1) * c, 1024:1024 + CONV_DIM]
    pieces = []
    for s in range(nb):
        acc = cb_ref[...] + ext_ref[s, 5:5 + c, :] * cw_ref[0:1, :]
        for j in range(1, CONV_W):
            acc = acc + ext_ref[s, 5 + j:5 + j + c, :] * cw_ref[j:j + 1, :]
        pieces.append(acc)
    for s in range(nb):
        conv_out_ref[s] = ext_ref[s, c + 5:c + 8, :]
        ext_ref[s, 0:8, :] = ext_ref[s, c:c + 8, :]
    xbc = _silu(pieces[0] if nb == 1 else jnp.concatenate(pieces, axis=0))
    xs = xbc[:, :1024]
    bm = xbc[:, 1024:1024 + SSM_GROUPS * SSM_STATE]
    cm = xbc[:, 1024 + SSM_GROUPS * SSM_STATE:]

    dt = jax.nn.softplus(p_ref[:, 1024 + CONV_DIM:] + dtb_ref[...])
    a = -jnp.exp(alog_ref[...])
    tri = tri_ref[...]
    cum = _dot_f32(tri, dt * a)
    cum_t = cum.T

    def expand(v, e_ref):
        hi = v.astype(BF16)
        r1 = v - hi.astype(F32)
        mid = r1.astype(BF16)
        lo = (r1 - mid.astype(F32)).astype(BF16)
        return jnp.dot(jnp.concatenate([hi, mid, lo], axis=1), e_ref[...], preferred_element_type=F32)

    def bd(v):
        vb = v.astype(BF16)
        zero = jnp.zeros_like(vb)
        return jnp.concatenate([jnp.where(left, vb, zero), jnp.where(left, zero, vb)], axis=0)

    def cat_rows(pieces):
        return pieces[0] if nb == 1 else jnp.concatenate(pieces, axis=0)

    cum_full = expand(cum, e64_ref)
    cum_wide = cum_full if r == hp else expand(cum, er_ref)
    xdt = xs * expand(dt, e64_ref)
    last = [cum_full[s * c + c - 1:s * c + c, :] for s in seqs]
    xw = cat_rows([xdt[rw] * jnp.exp(last[s] - cum_full[rw]) for s, rw in enumerate(rows)])
    eci = jnp.exp(cum_full)
    mask2 = mask2_ref[...] > 0.5

    hg = SSM_HEADS // SSM_GROUPS
    bm_g = [bm[:, g * SSM_STATE:(g + 1) * SSM_STATE] for g in range(SSM_GROUPS)]
    cm_g = [cm[:, g * SSM_STATE:(g + 1) * SSM_STATE] for g in range(SSM_GROUPS)]
    cb2 = []
    for g in range(SSM_GROUPS):
        cb = _dot_nt(cm_g[g], bm_g[g])
        cb2.append(jnp.concatenate([cb, cb], axis=1))
    grp = [2 * q // hg for q in pairs]
    inter = [cat_rows([_dot(cm_g[grp[q]][rw], stt_ref[s, q]) for s, rw in enumerate(rows)]) for q in pairs]
    sc = []
    for q in pairs:
        row2 = jnp.concatenate([cum_t[2 * q:2 * q + 1, :], cum_t[2 * q + 1:2 * q + 2, :]], axis=1)
        seg = cum_wide[:, q * 2 * r:(q + 1) * 2 * r] - row2
        sc.append(cb2[grp[q]] * jnp.where(mask2, jnp.exp(seg), 0.0))
    y_p = [jnp.dot(sc[q].astype(BF16), bd(xdt[:, cols[q]]), preferred_element_type=F32)
           + inter[q] * eci[:, cols[q]] for q in pairs]
    for q in pairs:
        for s, rw in enumerate(rows):
            upd = _dot_tn(bm_g[grp[q]][rw], xw[rw, cols[q]])
            stt_ref[s, q] = stt_ref[s, q] * jnp.exp(last[s][:, cols[q]]) + upd
    y = jnp.concatenate(y_p, axis=1) + dfull_ref[...] * xs
    z = p_ref[:, :1024]
    o_ref[...] = _rms(y * _silu(z), norm_ref[...])

    @pl.when(pl.program_id(1) == pl.num_programs(1) - 1)
    def _():
        for s in seqs:
            for q in pairs:
                both = stt_ref[s, q].T
                st_ref[s, 2 * q] = both[:hp]
                st_ref[s, 2 * q + 1] = both[hp:]


def _ssd(p_ssm, conv_all, st_all, l, bsz, t, prm, conv_acc, st_acc):
    nb, c = _seq_blocking(bsz, t, SSD_CHUNK, SEQ_BLOCK_ROWS)
    r = nb * c
    nchunk = t // c
    _, incl, _ = _seq_masks(nb, c)
    tri = jnp.asarray(incl.astype(np.float32))
    mask2 = jnp.asarray(np.tile(incl.astype(np.float32), (1, 2)))

    def expander(width):
        e = np.zeros((LANES, SSM_HEADS * width), np.float32)
        for h in range(SSM_HEADS):
            e[h, h * width:(h + 1) * width] = 1.0
        return jnp.asarray(np.concatenate([e, e, e], axis=0), BF16)

    e64 = expander(SSM_HEADDIM)
    e_r = expander(r)
    st_shape = (nb, SSM_HEADS, SSM_HEADDIM, SSM_STATE)
    cv_shape = (nb, CONV_W - 1, CONV_DIM)
    if st_all is None:
        st_in = jnp.zeros((bsz,) + st_shape[1:], F32)
        conv_in = jnp.zeros((bsz,) + cv_shape[1:], F32)
        st_spec = pl.BlockSpec(st_shape, lambda i, j: (i, 0, 0, 0))
        cv_spec = pl.BlockSpec(cv_shape, lambda i, j: (i, 0, 0))
    else:
        st_in, conv_in = st_all, conv_all
        st_spec = pl.BlockSpec((None,) + st_shape, lambda i, j: (l, i, 0, 0, 0))
        cv_spec = pl.BlockSpec((None,) + cv_shape, lambda i, j: (l, i, 0, 0))
    m = bsz * t

    def vec(n):
        return pl.BlockSpec((None, 1, n), lambda i, j: (l, 0, 0))

    return pl.pallas_call(
        functools.partial(_ssd_kernel, nb=nb, c=c),
        grid=(bsz // nb, nchunk),
        in_specs=[
            pl.BlockSpec((r, SSM_PROJ_PAD), lambda i, j: (i * nchunk + j, 0)),
            pl.BlockSpec((None, CONV_W, CONV_DIM), lambda i, j: (l, 0, 0)),
            vec(CONV_DIM), vec(LANES), vec(LANES), vec(1024), vec(1024),
            pl.BlockSpec((r, r), lambda i, j: (0, 0)),
            pl.BlockSpec((r, 2 * r), lambda i, j: (0, 0)),
            pl.BlockSpec(e64.shape, lambda i, j: (0, 0)),
            pl.BlockSpec(e_r.shape, lambda i, j: (0, 0)),
            cv_spec, st_spec,
            pl.BlockSpec(memory_space=pl.ANY), pl.BlockSpec(memory_space=pl.ANY),
        ],
        out_specs=[
            pl.BlockSpec((r, 1024), lambda i, j: (i * nchunk + j, 0)),
            pl.BlockSpec((None,) + cv_shape, lambda i, j: (l, i, 0, 0)),
            pl.BlockSpec((None,) + st_shape, lambda i, j: (l, i, 0, 0, 0)),
        ],
        out_shape=[
            jax.ShapeDtypeStruct((m, 1024), F32),
            jax.ShapeDtypeStruct(conv_acc.shape, F32),
            jax.ShapeDtypeStruct(st_acc.shape, F32),
        ],
        input_output_aliases={13: 1, 14: 2},
        scratch_shapes=[pltpu.VMEM((nb, 8 + c, CONV_DIM), F32),
                        pltpu.VMEM((nb, SSM_HEADS // 2, SSM_STATE, 2 * SSM_HEADDIM), F32)],
        compiler_params=_params(("parallel", "arbitrary")),
        name="ssd",
    )(p_ssm, prm["ssm_conv_w"], prm["ssm_conv_b"], prm["ssm_dt_bias"], prm["ssm_a_log"],
      prm["ssm_d"], prm["ssm_norm"], tri, mask2, e64, e_r, conv_in, st_in, conv_acc, st_acc)


def _rwkv_kernel(p_ref, mu_ref, w0_ref, ww2_ref, a0_ref, wa2_ref, wg2_ref, kk_ref, ka_ref,
                 rk_ref, lng_ref, lnb_ref, masks_ref, tri_ref, shift_in_ref, st_in_ref, shift_acc_ref,
                 st_acc_ref, o_ref, shift_out_ref, st_ref, ext_ref, bd_ref, *, nq, nb, c):
    del shift_acc_ref, st_acc_ref
    r = nb * c
    hd = RWKV_HD
    npair = RWKV_HEADS // 2
    seqs = range(nq * nb)
    rows = [slice(s * c, (s + 1) * c) for s in seqs]
    lane = lax.broadcasted_iota(jnp.int32, (1, 2 * hd), 1)
    left = lane < hd
    sub = lax.broadcasted_iota(jnp.int32, (2 * hd, 1), 0)
    bd_mask = (sub < hd) == left

    @pl.when(pl.program_id(1) == 0)
    def _():
        zero = jnp.zeros((hd, hd), F32)
        for s in seqs:
            ext_ref[s, 7:8, :] = shift_in_ref[s]
            for p in range(npair):
                top = jnp.concatenate([st_in_ref[s, 2 * p], zero], axis=1)
                bot = jnp.concatenate([zero, st_in_ref[s, 2 * p + 1]], axis=1)
                bd_ref[s, p] = jnp.concatenate([top, bot], axis=0)

    x = p_ref[...].reshape(nq * r, RWKV_PROJ)
    for s in seqs:
        ext_ref[s, 8:8 + c, :] = x[rows[s]]
    prev = [ext_ref[s, 7:7 + c, :] for s in seqs]
    prev = prev[0] if len(prev) == 1 else jnp.concatenate(prev, axis=0)
    for s in seqs:
        ext_ref[s, 0:8, :] = ext_ref[s, c:c + 8, :]
        shift_out_ref[s] = ext_ref[s, 7:8, :]

    mixed = x + (prev - x) * mu_ref[...]
    rr = mixed[:, 0:1024]
    kc = mixed[:, 1024:2048]
    vc = mixed[:, 2048:3072]
    wd = mixed[:, 3072:3136]
    ad = mixed[:, 3136:3200]
    gd = mixed[:, 3200:3328]
    w_log = -jax.nn.softplus(-(w0_ref[...] + _dot(jnp.tanh(wd), ww2_ref[...]))) - 0.5
    lw = -jnp.exp(w_log)
    a_lr = jax.nn.sigmoid(a0_ref[...] + _dot(ad, wa2_ref[...]))
    g_rw = _dot(jax.nn.sigmoid(gd), wg2_ref[...])
    kk = kc * kk_ref[...]
    kmod = kc * (1.0 + (a_lr - 1.0) * ka_ref[...])
    rkk = rr * kmod * rk_ref[...]

    m_incl = masks_ref[0]
    m_strict = masks_ref[1]
    m_blk = masks_ref[2]
    eye = masks_ref[3]
    cl = _dot_f32(tri_ref[...], lw)

    def half_sum(a):
        sl = jnp.sum(jnp.where(left, a, 0.0), axis=-1, keepdims=True)
        sr = jnp.sum(jnp.where(left, 0.0, a), axis=-1, keepdims=True)
        return jnp.where(left, sl, sr)

    def bd(a):
        ab = a.astype(BF16)
        zero = jnp.zeros_like(ab)
        return jnp.concatenate([jnp.where(left, ab, zero), jnp.where(left, zero, ab)], axis=0)

    def lp_dot(a, b):
        return jnp.dot(a.astype(BF16), bd(b), preferred_element_type=F32)

    def cat_rows(pieces):
        return pieces[0] if nb == 1 else jnp.concatenate(pieces, axis=0)

    units = [(su, p) for su in range(nq) for p in range(npair)]
    pairs = range(len(units))
    ucols = [slice(p * 2 * hd, (p + 1) * 2 * hd) for _, p in units]
    urows = [slice(su * r, (su + 1) * r) for su, _ in units]
    lrows = [slice(sl * c, (sl + 1) * c) for sl in range(nb)]
    v_p = [vc[ur, cs] for ur, cs in zip(urows, ucols)]
    kt, rt, kh, bh, e_last = [], [], [], [], []
    for ur, cs in zip(urows, ucols):
        kk_p = kk[ur, cs]
        kk_p = kk_p * lax.rsqrt(jnp.maximum(half_sum(kk_p * kk_p), 1e-24))
        cl_p = cl[ur, cs]
        e_incl = jnp.exp(cl_p)
        e_inv = jnp.exp(-cl_p)
        rt.append(rr[ur, cs] * e_incl)
        kt.append(kk_p * jnp.exp(cl_p - lw[ur, cs]))
        kh.append(kmod[ur, cs] * e_inv)
        bh.append(kk_p * a_lr[ur, cs] * e_inv)
        e_last.append([e_incl[sl * c + c - 1:sl * c + c, :] for sl in range(nb)])
    lhs = [jnp.concatenate([kt[p], rt[p]], axis=0).astype(BF16) for p in pairs]
    gk = [_dot_nt(lhs[p], bd(kh[p])) for p in pairs]
    gb = [_dot_nt(lhs[p], bd(bh[p])) for p in pairs]
    both = [[_dot_nt(jnp.concatenate([kt[k][rw], rt[k][rw]], axis=0) if nb > 1 else lhs[k],
                     bd_ref[su * nb + sl, p]) for sl, rw in enumerate(lrows)]
            for k, (su, p) in enumerate(units)]
    a_kk = [g[:r] * m_strict for g in gk]
    a_rk = [g[r:] * m_incl for g in gk]
    a_kb = [g[:r] * m_strict for g in gb]
    a_rb = [g[r:] * m_incl for g in gb]
    akv = [lp_dot(a_kk[p], v_p[p]) for p in pairs]

    n1 = [a * m_blk for a in a_kb]
    n2 = [lp_dot(n, n) for n in n1]
    p1 = [lp_dot(eye - n1[p], eye + n2[p]) for p in pairs]
    n4 = [lp_dot(n, n) for n in n2]
    p2 = [lp_dot(p1[p], eye + n4[p]) for p in pairs]
    n8 = [lp_dot(n, n) for n in n4]
    dinv = [lp_dot(p2[p], eye + n8[p]) for p in pairs]
    xm = [lp_dot(dinv[p], a_kb[p] - n1[p]) for p in pairs]
    x2 = [lp_dot(x_, x_) for x_ in xm]
    t1 = [lp_dot(eye - xm[p], eye + x2[p]) for p in pairs]
    tinv = [lp_dot(t1[p], dinv[p]) for p in pairs]

    ks = [cat_rows([b[:c] for b in both[p]]) for p in pairs]
    rs = [cat_rows([b[c:] for b in both[p]]) for p in pairs]
    u = [lp_dot(tinv[p], ks[p] + akv[p]) for p in pairs]
    y = [rs[p] + lp_dot(a_rk[p], v_p[p]) - lp_dot(a_rb[p], u[p]) for p in pairs]
    for k, (su, p) in enumerate(units):
        for sl, rw in enumerate(lrows):
            s = su * nb + sl
            upd = _dot_tn(jnp.concatenate([v_p[k][rw], -u[k][rw]], axis=0),
                          jnp.concatenate([kh[k][rw], bh[k][rw]], axis=0))
            bd_ref[s, p] = (bd_ref[s, p] + jnp.where(bd_mask, upd, 0.0)) * e_last[k][sl]
    for k, (su, p) in enumerate(units):
        ur, cs = urows[k], ucols[k]
        bonus = half_sum(rkk[ur, cs]) * v_p[k]
        yc = y[k] - half_sum(y[k]) * (1.0 / hd)
        yn = yc * lax.rsqrt(half_sum(yc * yc) * (1.0 / hd) + RWKV_GN_EPS)
        out = (yn * lng_ref[:, cs] + lnb_ref[:, cs] + bonus) * g_rw[ur, cs]
        o_ref[su * nb:(su + 1) * nb, :, cs] = out.reshape(nb, c, 2 * hd)

    @pl.when(pl.program_id(1) == pl.num_programs(1) - 1)
    def _():
        for s in seqs:
            for p in range(npair):
                blk = bd_ref[s, p]
                st_ref[s, 2 * p] = blk[:hd, :hd]
                st_ref[s, 2 * p + 1] = blk[hd:, hd:]


def _rwkv(p_rwkv, shift_all, st_all, l, bsz, t, prm, shift_acc, st_acc):
    nb, c = _seq_blocking(bsz, t, RWKV_CHUNK, SEQ_BLOCK_ROWS)
    r = nb * c
    nchunk = t // c
    assert 2 * r == LANES, "the lane-paired layout holds two (r, r) matrices side by side"
    nq = 2 if bsz % (2 * nb) == 0 else 1
    nbt = nq * nb
    _, incl, strict = _seq_masks(nb, c)
    tri = jnp.asarray(_seq_masks(nbt, c)[1].astype(np.float32))
    i = np.arange(r)
    blk = (i[:, None] // 16) == (i[None, :] // 16)
    masks = np.stack([incl, strict, blk, np.eye(r, dtype=bool)]).astype(np.float32)
    masks = jnp.asarray(np.tile(masks, (1, 1, 2)))
    st_shape = (nbt, RWKV_HEADS, RWKV_HD, RWKV_HD)
    sh_shape = (nbt, 1, RWKV_PROJ)
    if st_all is None:
        st_in = jnp.zeros((bsz,) + st_shape[1:], F32)
        shift_in = jnp.zeros((bsz,) + sh_shape[1:], F32)
        st_spec = pl.BlockSpec(st_shape, lambda i, j: (i, 0, 0, 0))
        sh_spec = pl.BlockSpec(sh_shape, lambda i, j: (i, 0, 0))
    else:
        st_in, shift_in = st_all, shift_all
        st_spec = pl.BlockSpec((None,) + st_shape, lambda i, j: (l, i, 0, 0, 0))
        sh_spec = pl.BlockSpec((None,) + sh_shape, lambda i, j: (l, i, 0, 0))

    def vec(n):
        return pl.BlockSpec((None, 1, n), lambda i, j: (l, 0, 0))

    def mat(k):
        return pl.BlockSpec((None, k, 1024), lambda i, j: (l, 0, 0))

    out, shift_new, st_new = pl.pallas_call(
        functools.partial(_rwkv_kernel, nq=nq, nb=nb, c=c),
        grid=(bsz // nbt, nchunk),
        in_specs=[
            pl.BlockSpec((nbt, c, RWKV_PROJ), lambda i, j: (i, j, 0)),
            vec(RWKV_PROJ), vec(1024), mat(64), vec(1024), mat(64), mat(128),
            vec(1024), vec(1024), vec(1024), vec(1024), vec(1024),
            pl.BlockSpec((4, r, 2 * r), lambda i, j: (0, 0, 0)),
            pl.BlockSpec((nq * r, nq * r), lambda i, j: (0, 0)),
            sh_spec, st_spec,
            pl.BlockSpec(memory_space=pl.ANY), pl.BlockSpec(memory_space=pl.ANY),
        ],
        out_specs=[
            pl.BlockSpec((nbt, c, 1024), lambda i, j: (i, j, 0)),
            pl.BlockSpec((None,) + sh_shape, lambda i, j: (l, i, 0, 0)),
            pl.BlockSpec((None,) + st_shape, lambda i, j: (l, i, 0, 0, 0)),
        ],
        out_shape=[
            jax.ShapeDtypeStruct((bsz, t, 1024), F32),
            jax.ShapeDtypeStruct(shift_acc.shape, F32),
            jax.ShapeDtypeStruct(st_acc.shape, F32),
        ],
        input_output_aliases={16: 1, 17: 2},
        scratch_shapes=[pltpu.VMEM((nbt, 8 + c, RWKV_PROJ), F32),
                        pltpu.VMEM((nbt, RWKV_HEADS // 2, 2 * RWKV_HD, 2 * RWKV_HD), F32)],
        compiler_params=_params(("parallel", "arbitrary")),
        name="rwkv7",
    )(p_rwkv.reshape(bsz, t, RWKV_PROJ), prm["rwkv_mu"], prm["rwkv_w0"], prm["rwkv_w_w2"], prm["rwkv_a0"],
      prm["rwkv_w_a2"], prm["rwkv_w_g2"], prm["rwkv_k_k"], prm["rwkv_k_a"], prm["rwkv_r_k"], prm["rwkv_lnx_g"],
      prm["rwkv_lnx_b"], masks, tri, shift_in, st_in, shift_acc, st_acc)
    return out.reshape(bsz * t, 1024), shift_new, st_new


def _merge_kernel(a_ref, b_ref, c_ref, g_ref, x_ref, wb_ref, wo_ref, qn_ref, wq_ref, o_ref, q_ref):
    merged = None
    for i, br in enumerate((a_ref, b_ref, c_ref)):
        gate = jax.nn.sigmoid(g_ref[:, i * 1024:(i + 1) * 1024])
        term = gate * jnp.dot(br[...].astype(BF16), wb_ref[i], preferred_element_type=F32)
        merged = term if merged is None else merged + term
    x = x_ref[...] + jnp.dot(merged.astype(BF16), wo_ref[...], preferred_element_type=F32)
    o_ref[...] = x
    q_ref[...] = jnp.dot(_rms(x, qn_ref[...]).astype(BF16), wq_ref[...], preferred_element_type=F32)


def _merge(out_a, out_b, out_c, gates, x, l, wb, wo, qn, wq):
    m, d = x.shape
    tm = min(m, 256)
    row = lambda i: (i, 0)
    once = pl.Buffered(1)
    return pl.pallas_call(
        _merge_kernel,
        grid=(m // tm,),
        in_specs=[
            pl.BlockSpec((tm, d), row), pl.BlockSpec((tm, d), row), pl.BlockSpec((tm, d), row),
            pl.BlockSpec((tm, GATE_PROJ), row), pl.BlockSpec((tm, d), row),
            pl.BlockSpec((None, 3, d, d), lambda i: (l, 0, 0, 0), pipeline_mode=once),
            pl.BlockSpec((None, d, d), lambda i: (l, 0, 0), pipeline_mode=once),
            pl.BlockSpec((None, 1, d), lambda i: (l, 0, 0)),
            pl.BlockSpec((None, d, d), lambda i: (l, 0, 0), pipeline_mode=once),
        ],
        out_specs=[pl.BlockSpec((tm, d), row), pl.BlockSpec((tm, d), row)],
        out_shape=[jax.ShapeDtypeStruct((m, d), F32), jax.ShapeDtypeStruct((m, d), F32)],
        compiler_params=_params(("parallel",)),
        name="merge",
    )(out_a, out_b, out_c, gates, x, wb, wo, qn, wq)


def _attn_kernel(q_ref, k_ref, v_ref, o_ref, *, nb, c):
    rows = [slice(s * c, (s + 1) * c) for s in range(nb)]
    sc = [_dot_nt(q_ref[rows[s], :], k_ref[s]) * (X_HEAD_DIM ** -0.5) for s in range(nb)]
    e = [jnp.exp(x - jnp.max(x, axis=-1, keepdims=True)) for x in sc]
    pr = [x * (1.0 / jnp.sum(x, axis=-1, keepdims=True)) for x in e]
    for s in range(nb):
        o_ref[rows[s], :] = _dot(pr[s], v_ref[s])


def _attn_cache_kernel(q_ref, k_hbm, v_hbm, o_ref, kbuf, vbuf, sem, *, l, nb, c):
    nh = pl.num_programs(1)
    step = pl.program_id(0) * nh + pl.program_id(1)
    nsteps = pl.num_programs(0) * nh
    slot = step % 2

    def copies(at_step, sl):
        blk, h = at_step // nh, at_step % nh
        src = (l, pl.ds(blk * nb, nb), slice(None), h, slice(None))
        return (pltpu.make_async_copy(k_hbm.at[src], kbuf.at[sl], sem.at[0, sl]),
                pltpu.make_async_copy(v_hbm.at[src], vbuf.at[sl], sem.at[1, sl]))

    @pl.when(step == 0)
    def _():
        for cp in copies(step, slot):
            cp.start()

    @pl.when(step + 1 < nsteps)
    def _():
        for cp in copies(step + 1, 1 - slot):
            cp.start()

    for cp in copies(step, slot):
        cp.wait()
    _attn_kernel(q_ref, kbuf.at[slot], vbuf.at[slot], o_ref, nb=nb, c=c)


def _attn_cache(q, ck, cv, l, bsz, t):
    nb, c = max(1, min(bsz, SEQ_BLOCK_ROWS // t)), t
    r = nb * c
    hd = X_HEAD_DIM
    return pl.pallas_call(
        functools.partial(_attn_cache_kernel, l=l, nb=nb, c=c),
        grid=(bsz // nb, X_HEADS),
        in_specs=[pl.BlockSpec((r, hd), lambda i, h: (i, h)),
                  pl.BlockSpec(memory_space=pl.ANY), pl.BlockSpec(memory_space=pl.ANY)],
        out_specs=pl.BlockSpec((r, hd), lambda i, h: (i, h)),
        out_shape=jax.ShapeDtypeStruct((bsz * t, 1024), F32),
        scratch_shapes=[pltpu.VMEM((2, nb, N_MEM, hd), F32), pltpu.VMEM((2, nb, N_MEM, hd), F32),
                        pltpu.SemaphoreType.DMA((2, 2))],
        compiler_params=_params(("arbitrary", "arbitrary")),
        name="xattn_cache",
    )(q, ck, cv)


def _attn(q, mk, mv, l, bsz, t):
    if t > ATTN_ROWS:
        nb, c = 1, ATTN_ROWS
    else:
        nb, c = max(1, min(bsz, SEQ_BLOCK_ROWS // t)), t
    r = nb * c
    ntile = t // c
    hd = X_HEAD_DIM
    if mk.ndim == 3:
        kv_spec = pl.BlockSpec((nb, N_MEM, hd), lambda i, j, h: (i, 0, h))
    else:
        kv_spec = pl.BlockSpec((None, nb, N_MEM, hd), lambda i, j, h: (l, i, 0, h))
    return pl.pallas_call(
        functools.partial(_attn_kernel, nb=nb, c=c),
        grid=(bsz // nb, ntile, X_HEADS),
        in_specs=[pl.BlockSpec((r, hd), lambda i, j, h: (i * ntile + j, h)), kv_spec, kv_spec],
        out_specs=pl.BlockSpec((r, hd), lambda i, j, h: (i * ntile + j, h)),
        out_shape=jax.ShapeDtypeStruct((bsz * t, 1024), F32),
        compiler_params=_params(("parallel", "arbitrary", "arbitrary")),
        name="xattn_core",
    )(q, mk, mv)


def _trunk(x, prm, states, mem_k, mem_v, bsz, t, offset):
    acc_ret = jnp.zeros((DEPTH, bsz, RET_HEADS, RET_DK, RET_DK), F32)
    acc_ssm = jnp.zeros((DEPTH, bsz, SSM_HEADS, SSM_HEADDIM, SSM_STATE), F32)
    acc_conv = jnp.zeros((DEPTH, bsz, CONV_W - 1, CONV_DIM), F32)
    acc_rwkv = jnp.zeros((DEPTH, bsz, RWKV_HEADS, RWKV_HD, RWKV_HD), F32)
    acc_shift = jnp.zeros((DEPTH, bsz, 1, RWKV_PROJ), F32)
    for l in range(DEPTH):
        x = _ffn(x, prm["ffa_norm"], l, prm["ffa_w1"], prm["ffa_w3"], prm["ffa_w2"])
        p_ret = _norm_matmul(x, prm["mix_norm"], l, prm["w_ret"], 0, RET_PROJ)
        p_ssm = _norm_matmul(x, prm["mix_norm"], l, prm["w_ssm"], 0, SSM_PROJ_PAD)
        p_rwkv = _norm_matmul(x, prm["mix_norm"], l, prm["w_rwkv"], 0, RWKV_PROJ)
        p_gate = _norm_matmul(x, prm["mix_norm"], l, prm["w_gate"], 0, GATE_PROJ)
        if states is None:
            s_ret = s_ssm = s_conv = s_rwkv = s_shift = None
        else:
            s_ret, s_ssm, s_conv, s_rwkv, s_shift = states
        out_a, acc_ret = _retention(p_ret, s_ret, l, bsz, t, offset, acc_ret)
        out_b, acc_conv, acc_ssm = _ssd(p_ssm, s_conv, s_ssm, l, bsz, t, prm, acc_conv, acc_ssm)
        out_c, acc_shift, acc_rwkv = _rwkv(p_rwkv, s_shift, s_rwkv, l, bsz, t, prm, acc_shift, acc_rwkv)
        x, q = _merge(out_a, out_b, out_c, p_gate, x, l, prm["w_branch"], prm["w_out"],
                      prm["xattn_norm"], prm["xattn_wq"])
        if isinstance(mem_k, (list, tuple)):
            att = _attn(q, mem_k[l], mem_v[l], l, bsz, t)
        else:
            att = _attn_cache(q, mem_k, mem_v, l, bsz, t)
        x = _ffn(x, prm["ffb_norm"], l, prm["ffb_w1"], prm["ffb_w3"], prm["ffb_w2"], att=att, wo=prm["xattn_wo"],
                 final_g=prm["final_norm"] if l == DEPTH - 1 else None)
    return x, (acc_ret, acc_ssm, acc_conv, acc_rwkv, acc_shift)


def _prep_params(raw):
    p = {}
    for name in ("ffa_w1", "ffa_w3", "ffa_w2", "ffb_w1", "ffb_w3", "ffb_w2", "w_branch", "w_out",
                 "xattn_wq", "xattn_wk", "xattn_wv", "xattn_wo", "rwkv_w_w2", "rwkv_w_a2", "rwkv_w_g2"):
        p[name] = raw[name].astype(BF16)
    w_in = raw["w_in"]
    c0, c1, c2 = RET_PROJ, RET_PROJ + SSM_PROJ, RET_PROJ + SSM_PROJ + RWKV_PROJ
    p["w_ret"] = w_in[:, :, :c0].astype(BF16)
    p["w_ssm"] = jnp.pad(w_in[:, :, c0:c1], ((0, 0), (0, 0), (0, SSM_PROJ_PAD - SSM_PROJ))).astype(BF16)
    p["w_rwkv"] = w_in[:, :, c1:c2].astype(BF16)
    p["w_gate"] = w_in[:, :, c2:].astype(BF16)
    for name in ("ffa_norm", "mix_norm", "ssm_conv_b", "ssm_norm", "rwkv_mu", "rwkv_w0", "rwkv_a0",
                 "rwkv_k_k", "rwkv_k_a", "rwkv_lnx_g", "rwkv_lnx_b", "xattn_norm", "mem_norm", "ffb_norm"):
        p[name] = raw[name][:, None, :]
    p["rwkv_r_k"] = raw["rwkv_r_k"].reshape(DEPTH, 1, 1024)
    pad = ((0, 0), (0, LANES - SSM_HEADS))
    p["ssm_dt_bias"] = jnp.pad(raw["ssm_dt_bias"], pad)[:, None, :]
    p["ssm_a_log"] = jnp.pad(raw["ssm_a_log"], pad)[:, None, :]
    p["ssm_d"] = jnp.repeat(raw["ssm_d"], SSM_HEADDIM, axis=-1)[:, None, :]
    p["ssm_conv_w"] = raw["ssm_conv_w"]
    p["final_norm"] = raw["final_norm"][None, :]
    return p


def _run(x_prompt, x_sample, mem_prompt, states, cache_mem_k, cache_mem_v, raw):
    prm = _prep_params(raw)
    b, t, d = x_prompt.shape
    db, dt_, _ = x_sample.shape
    n_mem = mem_prompt.shape[1]
    mem2 = mem_prompt.reshape(b * n_mem, d)
    mk = [_norm_matmul(mem2, prm["mem_norm"], l, prm["xattn_wk"], 0, 1024) for l in range(DEPTH)]
    mv = [_norm_matmul(mem2, prm["mem_norm"], l, prm["xattn_wv"], 0, 1024) for l in range(DEPTH)]
    mk3 = [a.reshape(b, n_mem, d) for a in mk]
    mv3 = [a.reshape(b, n_mem, d) for a in mv]
    p_mem_k = jnp.stack(mk).reshape(DEPTH, b, n_mem, X_HEADS, X_HEAD_DIM)
    p_mem_v = jnp.stack(mv).reshape(DEPTH, b, n_mem, X_HEADS, X_HEAD_DIM)
    y_p, st_p = _trunk(x_prompt.reshape(b * t, d), prm, None, mk3, mv3, b, t, 0)

    y_s, st_s = _trunk(x_sample.reshape(db * dt_, d), prm, states, cache_mem_k, cache_mem_v, db, dt_, PAST_LEN)
    return (y_p.reshape(b, t, d), y_s.reshape(db, dt_, d)) + st_p + (p_mem_k, p_mem_v) + st_s


def kernel(x_prompt, x_sample, mem_prompt, state_ret, state_ssm, state_conv, state_rwkv, state_shift, cache_mem_k, cache_mem_v, ffa_norm, ffa_w1, ffa_w3, ffa_w2, mix_norm, w_in, ssm_conv_w, ssm_conv_b, ssm_dt_bias, ssm_a_log, ssm_d, ssm_norm, rwkv_mu, rwkv_w0, rwkv_w_w2, rwkv_a0, rwkv_w_a2, rwkv_w_g2, rwkv_k_k, rwkv_k_a, rwkv_r_k, rwkv_lnx_g, rwkv_lnx_b, w_branch, w_out, xattn_norm, mem_norm, xattn_wq, xattn_wk, xattn_wv, xattn_wo, ffb_norm, ffb_w1, ffb_w3, ffb_w2, final_norm):
    raw = dict(ffa_norm=ffa_norm, ffa_w1=ffa_w1, ffa_w3=ffa_w3, ffa_w2=ffa_w2, mix_norm=mix_norm, w_in=w_in,
               ssm_conv_w=ssm_conv_w, ssm_conv_b=ssm_conv_b, ssm_dt_bias=ssm_dt_bias, ssm_a_log=ssm_a_log,
               ssm_d=ssm_d, ssm_norm=ssm_norm, rwkv_mu=rwkv_mu, rwkv_w0=rwkv_w0, rwkv_w_w2=rwkv_w_w2,
               rwkv_a0=rwkv_a0, rwkv_w_a2=rwkv_w_a2, rwkv_w_g2=rwkv_w_g2, rwkv_k_k=rwkv_k_k,
               rwkv_k_a=rwkv_k_a, rwkv_r_k=rwkv_r_k, rwkv_lnx_g=rwkv_lnx_g, rwkv_lnx_b=rwkv_lnx_b,
               w_branch=w_branch, w_out=w_out, xattn_norm=xattn_norm, mem_norm=mem_norm,
               xattn_wq=xattn_wq, xattn_wk=xattn_wk, xattn_wv=xattn_wv, xattn_wo=xattn_wo,
               ffb_norm=ffb_norm, ffb_w1=ffb_w1, ffb_w3=ffb_w3, ffb_w2=ffb_w2, final_norm=final_norm)
    states = (state_ret, state_ssm, state_conv, state_rwkv, state_shift)
    return _run(x_prompt, x_sample, mem_prompt, states, cache_mem_k, cache_mem_v, raw)
```

```python
import functools
import math

import numpy as np
import jax
import jax.numpy as jnp
from jax import lax
from jax.experimental import pallas as pl
from jax.experimental.pallas import tpu as pltpu

F32 = jnp.float32
BF16 = jnp.bfloat16

D_MODEL = 1024
DEPTH = 4
PAST_LEN = 16384
D_FF = 2816
EPS = 1e-6
RET_HEADS = 8
RET_DK = 128
RET_GN_EPS = 1e-5
ROPE_BASE = 10000.0
SSM_HEADS = 16
SSM_HEADDIM = 64
SSM_GROUPS = 2
SSM_STATE = 128
CONV_W = 4
CONV_DIM = 1536
RWKV_HEADS = 16
RWKV_HD = 64
RWKV_PROJ = 3328
RWKV_GN_EPS = 64e-5
N_MEM = 256
X_HEADS = 4
X_HEAD_DIM = 256
RET_PROJ = 4096
SSM_PROJ = 2576
SSM_PROJ_PAD = 2688
GATE_PROJ = 3072
LANES = 128
VMEM_LIMIT = 56 * 1024 * 1024

RET_CHUNK = 128
SSD_CHUNK = 128
RWKV_CHUNK = 64
ATTN_ROWS = 2048
SEQ_BLOCK_ROWS = 64


def _params(sem):
    return pltpu.CompilerParams(dimension_semantics=sem, vmem_limit_bytes=VMEM_LIMIT)


def _rms(x, g):
    return x * lax.rsqrt(jnp.mean(x * x, axis=-1, keepdims=True) + EPS) * g


def _head_norm(x, eps):
    mu = jnp.mean(x, axis=-1, keepdims=True)
    xc = x - mu
    return xc * lax.rsqrt(jnp.mean(xc * xc, axis=-1, keepdims=True) + eps)


def _silu(x):
    return x * jax.nn.sigmoid(x)


def _dot(a, b):
    return jnp.dot(a.astype(BF16), b.astype(BF16), preferred_element_type=F32)


def _dot_nt(a, b):
    return lax.dot_general(a.astype(BF16), b.astype(BF16), (((1,), (1,)), ((), ())),
                           preferred_element_type=F32)


def _dot_tn(a, b):
    return lax.dot_general(a.astype(BF16), b.astype(BF16), (((0,), (0,)), ((), ())),
                           preferred_element_type=F32)


def _dot_f32(a, b):
    return jnp.dot(a, b, precision=lax.Precision.HIGHEST, preferred_element_type=F32)


def _seq_blocking(bsz, t, chunk_max, block_rows):
    if t > chunk_max:
        assert t % chunk_max == 0
        return 1, chunk_max
    nb = max(1, min(bsz, block_rows // t))
    assert bsz % nb == 0
    return nb, t


def _seq_masks(nb, c):
    r = nb * c
    i = np.arange(r)
    same = (i[:, None] // c) == (i[None, :] // c)
    incl = same & (i[:, None] >= i[None, :])
    strict = same & (i[:, None] > i[None, :])
    return same, incl, strict


def _nm_kernel(x_ref, g_ref, w_ref, o_ref, h_ref):
    @pl.when(pl.program_id(1) == 0)
    def _():
        h_ref[...] = _rms(x_ref[...], g_ref[...]).astype(BF16)

    o_ref[...] = jnp.dot(h_ref[...], w_ref[...], preferred_element_type=F32)


def _norm_matmul(x, g, l, w, col0, ncols):
    m, d = x.shape
    tn = ncols
    tm = min(m, 512)
    assert col0 % tn == 0 and m % tm == 0
    cb = col0 // tn
    return pl.pallas_call(
        _nm_kernel,
        grid=(m // tm, ncols // tn),
        in_specs=[
            pl.BlockSpec((tm, d), lambda i, j: (i, 0)),
            pl.BlockSpec((None, 1, d), lambda i, j: (l, 0, 0)),
            pl.BlockSpec((None, d, tn), lambda i, j: (l, 0, cb + j)),
        ],
        out_specs=pl.BlockSpec((tm, tn), lambda i, j: (i, j)),
        out_shape=jax.ShapeDtypeStruct((m, ncols), F32),
        scratch_shapes=[pltpu.VMEM((tm, d), BF16)],
        compiler_params=_params(("parallel", "arbitrary")),
        name="norm_matmul",
    )(x, g, w)


def _ffn_kernel(*refs, pre, final):
    x_ref, g_ref, w1_ref, w3_ref, w2_ref = refs[:5]
    extra = list(refs[5:-1])
    o_ref = refs[-1]
    x = x_ref[...]
    if pre:
        att_ref, wo_ref = extra[:2]
        extra = extra[2:]
        x = x + jnp.dot(att_ref[...].astype(BF16), wo_ref[...], preferred_element_type=F32)
    h = _rms(x, g_ref[...]).astype(BF16)
    a = jnp.dot(h, w1_ref[...], preferred_element_type=F32)
    b = jnp.dot(h, w3_ref[...], preferred_element_type=F32)
    u = (_silu(a) * b).astype(BF16)
    y = x + 0.5 * jnp.dot(u, w2_ref[...], preferred_element_type=F32)
    o_ref[...] = _rms(y, extra[0][...]) if final else y


def _ffn(x, g, l, w1, w3, w2, att=None, wo=None, final_g=None):
    m, d = x.shape
    f = w1.shape[-1]
    tm = min(m, 512)
    once = pl.Buffered(1)
    row = lambda i: (i, 0)
    in_specs = [
        pl.BlockSpec((tm, d), row),
        pl.BlockSpec((None, 1, d), lambda i: (l, 0, 0)),
        pl.BlockSpec((None, d, f), lambda i: (l, 0, 0), pipeline_mode=once),
        pl.BlockSpec((None, d, f), lambda i: (l, 0, 0), pipeline_mode=once),
        pl.BlockSpec((None, f, d), lambda i: (l, 0, 0), pipeline_mode=once),
    ]
    args = [x, g, w1, w3, w2]
    if att is not None:
        in_specs += [pl.BlockSpec((tm, d), row), pl.BlockSpec((None, d, d), lambda i: (l, 0, 0), pipeline_mode=once)]
        args += [att, wo]
    if final_g is not None:
        in_specs.append(pl.BlockSpec((1, d), lambda i: (0, 0)))
        args.append(final_g)
    return pl.pallas_call(
        functools.partial(_ffn_kernel, pre=att is not None, final=final_g is not None),
        grid=(m // tm,),
        in_specs=in_specs,
        out_specs=pl.BlockSpec((tm, d), row),
        out_shape=jax.ShapeDtypeStruct((m, d), F32),
        compiler_params=_params(("parallel",)),
        name="ffn",
    )(*args)


def _ret_kernel(p_ref, cos_ref, sin_ref, dmat_ref, qdec_ref, kdec_ref, st_in_ref, acc_ref,
                o_ref, st_ref, *, nq, nb, c, sdec):
    del acc_ref

    @pl.when(pl.program_id(1) == 0)
    def _():
        st_ref[...] = st_in_ref[...]

    cosf = cos_ref[...]
    sinf = sin_ref[...]
    hd = RET_DK
    r = nb * c
    units = [(su, h) for su in range(nq) for h in range(RET_HEADS)]
    lrows = [slice(sl * c, (sl + 1) * c) for sl in range(nb)]

    def slab(su, col0):
        return p_ref[su * nb:(su + 1) * nb, :, col0:col0 + hd].reshape(r, hd)

    qr, kr, v = [], [], []
    for su, h in units:
        q = slab(su, h * hd)
        k = slab(su, 1024 + h * hd)
        qr.append(((q * cosf + pltpu.roll(q, hd // 2, 1) * sinf) * (RET_DK ** -0.5)).astype(BF16))
        kr.append(k * cosf + pltpu.roll(k, hd // 2, 1) * sinf)
        v.append(slab(su, 2048 + h * hd).astype(BF16))
    scores = [_dot_nt(qr[k], kr[k]) * dmat_ref[h] for k, (_, h) in enumerate(units)]
    inter = [[_dot(qr[k][rw], st_ref[su * nb + sl, h]) for sl, rw in enumerate(lrows)]
             for k, (su, h) in enumerate(units)]
    o = [_dot(scores[k], v[k]) for k in range(len(units))]
    for k, (su, h) in enumerate(units):
        kd = kr[k] * kdec_ref[h]
        qdec = qdec_ref[h]
        for sl, rw in enumerate(lrows):
            s = su * nb + sl
            o_s = o[k][rw] + inter[k][sl] * qdec[rw]
            g = p_ref[s, :, 3072 + h * hd:3072 + (h + 1) * hd]
            o_ref[s, :, h * hd:(h + 1) * hd] = _silu(g) * _head_norm(o_s, RET_GN_EPS)
            st_ref[s, h] = st_ref[s, h] * sdec[h] + _dot_tn(kd[rw], v[k][rw])


def _retention(p_ret, st_all, l, bsz, t, offset, acc):
    nb, c = _seq_blocking(bsz, t, RET_CHUNK, SEQ_BLOCK_ROWS)
    r = nb * c
    nchunk = t // c
    heads = np.arange(RET_HEADS, dtype=np.float64)
    log_g = np.log1p(-np.exp2(-5.0 - heads))
    idx = np.arange(r) % c
    _, incl, _ = _seq_masks(nb, c)
    diff = (idx[:, None] - idx[None, :]).astype(np.float64)
    dmat = np.where(incl[None], np.exp(np.maximum(diff, 0.0)[None] * log_g[:, None, None]), 0.0)
    qdec = np.exp((idx + 1.0)[None, :] * log_g[:, None])
    kdec = np.exp((c - 1.0 - idx)[None, :] * log_g[:, None])
    qdec = np.broadcast_to(qdec[:, :, None], (RET_HEADS, r, RET_DK))
    kdec = np.broadcast_to(kdec[:, :, None], (RET_HEADS, r, RET_DK))
    sdec = tuple(float(x) for x in np.exp(c * log_g))

    half = RET_DK // 2
    freqs = ROPE_BASE ** (-jnp.arange(half, dtype=F32) / half)
    pos = jnp.float32(offset) + jnp.arange(t, dtype=F32)
    ang = pos[:, None] * freqs[None, :]
    cos = jnp.cos(ang)
    sin = jnp.sin(ang)
    cosf = jnp.tile(jnp.concatenate([cos, cos], axis=-1), (nb, 1))
    sinf = jnp.tile(jnp.concatenate([-sin, sin], axis=-1), (nb, 1))

    nq = 2 if (nb == 1 and bsz % 2 == 0) else 1
    nbt = nq * nb
    st_shape = (nbt, RET_HEADS, RET_DK, RET_DK)
    if st_all is None:
        st_in = jnp.zeros((bsz,) + st_shape[1:], F32)
        st_spec = pl.BlockSpec(st_shape, lambda i, j: (i, 0, 0, 0))
    else:
        st_in = st_all
        st_spec = pl.BlockSpec((None,) + st_shape, lambda i, j: (l, i, 0, 0, 0))
    out, st_new = pl.pallas_call(
        functools.partial(_ret_kernel, nq=nq, nb=nb, c=c, sdec=sdec),
        grid=(bsz // nbt, nchunk),
        in_specs=[
            pl.BlockSpec((nbt, c, RET_PROJ), lambda i, j: (i, j, 0)),
            pl.BlockSpec((r, RET_DK), lambda i, j: (j, 0)),
            pl.BlockSpec((r, RET_DK), lambda i, j: (j, 0)),
            pl.BlockSpec((RET_HEADS, r, r), lambda i, j: (0, 0, 0)),
            pl.BlockSpec((RET_HEADS, r, RET_DK), lambda i, j: (0, 0, 0)),
            pl.BlockSpec((RET_HEADS, r, RET_DK), lambda i, j: (0, 0, 0)),
            st_spec,
            pl.BlockSpec(memory_space=pl.ANY),
        ],
        out_specs=[
            pl.BlockSpec((nbt, c, 1024), lambda i, j: (i, j, 0)),
            pl.BlockSpec((None,) + st_shape, lambda i, j: (l, i, 0, 0, 0)),
        ],
        out_shape=[
            jax.ShapeDtypeStruct((bsz, t, 1024), F32),
            jax.ShapeDtypeStruct(acc.shape, F32),
        ],
        input_output_aliases={7: 1},
        compiler_params=_params(("parallel", "arbitrary")),
        name="retention",
    )(p_ret.reshape(bsz, t, RET_PROJ), cosf, sinf, jnp.asarray(dmat, F32), jnp.asarray(qdec, F32),
      jnp.asarray(kdec, F32), st_in, acc)
    return out.reshape(bsz * t, 1024), st_new


def _ssd_kernel(p_ref, cw_ref, cb_ref, dtb_ref, alog_ref, dfull_ref, norm_ref, tri_ref, mask2_ref,
                e64_ref, er_ref, conv_in_ref, st_in_ref, conv_acc_ref, st_acc_ref,
                o_ref, conv_out_ref, st_ref, ext_ref, stt_ref, *, nb, c):
    del conv_acc_ref, st_acc_ref
    r = nb * c
    hp = SSM_HEADDIM
    npair = SSM_HEADS // 2
    pairs = range(npair)
    seqs = range(nb)
    rows = [slice(s * c, (s + 1) * c) for s in seqs]
    cols = [slice(q * 2 * hp, (q + 1) * 2 * hp) for q in pairs]
    left = lax.broadcasted_iota(jnp.int32, (1, 2 * hp), 1) < hp

    @pl.when(pl.program_id(1) == 0)
    def _():
        for s in seqs:
            ext_ref[s, 5:8, :] = conv_in_ref[s]
            for q in pairs:
                both = jnp.concatenate([st_in_ref[s, 2 * q], st_in_ref[s, 2 * q + 1]], axis=0)
                stt_ref[s, q] = both.T

    for s in range(nb):
        ext_ref[s, 8:8 + c, :] = p_ref[s * c:(s + 1) * c, 1024:1024 + CONV_DIM]
    pieces = []
    for s in range(nb):
        acc = cb_ref[...] + ext_ref[s, 5:5 + c, :] * cw_ref[0:1, :]
        for j in range(1, CONV_W):
            acc = acc + ext_ref[s, 5 + j:5 + j + c, :] * cw_ref[j:j + 1, :]
        pieces.append(acc)
    for s in range(nb):
        conv_out_ref[s] = ext_ref[s, c + 5:c + 8, :]
        ext_ref[s, 0:8, :] = ext_ref[s, c:c + 8, :]
    xbc = _silu(pieces[0] if nb == 1 else jnp.concatenate(pieces, axis=0))
    xs = xbc[:, :1024]
    bm = xbc[:, 1024:1024 + SSM_GROUPS * SSM_STATE]
    cm = xbc[:, 1024 + SSM_GROUPS * SSM_STATE:]

    dt = jax.nn.softplus(p_ref[:, 1024 + CONV_DIM:] + dtb_ref[...])
    a = -jnp.exp(alog_ref[...])
    tri = tri_ref[...]
    cum = _dot_f32(tri, dt * a)
    cum_t = cum.T

    def expand(v, e_ref):
        hi = v.astype(BF16)
        r1 = v - hi.astype(F32)
        mid = r1.astype(BF16)
        lo = (r1 - mid.astype(F32)).astype(BF16)
        return jnp.dot(jnp.concatenate([hi, mid, lo], axis=1), e_ref[...], preferred_element_type=F32)

    def bd(v):
        vb = v.astype(BF16)
        zero = jnp.zeros_like(vb)
        return jnp.concatenate([jnp.where(left, vb, zero), jnp.where(left, zero, vb)], axis=0)

    def cat_rows(pieces):
        return pieces[0] if nb == 1 else jnp.concatenate(pieces, axis=0)

    cum_full = expand(cum, e64_ref)
    cum_wide = cum_full if r == hp else expand(cum, er_ref)
    xdt = xs * expand(dt, e64_ref)
    last = [cum_full[s * c + c - 1:s * c + c, :] for s in seqs]
    xw = cat_rows([xdt[rw] * jnp.exp(last[s] - cum_full[rw]) for s, rw in enumerate(rows)])
    eci = jnp.exp(cum_full)
    mask2 = mask2_ref[...] > 0.5

    hg = SSM_HEADS // SSM_GROUPS
    bm_g = [bm[:, g * SSM_STATE:(g + 1) * SSM_STATE] for g in range(SSM_GROUPS)]
    cm_g = [cm[:, g * SSM_STATE:(g + 1) * SSM_STATE] for g in range(SSM_GROUPS)]
    cb2 = []
    for g in range(SSM_GROUPS):
        cb = _dot_nt(cm_g[g], bm_g[g])
        cb2.append(jnp.concatenate([cb, cb], axis=1))
    grp = [2 * q // hg for q in pairs]
    inter = [cat_rows([_dot(cm_g[grp[q]][rw], stt_ref[s, q]) for s, rw in enumerate(rows)]) for q in pairs]
    sc = []
    for q in pairs:
        row2 = jnp.concatenate([cum_t[2 * q:2 * q + 1, :], cum_t[2 * q + 1:2 * q + 2, :]], axis=1)
        seg = cum_wide[:, q * 2 * r:(q + 1) * 2 * r] - row2
        sc.append(cb2[grp[q]] * jnp.where(mask2, jnp.exp(seg), 0.0))
    y_p = [jnp.dot(sc[q].astype(BF16), bd(xdt[:, cols[q]]), preferred_element_type=F32)
           + inter[q] * eci[:, cols[q]] for q in pairs]
    for q in pairs:
        for s, rw in enumerate(rows):
            upd = _dot_tn(bm_g[grp[q]][rw], xw[rw, cols[q]])
            stt_ref[s, q] = stt_ref[s, q] * jnp.exp(last[s][:, cols[q]]) + upd
    y = jnp.concatenate(y_p, axis=1) + dfull_ref[...] * xs
    z = p_ref[:, :1024]
    o_ref[...] = _rms(y * _silu(z), norm_ref[...])

    @pl.when(pl.program_id(1) == pl.num_programs(1) - 1)
    def _():
        for s in seqs:
            for q in pairs:
                both = stt_ref[s, q].T
                st_ref[s, 2 * q] = both[:hp]
                st_ref[s, 2 * q + 1] = both[hp:]


def _ssd(p_ssm, conv_all, st_all, l, bsz, t, prm, conv_acc, st_acc):
    nb, c = _seq_blocking(bsz, t, SSD_CHUNK, SEQ_BLOCK_ROWS)
    r = nb * c
    nchunk = t // c
    _, incl, _ = _seq_masks(nb, c)
    tri = jnp.asarray(incl.astype(np.float32))
    mask2 = jnp.asarray(np.tile(incl.astype(np.float32), (1, 2)))

    def expander(width):
        e = np.zeros((LANES, SSM_HEADS * width), np.float32)
        for h in range(SSM_HEADS):
            e[h, h * width:(h + 1) * width] = 1.0
        return jnp.asarray(np.concatenate([e, e, e], axis=0), BF16)

    e64 = expander(SSM_HEADDIM)
    e_r = expander(r)
    st_shape = (nb, SSM_HEADS, SSM_HEADDIM, SSM_STATE)
    cv_shape = (nb, CONV_W - 1, CONV_DIM)
    if st_all is None:
        st_in = jnp.zeros((bsz,) + st_shape[1:], F32)
        conv_in = jnp.zeros((bsz,) + cv_shape[1:], F32)
        st_spec = pl.BlockSpec(st_shape, lambda i, j: (i, 0, 0, 0))
        cv_spec = pl.BlockSpec(cv_shape, lambda i, j: (i, 0, 0))
    else:
        st_in, conv_in = st_all, conv_all
        st_spec = pl.BlockSpec((None,) + st_shape, lambda i, j: (l, i, 0, 0, 0))
        cv_spec = pl.BlockSpec((None,) + cv_shape, lambda i, j: (l, i, 0, 0))
    m = bsz * t

    def vec(n):
        return pl.BlockSpec((None, 1, n), lambda i, j: (l, 0, 0))

    return pl.pallas_call(
        functools.partial(_ssd_kernel, nb=nb, c=c),
        grid=(bsz // nb, nchunk),
        in_specs=[
            pl.BlockSpec((r, SSM_PROJ_PAD), lambda i, j: (i * nchunk + j, 0)),
            pl.BlockSpec((None, CONV_W, CONV_DIM), lambda i, j: (l, 0, 0)),
            vec(CONV_DIM), vec(LANES), vec(LANES), vec(1024), vec(1024),
            pl.BlockSpec((r, r), lambda i, j: (0, 0)),
            pl.BlockSpec((r, 2 * r), lambda i, j: (0, 0)),
            pl.BlockSpec(e64.shape, lambda i, j: (0, 0)),
            pl.BlockSpec(e_r.shape, lambda i, j: (0, 0)),
            cv_spec, st_spec,
            pl.BlockSpec(memory_space=pl.ANY), pl.BlockSpec(memory_space=pl.ANY),
        ],
        out_specs=[
            pl.BlockSpec((r, 1024), lambda i, j: (i * nchunk + j, 0)),
            pl.BlockSpec((None,) + cv_shape, lambda i, j: (l, i, 0, 0)),
            pl.BlockSpec((None,) + st_shape, lambda i, j: (l, i, 0, 0, 0)),
        ],
        out_shape=[
            jax.ShapeDtypeStruct((m, 1024), F32),
            jax.ShapeDtypeStruct(conv_acc.shape, F32),
            jax.ShapeDtypeStruct(st_acc.shape, F32),
        ],
        input_output_aliases={13: 1, 14: 2},
        scratch_shapes=[pltpu.VMEM((nb, 8 + c, CONV_DIM), F32),
                        pltpu.VMEM((nb, SSM_HEADS // 2, SSM_STATE, 2 * SSM_HEADDIM), F32)],
        compiler_params=_params(("parallel", "arbitrary")),
        name="ssd",
    )(p_ssm, prm["ssm_conv_w"], prm["ssm_conv_b"], prm["ssm_dt_bias"], prm["ssm_a_log"],
      prm["ssm_d"], prm["ssm_norm"], tri, mask2, e64, e_r, conv_in, st_in, conv_acc, st_acc)


def _rwkv_kernel(p_ref, mu_ref, w0_ref, ww2_ref, a0_ref, wa2_ref, wg2_ref, kk_ref, ka_ref,
                 rk_ref, lng_ref, lnb_ref, masks_ref, tri_ref, shift_in_ref, st_in_ref, shift_acc_ref,
                 st_acc_ref, o_ref, shift_out_ref, st_ref, ext_ref, bd_ref, *, nq, nb, c):
    del shift_acc_ref, st_acc_ref
    r = nb * c
    hd = RWKV_HD
    npair = RWKV_HEADS // 2
    seqs = range(nq * nb)
    rows = [slice(s * c, (s + 1) * c) for s in seqs]
    lane = lax.broadcasted_iota(jnp.int32, (1, 2 * hd), 1)
    left = lane < hd
    sub = lax.broadcasted_iota(jnp.int32, (2 * hd, 1), 0)
    bd_mask = (sub < hd) == left

    @pl.when(pl.program_id(1) == 0)
    def _():
        zero = jnp.zeros((hd, hd), F32)
        for s in seqs:
            ext_ref[s, 7:8, :] = shift_in_ref[s]
            for p in range(npair):
                top = jnp.concatenate([st_in_ref[s, 2 * p], zero], axis=1)
                bot = jnp.concatenate([zero, st_in_ref[s, 2 * p + 1]], axis=1)
                bd_ref[s, p] = jnp.concatenate([top, bot], axis=0)

    x = p_ref[...].reshape(nq * r, RWKV_PROJ)
    for s in seqs:
        ext_ref[s, 8:8 + c, :] = x[rows[s]]
    prev = [ext_ref[s, 7:7 + c, :] for s in seqs]
    prev = prev[0] if len(prev) == 1 else jnp.concatenate(prev, axis=0)
    for s in seqs:
        ext_ref[s, 0:8, :] = ext_ref[s, c:c + 8, :]
        shift_out_ref[s] = ext_ref[s, 7:8, :]

    mixed = x + (prev - x) * mu_ref[...]
    rr = mixed[:, 0:1024]
    kc = mixed[:, 1024:2048]
    vc = mixed[:, 2048:3072]
    wd = mixed[:, 3072:3136]
    ad = mixed[:, 3136:3200]
    gd = mixed[:, 3200:3328]
    w_log = -jax.nn.softplus(-(w0_ref[...] + _dot(jnp.tanh(wd), ww2_ref[...]))) - 0.5
    lw = -jnp.exp(w_log)
    a_lr = jax.nn.sigmoid(a0_ref[...] + _dot(ad, wa2_ref[...]))
    g_rw = _dot(jax.nn.sigmoid(gd), wg2_ref[...])
    kk = kc * kk_ref[...]
    kmod = kc * (1.0 + (a_lr - 1.0) * ka_ref[...])
    rkk = rr * kmod * rk_ref[...]

    m_incl = masks_ref[0]
    m_strict = masks_ref[1]
    m_blk = masks_ref[2]
    eye = masks_ref[3]
    cl = _dot_f32(tri_ref[...], lw)

    def half_sum(a):
        sl = jnp.sum(jnp.where(left, a, 0.0), axis=-1, keepdims=True)
        sr = jnp.sum(jnp.where(left, 0.0, a), axis=-1, keepdims=True)
        return jnp.where(left, sl, sr)

    def bd(a):
        ab = a.astype(BF16)
        zero = jnp.zeros_like(ab)
        return jnp.concatenate([jnp.where(left, ab, zero), jnp.where(left, zero, ab)], axis=0)

    def lp_dot(a, b):
        return jnp.dot(a.astype(BF16), bd(b), preferred_element_type=F32)

    def cat_rows(pieces):
        return pieces[0] if nb == 1 else jnp.concatenate(pieces, axis=0)

    units = [(su, p) for su in range(nq) for p in range(npair)]
    pairs = range(len(units))
    ucols = [slice(p * 2 * hd, (p + 1) * 2 * hd) for _, p in units]
    urows = [slice(su * r, (su + 1) * r) for su, _ in units]
    lrows = [slice(sl * c, (sl + 1) * c) for sl in range(nb)]
    v_p = [vc[ur, cs] for ur, cs in zip(urows, ucols)]
    kt, rt, kh, bh, e_last = [], [], [], [], []
    for ur, cs in zip(urows, ucols):
        kk_p = kk[ur, cs]
        kk_p = kk_p * lax.rsqrt(jnp.maximum(half_sum(kk_p * kk_p), 1e-24))
        cl_p = cl[ur, cs]
        e_incl = jnp.exp(cl_p)
        e_inv = jnp.exp(-cl_p)
        rt.append(rr[ur, cs] * e_incl)
        kt.append(kk_p * jnp.exp(cl_p - lw[ur, cs]))
        kh.append(kmod[ur, cs] * e_inv)
        bh.append(kk_p * a_lr[ur, cs] * e_inv)
        e_last.append([e_incl[sl * c + c - 1:sl * c + c, :] for sl in range(nb)])
    lhs = [jnp.concatenate([kt[p], rt[p]], axis=0).astype(BF16) for p in pairs]
    gk = [_dot_nt(lhs[p], bd(kh[p])) for p in pairs]
    gb = [_dot_nt(lhs[p], bd(bh[p])) for p in pairs]
    both = [[_dot_nt(jnp.concatenate([kt[k][rw], rt[k][rw]], axis=0) if nb > 1 else lhs[k],
                     bd_ref[su * nb + sl, p]) for sl, rw in enumerate(lrows)]
            for k, (su, p) in enumerate(units)]
    a_kk = [g[:r] * m_strict for g in gk]
    a_rk = [g[r:] * m_incl for g in gk]
    a_kb = [g[:r] * m_strict for g in gb]
    a_rb = [g[r:] * m_incl for g in gb]
    akv = [lp_dot(a_kk[p], v_p[p]) for p in pairs]

    n1 = [a * m_blk for a in a_kb]
    n2 = [lp_dot(n, n) for n in n1]
    p1 = [lp_dot(eye - n1[p], eye + n2[p]) for p in pairs]
    n4 = [lp_dot(n, n) for n in n2]
    p2 = [lp_dot(p1[p], eye + n4[p]) for p in pairs]
    n8 = [lp_dot(n, n) for n in n4]
    dinv = [lp_dot(p2[p], eye + n8[p]) for p in pairs]
    xm = [lp_dot(dinv[p], a_kb[p] - n1[p]) for p in pairs]
    x2 = [lp_dot(x_, x_) for x_ in xm]
    t1 = [lp_dot(eye - xm[p], eye + x2[p]) for p in pairs]
    tinv = [lp_dot(t1[p], dinv[p]) for p in pairs]

    ks = [cat_rows([b[:c] for b in both[p]]) for p in pairs]
    rs = [cat_rows([b[c:] for b in both[p]]) for p in pairs]
    u = [lp_dot(tinv[p], ks[p] + akv[p]) for p in pairs]
    y = [rs[p] + lp_dot(a_rk[p], v_p[p]) - lp_dot(a_rb[p], u[p]) for p in pairs]
    for k, (su, p) in enumerate(units):
        for sl, rw in enumerate(lrows):
            s = su * nb + sl
            upd = _dot_tn(jnp.concatenate([v_p[k][rw], -u[k][rw]], axis=0),
                          jnp.concatenate([kh[k][rw], bh[k][rw]], axis=0))
            bd_ref[s, p] = (bd_ref[s, p] + jnp.where(bd_mask, upd, 0.0)) * e_last[k][sl]
    for k, (su, p) in enumerate(units):
        ur, cs = urows[k], ucols[k]
        bonus = half_sum(rkk[ur, cs]) * v_p[k]
        yc = y[k] - half_sum(y[k]) * (1.0 / hd)
        yn = yc * lax.rsqrt(half_sum(yc * yc) * (1.0 / hd) + RWKV_GN_EPS)
        out = (yn * lng_ref[:, cs] + lnb_ref[:, cs] + bonus) * g_rw[ur, cs]
        o_ref[su * nb:(su + 1) * nb, :, cs] = out.reshape(nb, c, 2 * hd)

    @pl.when(pl.program_id(1) == pl.num_programs(1) - 1)
    def _():
        for s in seqs:
            for p in range(npair):
                blk = bd_ref[s, p]
                st_ref[s, 2 * p] = blk[:hd, :hd]
                st_ref[s, 2 * p + 1] = blk[hd:, hd:]


def _rwkv(p_rwkv, shift_all, st_all, l, bsz, t, prm, shift_acc, st_acc):
    nb, c = _seq_blocking(bsz, t, RWKV_CHUNK, SEQ_BLOCK_ROWS)
    r = nb * c
    nchunk = t // c
    assert 2 * r == LANES, "the lane-paired layout holds two (r, r) matrices side by side"
    nq = next(n for n in ((4, 2, 1) if nb == 1 else (2, 1)) if bsz % (n * nb) == 0)
    nbt = nq * nb
    _, incl, strict = _seq_masks(nb, c)
    tri = jnp.asarray(_seq_masks(nbt, c)[1].astype(np.float32))
    i = np.arange(r)
    blk = (i[:, None] // 16) == (i[None, :] // 16)
    masks = np.stack([incl, strict, blk, np.eye(r, dtype=bool)]).astype(np.float32)
    masks = jnp.asarray(np.tile(masks, (1, 1, 2)))
    st_shape = (nbt, RWKV_HEADS, RWKV_HD, RWKV_HD)
    sh_shape = (nbt, 1, RWKV_PROJ)
    if st_all is None:
        st_in = jnp.zeros((bsz,) + st_shape[1:], F32)
        shift_in = jnp.zeros((bsz,) + sh_shape[1:], F32)
        st_spec = pl.BlockSpec(st_shape, lambda i, j: (i, 0, 0, 0))
        sh_spec = pl.BlockSpec(sh_shape, lambda i, j: (i, 0, 0))
    else:
        st_in, shift_in = st_all, shift_all
        st_spec = pl.BlockSpec((None,) + st_shape, lambda i, j: (l, i, 0, 0, 0))
        sh_spec = pl.BlockSpec((None,) + sh_shape, lambda i, j: (l, i, 0, 0))

    def vec(n):
        return pl.BlockSpec((None, 1, n), lambda i, j: (l, 0, 0))

    def mat(k):
        return pl.BlockSpec((None, k, 1024), lambda i, j: (l, 0, 0))

    out, shift_new, st_new = pl.pallas_call(
        functools.partial(_rwkv_kernel, nq=nq, nb=nb, c=c),
        grid=(bsz // nbt, nchunk),
        in_specs=[
            pl.BlockSpec((nbt, c, RWKV_PROJ), lambda i, j: (i, j, 0)),
            vec(RWKV_PROJ), vec(1024), mat(64), vec(1024), mat(64), mat(128),
            vec(1024), vec(1024), vec(1024), vec(1024), vec(1024),
            pl.BlockSpec((4, r, 2 * r), lambda i, j: (0, 0, 0)),
            pl.BlockSpec((nq * r, nq * r), lambda i, j: (0, 0)),
            sh_spec, st_spec,
            pl.BlockSpec(memory_space=pl.ANY), pl.BlockSpec(memory_space=pl.ANY),
        ],
        out_specs=[
            pl.BlockSpec((nbt, c, 1024), lambda i, j: (i, j, 0)),
            pl.BlockSpec((None,) + sh_shape, lambda i, j: (l, i, 0, 0)),
            pl.BlockSpec((None,) + st_shape, lambda i, j: (l, i, 0, 0, 0)),
        ],
        out_shape=[
            jax.ShapeDtypeStruct((bsz, t, 1024), F32),
            jax.ShapeDtypeStruct(shift_acc.shape, F32),
            jax.ShapeDtypeStruct(st_acc.shape, F32),
        ],
        input_output_aliases={16: 1, 17: 2},
        scratch_shapes=[pltpu.VMEM((nbt, 8 + c, RWKV_PROJ), F32),
                        pltpu.VMEM((nbt, RWKV_HEADS // 2, 2 * RWKV_HD, 2 * RWKV_HD), F32)],
        compiler_params=_params(("parallel", "arbitrary")),
        name="rwkv7",
    )(p_rwkv.reshape(bsz, t, RWKV_PROJ), prm["rwkv_mu"], prm["rwkv_w0"], prm["rwkv_w_w2"], prm["rwkv_a0"],
      prm["rwkv_w_a2"], prm["rwkv_w_g2"], prm["rwkv_k_k"], prm["rwkv_k_a"], prm["rwkv_r_k"], prm["rwkv_lnx_g"],
      prm["rwkv_lnx_b"], masks, tri, shift_in, st_in, shift_acc, st_acc)
    return out.reshape(bsz * t, 1024), shift_new, st_new


def _merge_kernel(a_ref, b_ref, c_ref, g_ref, x_ref, wb_ref, wo_ref, qn_ref, wq_ref, o_ref, q_ref):
    merged = None
    for i, br in enumerate((a_ref, b_ref, c_ref)):
        gate = jax.nn.sigmoid(g_ref[:, i * 1024:(i + 1) * 1024])
        term = gate * jnp.dot(br[...].astype(BF16), wb_ref[i], preferred_element_type=F32)
        merged = term if merged is None else merged + term
    x = x_ref[...] + jnp.dot(merged.astype(BF16), wo_ref[...], preferred_element_type=F32)
    o_ref[...] = x
    q_ref[...] = jnp.dot(_rms(x, qn_ref[...]).astype(BF16), wq_ref[...], preferred_element_type=F32)


def _merge(out_a, out_b, out_c, gates, x, l, wb, wo, qn, wq):
    m, d = x.shape
    tm = min(m, 256)
    row = lambda i: (i, 0)
    once = pl.Buffered(1)
    return pl.pallas_call(
        _merge_kernel,
        grid=(m // tm,),
        in_specs=[
            pl.BlockSpec((tm, d), row), pl.BlockSpec((tm, d), row), pl.BlockSpec((tm, d), row),
            pl.BlockSpec((tm, GATE_PROJ), row), pl.BlockSpec((tm, d), row),
            pl.BlockSpec((None, 3, d, d), lambda i: (l, 0, 0, 0), pipeline_mode=once),
            pl.BlockSpec((None, d, d), lambda i: (l, 0, 0), pipeline_mode=once),
            pl.BlockSpec((None, 1, d), lambda i: (l, 0, 0)),
            pl.BlockSpec((None, d, d), lambda i: (l, 0, 0), pipeline_mode=once),
        ],
        out_specs=[pl.BlockSpec((tm, d), row), pl.BlockSpec((tm, d), row)],
        out_shape=[jax.ShapeDtypeStruct((m, d), F32), jax.ShapeDtypeStruct((m, d), F32)],
        compiler_params=_params(("parallel",)),
        name="merge",
    )(out_a, out_b, out_c, gates, x, wb, wo, qn, wq)


def _attn_kernel(q_ref, k_ref, v_ref, o_ref, *, nb, c):
    rows = [slice(s * c, (s + 1) * c) for s in range(nb)]
    sc = [_dot_nt(q_ref[rows[s], :], k_ref[s]) * (X_HEAD_DIM ** -0.5) for s in range(nb)]
    e = [jnp.exp(x - jnp.max(x, axis=-1, keepdims=True)) for x in sc]
    pr = [x * (1.0 / jnp.sum(x, axis=-1, keepdims=True)) for x in e]
    for s in range(nb):
        o_ref[rows[s], :] = _dot(pr[s], v_ref[s])


def _attn_cache_kernel(q_ref, k_hbm, v_hbm, o_ref, kbuf, vbuf, sem, *, l, nb, c):
    nh = pl.num_programs(1)
    step = pl.program_id(0) * nh + pl.program_id(1)
    nsteps = pl.num_programs(0) * nh
    slot = step % 2

    def copies(at_step, sl):
        blk, h = at_step // nh, at_step % nh
        src = (l, pl.ds(blk * nb, nb), slice(None), h, slice(None))
        return (pltpu.make_async_copy(k_hbm.at[src], kbuf.at[sl], sem.at[0, sl]),
                pltpu.make_async_copy(v_hbm.at[src], vbuf.at[sl], sem.at[1, sl]))

    @pl.when(step == 0)
    def _():
        for cp in copies(step, slot):
            cp.start()

    @pl.when(step + 1 < nsteps)
    def _():
        for cp in copies(step + 1, 1 - slot):
            cp.start()

    for cp in copies(step, slot):
        cp.wait()
    _attn_kernel(q_ref, kbuf.at[slot], vbuf.at[slot], o_ref, nb=nb, c=c)


def _attn_cache(q, ck, cv, l, bsz, t):
    nb, c = max(1, min(bsz, SEQ_BLOCK_ROWS // t)), t
    r = nb * c
    hd = X_HEAD_DIM
    return pl.pallas_call(
        functools.partial(_attn_cache_kernel, l=l, nb=nb, c=c),
        grid=(bsz // nb, X_HEADS),
        in_specs=[pl.BlockSpec((r, hd), lambda i, h: (i, h)),
                  pl.BlockSpec(memory_space=pl.ANY), pl.BlockSpec(memory_space=pl.ANY)],
        out_specs=pl.BlockSpec((r, hd), lambda i, h: (i, h)),
        out_shape=jax.ShapeDtypeStruct((bsz * t, 1024), F32),
        scratch_shapes=[pltpu.VMEM((2, nb, N_MEM, hd), F32), pltpu.VMEM((2, nb, N_MEM, hd), F32),
                        pltpu.SemaphoreType.DMA((2, 2))],
        compiler_params=_params(("arbitrary", "arbitrary")),
        name="xattn_cache",
    )(q, ck, cv)


def _attn(q, mk, mv, l, bsz, t):
    if t > ATTN_ROWS:
        nb, c = 1, ATTN_ROWS
    else:
        nb, c = max(1, min(bsz, SEQ_BLOCK_ROWS // t)), t
    r = nb * c
    ntile = t // c
    hd = X_HEAD_DIM
    if mk.ndim == 3:
        kv_spec = pl.BlockSpec((nb, N_MEM, hd), lambda i, j, h: (i, 0, h))
    else:
        kv_spec = pl.BlockSpec((None, nb, N_MEM, hd), lambda i, j, h: (l, i, 0, h))
    return pl.pallas_call(
        functools.partial(_attn_kernel, nb=nb, c=c),
        grid=(bsz // nb, ntile, X_HEADS),
        in_specs=[pl.BlockSpec((r, hd), lambda i, j, h: (i * ntile + j, h)), kv_spec, kv_spec],
        out_specs=pl.BlockSpec((r, hd), lambda i, j, h: (i * ntile + j, h)),
        out_shape=jax.ShapeDtypeStruct((bsz * t, 1024), F32),
        compiler_params=_params(("parallel", "arbitrary", "arbitrary")),
        name="xattn_core",
    )(q, mk, mv)


def _trunk(x, prm, states, mem_k, mem_v, bsz, t, offset):
    acc_ret = jnp.zeros((DEPTH, bsz, RET_HEADS, RET_DK, RET_DK), F32)
    acc_ssm = jnp.zeros((DEPTH, bsz, SSM_HEADS, SSM_HEADDIM, SSM_STATE), F32)
    acc_conv = jnp.zeros((DEPTH, bsz, CONV_W - 1, CONV_DIM), F32)
    acc_rwkv = jnp.zeros((DEPTH, bsz, RWKV_HEADS, RWKV_HD, RWKV_HD), F32)
    acc_shift = jnp.zeros((DEPTH, bsz, 1, RWKV_PROJ), F32)
    for l in range(DEPTH):
        x = _ffn(x, prm["ffa_norm"], l, prm["ffa_w1"], prm["ffa_w3"], prm["ffa_w2"])
        p_ret = _norm_matmul(x, prm["mix_norm"], l, prm["w_ret"], 0, RET_PROJ)
        p_ssm = _norm_matmul(x, prm["mix_norm"], l, prm["w_ssm"], 0, SSM_PROJ_PAD)
        p_rwkv = _norm_matmul(x, prm["mix_norm"], l, prm["w_rwkv"], 0, RWKV_PROJ)
        p_gate = _norm_matmul(x, prm["mix_norm"], l, prm["w_gate"], 0, GATE_PROJ)
        if states is None:
            s_ret = s_ssm = s_conv = s_rwkv = s_shift = None
        else:
            s_ret, s_ssm, s_conv, s_rwkv, s_shift = states
        out_a, acc_ret = _retention(p_ret, s_ret, l, bsz, t, offset, acc_ret)
        out_b, acc_conv, acc_ssm = _ssd(p_ssm, s_conv, s_ssm, l, bsz, t, prm, acc_conv, acc_ssm)
        out_c, acc_shift, acc_rwkv = _rwkv(p_rwkv, s_shift, s_rwkv, l, bsz, t, prm, acc_shift, acc_rwkv)
        x, q = _merge(out_a, out_b, out_c, p_gate, x, l, prm["w_branch"], prm["w_out"],
                      prm["xattn_norm"], prm["xattn_wq"])
        if isinstance(mem_k, (list, tuple)):
            att = _attn(q, mem_k[l], mem_v[l], l, bsz, t)
        else:
            att = _attn_cache(q, mem_k, mem_v, l, bsz, t)
        x = _ffn(x, prm["ffb_norm"], l, prm["ffb_w1"], prm["ffb_w3"], prm["ffb_w2"], att=att, wo=prm["xattn_wo"],
                 final_g=prm["final_norm"] if l == DEPTH - 1 else None)
    return x, (acc_ret, acc_ssm, acc_conv, acc_rwkv, acc_shift)


def _prep_params(raw):
    p = {}
    for name in ("ffa_w1", "ffa_w3", "ffa_w2", "ffb_w1", "ffb_w3", "ffb_w2", "w_branch", "w_out",
                 "xattn_wq", "xattn_wk", "xattn_wv", "xattn_wo", "rwkv_w_w2", "rwkv_w_a2", "rwkv_w_g2"):
        p[name] = raw[name].astype(BF16)
    w_in = raw["w_in"]
    c0, c1, c2 = RET_PROJ, RET_PROJ + SSM_PROJ, RET_PROJ + SSM_PROJ + RWKV_PROJ
    p["w_ret"] = w_in[:, :, :c0].astype(BF16)
    p["w_ssm"] = jnp.pad(w_in[:, :, c0:c1], ((0, 0), (0, 0), (0, SSM_PROJ_PAD - SSM_PROJ))).astype(BF16)
    p["w_rwkv"] = w_in[:, :, c1:c2].astype(BF16)
    p["w_gate"] = w_in[:, :, c2:].astype(BF16)
    for name in ("ffa_norm", "mix_norm", "ssm_conv_b", "ssm_norm", "rwkv_mu", "rwkv_w0", "rwkv_a0",
                 "rwkv_k_k", "rwkv_k_a", "rwkv_lnx_g", "rwkv_lnx_b", "xattn_norm", "mem_norm", "ffb_norm"):
        p[name] = raw[name][:, None, :]
    p["rwkv_r_k"] = raw["rwkv_r_k"].reshape(DEPTH, 1, 1024)
    pad = ((0, 0), (0, LANES - SSM_HEADS))
    p["ssm_dt_bias"] = jnp.pad(raw["ssm_dt_bias"], pad)[:, None, :]
    p["ssm_a_log"] = jnp.pad(raw["ssm_a_log"], pad)[:, None, :]
    p["ssm_d"] = jnp.repeat(raw["ssm_d"], SSM_HEADDIM, axis=-1)[:, None, :]
    p["ssm_conv_w"] = raw["ssm_conv_w"]
    p["final_norm"] = raw["final_norm"][None, :]
    return p


def _run(x_prompt, x_sample, mem_prompt, states, cache_mem_k, cache_mem_v, raw):
    prm = _prep_params(raw)
    b, t, d = x_prompt.shape
    db, dt_, _ = x_sample.shape
    n_mem = mem_prompt.shape[1]
    mem2 = mem_prompt.reshape(b * n_mem, d)
    mk = [_norm_matmul(mem2, prm["mem_norm"], l, prm["xattn_wk"], 0, 1024) for l in range(DEPTH)]
    mv = [_norm_matmul(mem2, prm["mem_norm"], l, prm["xattn_wv"], 0, 1024) for l in range(DEPTH)]
    mk3 = [a.reshape(b, n_mem, d) for a in mk]
    mv3 = [a.reshape(b, n_mem, d) for a in mv]
    p_mem_k = jnp.stack(mk).reshape(DEPTH, b, n_mem, X_HEADS, X_HEAD_DIM)
    p_mem_v = jnp.stack(mv).reshape(DEPTH, b, n_mem, X_HEADS, X_HEAD_DIM)
    y_p, st_p = _trunk(x_prompt.reshape(b * t, d), prm, None, mk3, mv3, b, t, 0)

    y_s, st_s = _trunk(x_sample.reshape(db * dt_, d), prm, states, cache_mem_k, cache_mem_v, db, dt_, PAST_LEN)
    return (y_p.reshape(b, t, d), y_s.reshape(db, dt_, d)) + st_p + (p_mem_k, p_mem_v) + st_s


def kernel(x_prompt, x_sample, mem_prompt, state_ret, state_ssm, state_conv, state_rwkv, state_shift, cache_mem_k, cache_mem_v, ffa_norm, ffa_w1, ffa_w3, ffa_w2, mix_norm, w_in, ssm_conv_w, ssm_conv_b, ssm_dt_bias, ssm_a_log, ssm_d, ssm_norm, rwkv_mu, rwkv_w0, rwkv_w_w2, rwkv_a0, rwkv_w_a2, rwkv_w_g2, rwkv_k_k, rwkv_k_a, rwkv_r_k, rwkv_lnx_g, rwkv_lnx_b, w_branch, w_out, xattn_norm, mem_norm, xattn_wq, xattn_wk, xattn_wv, xattn_wo, ffb_norm, ffb_w1, ffb_w3, ffb_w2, final_norm):
    raw = dict(ffa_norm=ffa_norm, ffa_w1=ffa_w1, ffa_w3=ffa_w3, ffa_w2=ffa_w2, mix_norm=mix_norm, w_in=w_in,
               ssm_conv_w=ssm_conv_w, ssm_conv_b=ssm_conv_b, ssm_dt_bias=ssm_dt_bias, ssm_a_log=ssm_a_log,
               ssm_d=ssm_d, ssm_norm=ssm_norm, rwkv_mu=rwkv_mu, rwkv_w0=rwkv_w0, rwkv_w_w2=rwkv_w_w2,
               rwkv_a0=rwkv_a0, rwkv_w_a2=rwkv_w_a2, rwkv_w_g2=rwkv_w_g2, rwkv_k_k=rwkv_k_k,
               rwkv_k_a=rwkv_k_a, rwkv_r_k=rwkv_r_k, rwkv_lnx_g=rwkv_lnx_g, rwkv_lnx_b=rwkv_lnx_b,
               w_branch=w_branch, w_out=w_out, xattn_norm=xattn_norm, mem_norm=mem_norm,
               xattn_wq=xattn_wq, xattn_wk=xattn_wk, xattn_wv=xattn_wv, xattn_wo=xattn_wo,
               ffb_norm=ffb_norm, ffb_w1=ffb_w1, ffb_w3=ffb_w3, ffb_w2=ffb_w2, final_norm=final_norm)
    states = (state_ret, state_ssm, state_conv, state_rwkv, state_shift)
    return _run(x_prompt, x_sample, mem_prompt, states, cache_mem_k, cache_mem_v, raw)
```

```python
import functools
import math

import numpy as np
import jax
import jax.numpy as jnp
from jax import lax
from jax.experimental import pallas as pl
from jax.experimental.pallas import tpu as pltpu

F32 = jnp.float32
BF16 = jnp.bfloat16

D_MODEL = 1024
DEPTH = 4
PAST_LEN = 16384
D_FF = 2816
EPS = 1e-6
RET_HEADS = 8
RET_DK = 128
RET_GN_EPS = 1e-5
ROPE_BASE = 10000.0
SSM_HEADS = 16
SSM_HEADDIM = 64
SSM_GROUPS = 2
SSM_STATE = 128
CONV_W = 4
CONV_DIM = 1536
RWKV_HEADS = 16
RWKV_HD = 64
RWKV_PROJ = 3328
RWKV_GN_EPS = 64e-5
N_MEM = 256
X_HEADS = 4
X_HEAD_DIM = 256
RET_PROJ = 4096
SSM_PROJ = 2576
SSM_PROJ_PAD = 2688
GATE_PROJ = 3072
LANES = 128
VMEM_LIMIT = 56 * 1024 * 1024

RET_CHUNK = 128
SSD_CHUNK = 128
RWKV_CHUNK = 64
ATTN_ROWS = 2048
SEQ_BLOCK_ROWS = 64


def _params(sem):
    return pltpu.CompilerParams(dimension_semantics=sem, vmem_limit_bytes=VMEM_LIMIT)


def _rms(x, g):
    return x * lax.rsqrt(jnp.mean(x * x, axis=-1, keepdims=True) + EPS) * g


def _head_norm(x, eps):
    mu = jnp.mean(x, axis=-1, keepdims=True)
    xc = x - mu
    return xc * lax.rsqrt(jnp.mean(xc * xc, axis=-1, keepdims=True) + eps)


def _silu(x):
    return x * jax.nn.sigmoid(x)


def _dot(a, b):
    return jnp.dot(a.astype(BF16), b.astype(BF16), preferred_element_type=F32)


def _dot_nt(a, b):
    return lax.dot_general(a.astype(BF16), b.astype(BF16), (((1,), (1,)), ((), ())),
                           preferred_element_type=F32)


def _dot_tn(a, b):
    return lax.dot_general(a.astype(BF16), b.astype(BF16), (((0,), (0,)), ((), ())),
                           preferred_element_type=F32)


def _dot_f32(a, b):
    return jnp.dot(a, b, precision=lax.Precision.HIGHEST, preferred_element_type=F32)


def _seq_blocking(bsz, t, chunk_max, block_rows):
    if t > chunk_max:
        assert t % chunk_max == 0
        return 1, chunk_max
    nb = max(1, min(bsz, block_rows // t))
    assert bsz % nb == 0
    return nb, t


def _seq_masks(nb, c):
    r = nb * c
    i = np.arange(r)
    same = (i[:, None] // c) == (i[None, :] // c)
    incl = same & (i[:, None] >= i[None, :])
    strict = same & (i[:, None] > i[None, :])
    return same, incl, strict


def _nm_kernel(x_ref, g_ref, *refs):
    n = len(refs) // 2
    h = _rms(x_ref[...], g_ref[...]).astype(BF16)
    for w_ref, o_ref in zip(refs[:n], refs[n:]):
        o_ref[...] = jnp.dot(h, w_ref[...], preferred_element_type=F32)


def _norm_matmul(x, g, l, ws):
    m, d = x.shape
    tm = min(m, 512)
    row = lambda i: (i, 0)
    once = pl.Buffered(1)
    outs = pl.pallas_call(
        _nm_kernel,
        grid=(m // tm,),
        in_specs=[pl.BlockSpec((tm, d), row), pl.BlockSpec((None, 1, d), lambda i: (l, 0, 0))]
        + [pl.BlockSpec((None, d, w.shape[-1]), lambda i: (l, 0, 0), pipeline_mode=once) for w in ws],
        out_specs=[pl.BlockSpec((tm, w.shape[-1]), row) for w in ws],
        out_shape=[jax.ShapeDtypeStruct((m, w.shape[-1]), F32) for w in ws],
        compiler_params=_params(("parallel",)),
        name="norm_matmul",
    )(x, g, *ws)
    return outs


def _ffn_kernel(*refs, pre, final):
    x_ref, g_ref, w1_ref, w3_ref, w2_ref = refs[:5]
    extra = list(refs[5:-1])
    o_ref = refs[-1]
    x = x_ref[...]
    if pre:
        att_ref, wo_ref = extra[:2]
        extra = extra[2:]
        x = x + jnp.dot(att_ref[...].astype(BF16), wo_ref[...], preferred_element_type=F32)
    h = _rms(x, g_ref[...]).astype(BF16)
    a = jnp.dot(h, w1_ref[...], preferred_element_type=F32)
    b = jnp.dot(h, w3_ref[...], preferred_element_type=F32)
    u = (_silu(a) * b).astype(BF16)
    y = x + 0.5 * jnp.dot(u, w2_ref[...], preferred_element_type=F32)
    o_ref[...] = _rms(y, extra[0][...]) if final else y


def _ffn(x, g, l, w1, w3, w2, att=None, wo=None, final_g=None):
    m, d = x.shape
    f = w1.shape[-1]
    tm = min(m, 512)
    once = pl.Buffered(1)
    row = lambda i: (i, 0)
    in_specs = [
        pl.BlockSpec((tm, d), row),
        pl.BlockSpec((None, 1, d), lambda i: (l, 0, 0)),
        pl.BlockSpec((None, d, f), lambda i: (l, 0, 0), pipeline_mode=once),
        pl.BlockSpec((None, d, f), lambda i: (l, 0, 0), pipeline_mode=once),
        pl.BlockSpec((None, f, d), lambda i: (l, 0, 0), pipeline_mode=once),
    ]
    args = [x, g, w1, w3, w2]
    if att is not None:
        in_specs += [pl.BlockSpec((tm, d), row), pl.BlockSpec((None, d, d), lambda i: (l, 0, 0), pipeline_mode=once)]
        args += [att, wo]
    if final_g is not None:
        in_specs.append(pl.BlockSpec((1, d), lambda i: (0, 0)))
        args.append(final_g)
    return pl.pallas_call(
        functools.partial(_ffn_kernel, pre=att is not None, final=final_g is not None),
        grid=(m // tm,),
        in_specs=in_specs,
        out_specs=pl.BlockSpec((tm, d), row),
        out_shape=jax.ShapeDtypeStruct((m, d), F32),
        compiler_params=_params(("parallel",)),
        name="ffn",
    )(*args)


def _ret_kernel(p_ref, cos_ref, sin_ref, dmat_ref, qdec_ref, kdec_ref, st_in_ref, acc_ref,
                o_ref, st_ref, *, nq, nb, c, sdec):
    del acc_ref

    @pl.when(pl.program_id(1) == 0)
    def _():
        st_ref[...] = st_in_ref[...]

    cosf = cos_ref[...]
    sinf = sin_ref[...]
    hd = RET_DK
    r = nb * c
    units = [(su, h) for su in range(nq) for h in range(RET_HEADS)]
    lrows = [slice(sl * c, (sl + 1) * c) for sl in range(nb)]

    def slab(su, col0):
        return p_ref[su * nb:(su + 1) * nb, :, col0:col0 + hd].reshape(r, hd)

    qr, kr, v = [], [], []
    for su, h in units:
        q = slab(su, h * hd)
        k = slab(su, 1024 + h * hd)
        qr.append(((q * cosf + pltpu.roll(q, hd // 2, 1) * sinf) * (RET_DK ** -0.5)).astype(BF16))
        kr.append(k * cosf + pltpu.roll(k, hd // 2, 1) * sinf)
        v.append(slab(su, 2048 + h * hd).astype(BF16))
    scores = [_dot_nt(qr[k], kr[k]) * dmat_ref[h] for k, (_, h) in enumerate(units)]
    inter = [[_dot(qr[k][rw], st_ref[su * nb + sl, h]) for sl, rw in enumerate(lrows)]
             for k, (su, h) in enumerate(units)]
    o = [_dot(scores[k], v[k]) for k in range(len(units))]
    for k, (su, h) in enumerate(units):
        kd = kr[k] * kdec_ref[h]
        qdec = qdec_ref[h]
        for sl, rw in enumerate(lrows):
            s = su * nb + sl
            o_s = o[k][rw] + inter[k][sl] * qdec[rw]
            g = p_ref[s, :, 3072 + h * hd:3072 + (h + 1) * hd]
            o_ref[s, :, h * hd:(h + 1) * hd] = _silu(g) * _head_norm(o_s, RET_GN_EPS)
            st_ref[s, h] = st_ref[s, h] * sdec[h] + _dot_tn(kd[rw], v[k][rw])


def _retention(p_ret, st_all, l, bsz, t, offset, acc):
    nb, c = _seq_blocking(bsz, t, RET_CHUNK, SEQ_BLOCK_ROWS)
    r = nb * c
    nchunk = t // c
    heads = np.arange(RET_HEADS, dtype=np.float64)
    log_g = np.log1p(-np.exp2(-5.0 - heads))
    idx = np.arange(r) % c
    _, incl, _ = _seq_masks(nb, c)
    diff = (idx[:, None] - idx[None, :]).astype(np.float64)
    dmat = np.where(incl[None], np.exp(np.maximum(diff, 0.0)[None] * log_g[:, None, None]), 0.0)
    qdec = np.exp((idx + 1.0)[None, :] * log_g[:, None])
    kdec = np.exp((c - 1.0 - idx)[None, :] * log_g[:, None])
    qdec = np.broadcast_to(qdec[:, :, None], (RET_HEADS, r, RET_DK))
    kdec = np.broadcast_to(kdec[:, :, None], (RET_HEADS, r, RET_DK))
    sdec = tuple(float(x) for x in np.exp(c * log_g))

    half = RET_DK // 2
    freqs = ROPE_BASE ** (-jnp.arange(half, dtype=F32) / half)
    pos = jnp.float32(offset) + jnp.arange(t, dtype=F32)
    ang = pos[:, None] * freqs[None, :]
    cos = jnp.cos(ang)
    sin = jnp.sin(ang)
    cosf = jnp.tile(jnp.concatenate([cos, cos], axis=-1), (nb, 1))
    sinf = jnp.tile(jnp.concatenate([-sin, sin], axis=-1), (nb, 1))

    nq = 2 if (nb == 1 and bsz % 2 == 0) else 1
    nbt = nq * nb
    st_shape = (nbt, RET_HEADS, RET_DK, RET_DK)
    if st_all is None:
        st_in = jnp.zeros((bsz,) + st_shape[1:], F32)
        st_spec = pl.BlockSpec(st_shape, lambda i, j: (i, 0, 0, 0))
    else:
        st_in = st_all
        st_spec = pl.BlockSpec((None,) + st_shape, lambda i, j: (l, i, 0, 0, 0))
    out, st_new = pl.pallas_call(
        functools.partial(_ret_kernel, nq=nq, nb=nb, c=c, sdec=sdec),
        grid=(bsz // nbt, nchunk),
        in_specs=[
            pl.BlockSpec((nbt, c, RET_PROJ), lambda i, j: (i, j, 0)),
            pl.BlockSpec((r, RET_DK), lambda i, j: (j, 0)),
            pl.BlockSpec((r, RET_DK), lambda i, j: (j, 0)),
            pl.BlockSpec((RET_HEADS, r, r), lambda i, j: (0, 0, 0)),
            pl.BlockSpec((RET_HEADS, r, RET_DK), lambda i, j: (0, 0, 0)),
            pl.BlockSpec((RET_HEADS, r, RET_DK), lambda i, j: (0, 0, 0)),
            st_spec,
            pl.BlockSpec(memory_space=pl.ANY),
        ],
        out_specs=[
            pl.BlockSpec((nbt, c, 1024), lambda i, j: (i, j, 0)),
            pl.BlockSpec((None,) + st_shape, lambda i, j: (l, i, 0, 0, 0)),
        ],
        out_shape=[
            jax.ShapeDtypeStruct((bsz, t, 1024), F32),
            jax.ShapeDtypeStruct(acc.shape, F32),
        ],
        input_output_aliases={7: 1},
        compiler_params=_params(("parallel", "arbitrary")),
        name="retention",
    )(p_ret.reshape(bsz, t, RET_PROJ), cosf, sinf, jnp.asarray(dmat, F32), jnp.asarray(qdec, F32),
      jnp.asarray(kdec, F32), st_in, acc)
    return out.reshape(bsz * t, 1024), st_new


def _ssd_kernel(p_ref, cw_ref, cb_ref, dtb_ref, alog_ref, dfull_ref, norm_ref, tri_ref, mask2_ref,
                e64_ref, er_ref, conv_in_ref, st_in_ref, conv_acc_ref, st_acc_ref,
                o_ref, conv_out_ref, st_ref, ext_ref, stt_ref, *, nb, c):
    del conv_acc_ref, st_acc_ref
    r = nb * c
    hp = SSM_HEADDIM
    npair = SSM_HEADS // 2
    pairs = range(npair)
    seqs = range(nb)
    rows = [slice(s * c, (s + 1) * c) for s in seqs]
    cols = [slice(q * 2 * hp, (q + 1) * 2 * hp) for q in pairs]
    left = lax.broadcasted_iota(jnp.int32, (1, 2 * hp), 1) < hp

    @pl.when(pl.program_id(1) == 0)
    def _():
        for s in seqs:
            ext_ref[s, 5:8, :] = conv_in_ref[s]
            for q in pairs:
                both = jnp.concatenate([st_in_ref[s, 2 * q], st_in_ref[s, 2 * q + 1]], axis=0)
                stt_ref[s, q] = both.T

    for s in range(nb):
        ext_ref[s, 8:8 + c, :] = p_ref[s * c:(s + 1) * c, 1024:1024 + CONV_DIM]
    pieces = []
    for s in range(nb):
        acc = cb_ref[...] + ext_ref[s, 5:5 + c, :] * cw_ref[0:1, :]
        for j in range(1, CONV_W):
            acc = acc + ext_ref[s, 5 + j:5 + j + c, :] * cw_ref[j:j + 1, :]
        pieces.append(acc)
    for s in range(nb):
        conv_out_ref[s] = ext_ref[s, c + 5:c + 8, :]
        ext_ref[s, 0:8, :] = ext_ref[s, c:c + 8, :]
    xbc = _silu(pieces[0] if nb == 1 else jnp.concatenate(pieces, axis=0))
    xs = xbc[:, :1024]
    bm = xbc[:, 1024:1024 + SSM_GROUPS * SSM_STATE]
    cm = xbc[:, 1024 + SSM_GROUPS * SSM_STATE:]

    dt = jax.nn.softplus(p_ref[:, 1024 + CONV_DIM:] + dtb_ref[...])
    a = -jnp.exp(alog_ref[...])
    tri = tri_ref[...]
    cum = _dot_f32(tri, dt * a)
    cum_t = cum.T

    def expand(v, e_ref):
        hi = v.astype(BF16)
        r1 = v - hi.astype(F32)
        mid = r1.astype(BF16)
        lo = (r1 - mid.astype(F32)).astype(BF16)
        return jnp.dot(jnp.concatenate([hi, mid, lo], axis=1), e_ref[...], preferred_element_type=F32)

    def bd(v):
        vb = v.astype(BF16)
        zero = jnp.zeros_like(vb)
        return jnp.concatenate([jnp.where(left, vb, zero), jnp.where(left, zero, vb)], axis=0)

    def cat_rows(pieces):
        return pieces[0] if nb == 1 else jnp.concatenate(pieces, axis=0)

    cum_full = expand(cum, e64_ref)
    cum_wide = cum_full if r == hp else expand(cum, er_ref)
    xdt = xs * expand(dt, e64_ref)
    last = [cum_full[s * c + c - 1:s * c + c, :] for s in seqs]
    xw = cat_rows([xdt[rw] * jnp.exp(last[s] - cum_full[rw]) for s, rw in enumerate(rows)])
    eci = jnp.exp(cum_full)
    mask2 = mask2_ref[...] > 0.5

    hg = SSM_HEADS // SSM_GROUPS
    bm_g = [bm[:, g * SSM_STATE:(g + 1) * SSM_STATE] for g in range(SSM_GROUPS)]
    cm_g = [cm[:, g * SSM_STATE:(g + 1) * SSM_STATE] for g in range(SSM_GROUPS)]
    cb2 = []
    for g in range(SSM_GROUPS):
        cb = _dot_nt(cm_g[g], bm_g[g])
        cb2.append(jnp.concatenate([cb, cb], axis=1))
    grp = [2 * q // hg for q in pairs]
    inter = [cat_rows([_dot(cm_g[grp[q]][rw], stt_ref[s, q]) for s, rw in enumerate(rows)]) for q in pairs]
    sc = []
    for q in pairs:
        row2 = jnp.concatenate([cum_t[2 * q:2 * q + 1, :], cum_t[2 * q + 1:2 * q + 2, :]], axis=1)
        seg = cum_wide[:, q * 2 * r:(q + 1) * 2 * r] - row2
        sc.append(cb2[grp[q]] * jnp.where(mask2, jnp.exp(seg), 0.0))
    y_p = [jnp.dot(sc[q].astype(BF16), bd(xdt[:, cols[q]]), preferred_element_type=F32)
           + inter[q] * eci[:, cols[q]] for q in pairs]
    for q in pairs:
        for s, rw in enumerate(rows):
            upd = _dot_tn(bm_g[grp[q]][rw], xw[rw, cols[q]])
            stt_ref[s, q] = stt_ref[s, q] * jnp.exp(last[s][:, cols[q]]) + upd
    y = jnp.concatenate(y_p, axis=1) + dfull_ref[...] * xs
    z = p_ref[:, :1024]
    o_ref[...] = _rms(y * _silu(z), norm_ref[...])

    @pl.when(pl.program_id(1) == pl.num_programs(1) - 1)
    def _():
        for s in seqs:
            for q in pairs:
                both = stt_ref[s, q].T
                st_ref[s, 2 * q] = both[:hp]
                st_ref[s, 2 * q + 1] = both[hp:]


def _ssd(p_ssm, conv_all, st_all, l, bsz, t, prm, conv_acc, st_acc):
    nb, c = _seq_blocking(bsz, t, SSD_CHUNK, SEQ_BLOCK_ROWS)
    r = nb * c
    nchunk = t // c
    _, incl, _ = _seq_masks(nb, c)
    tri = jnp.asarray(incl.astype(np.float32))
    mask2 = jnp.asarray(np.tile(incl.astype(np.float32), (1, 2)))

    def expander(width):
        e = np.zeros((LANES, SSM_HEADS * width), np.float32)
        for h in range(SSM_HEADS):
            e[h, h * width:(h + 1) * width] = 1.0
        return jnp.asarray(np.concatenate([e, e, e], axis=0), BF16)

    e64 = expander(SSM_HEADDIM)
    e_r = expander(r)
    st_shape = (nb, SSM_HEADS, SSM_HEADDIM, SSM_STATE)
    cv_shape = (nb, CONV_W - 1, CONV_DIM)
    if st_all is None:
        st_in = jnp.zeros((bsz,) + st_shape[1:], F32)
        conv_in = jnp.zeros((bsz,) + cv_shape[1:], F32)
        st_spec = pl.BlockSpec(st_shape, lambda i, j: (i, 0, 0, 0))
        cv_spec = pl.BlockSpec(cv_shape, lambda i, j: (i, 0, 0))
    else:
        st_in, conv_in = st_all, conv_all
        st_spec = pl.BlockSpec((None,) + st_shape, lambda i, j: (l, i, 0, 0, 0))
        cv_spec = pl.BlockSpec((None,) + cv_shape, lambda i, j: (l, i, 0, 0))
    m = bsz * t

    def vec(n):
        return pl.BlockSpec((None, 1, n), lambda i, j: (l, 0, 0))

    return pl.pallas_call(
        functools.partial(_ssd_kernel, nb=nb, c=c),
        grid=(bsz // nb, nchunk),
        in_specs=[
            pl.BlockSpec((r, SSM_PROJ_PAD), lambda i, j: (i * nchunk + j, 0)),
            pl.BlockSpec((None, CONV_W, CONV_DIM), lambda i, j: (l, 0, 0)),
            vec(CONV_DIM), vec(LANES), vec(LANES), vec(1024), vec(1024),
            pl.BlockSpec((r, r), lambda i, j: (0, 0)),
            pl.BlockSpec((r, 2 * r), lambda i, j: (0, 0)),
            pl.BlockSpec(e64.shape, lambda i, j: (0, 0)),
            pl.BlockSpec(e_r.shape, lambda i, j: (0, 0)),
            cv_spec, st_spec,
            pl.BlockSpec(memory_space=pl.ANY), pl.BlockSpec(memory_space=pl.ANY),
        ],
        out_specs=[
            pl.BlockSpec((r, 1024), lambda i, j: (i * nchunk + j, 0)),
            pl.BlockSpec((None,) + cv_shape, lambda i, j: (l, i, 0, 0)),
            pl.BlockSpec((None,) + st_shape, lambda i, j: (l, i, 0, 0, 0)),
        ],
        out_shape=[
            jax.ShapeDtypeStruct((m, 1024), F32),
            jax.ShapeDtypeStruct(conv_acc.shape, F32),
            jax.ShapeDtypeStruct(st_acc.shape, F32),
        ],
        input_output_aliases={13: 1, 14: 2},
        scratch_shapes=[pltpu.VMEM((nb, 8 + c, CONV_DIM), F32),
                        pltpu.VMEM((nb, SSM_HEADS // 2, SSM_STATE, 2 * SSM_HEADDIM), F32)],
        compiler_params=_params(("parallel", "arbitrary")),
        name="ssd",
    )(p_ssm, prm["ssm_conv_w"], prm["ssm_conv_b"], prm["ssm_dt_bias"], prm["ssm_a_log"],
      prm["ssm_d"], prm["ssm_norm"], tri, mask2, e64, e_r, conv_in, st_in, conv_acc, st_acc)


def _rwkv_kernel(p_ref, mu_ref, w0_ref, ww2_ref, a0_ref, wa2_ref, wg2_ref, kk_ref, ka_ref,
                 rk_ref, lng_ref, lnb_ref, masks_ref, tri_ref, shift_in_ref, st_in_ref, shift_acc_ref,
                 st_acc_ref, o_ref, shift_out_ref, st_ref, ext_ref, bd_ref, *, nq, nb, c):
    del shift_acc_ref, st_acc_ref
    r = nb * c
    hd = RWKV_HD
    npair = RWKV_HEADS // 2
    seqs = range(nq * nb)
    rows = [slice(s * c, (s + 1) * c) for s in seqs]
    lane = lax.broadcasted_iota(jnp.int32, (1, 2 * hd), 1)
    left = lane < hd
    sub = lax.broadcasted_iota(jnp.int32, (2 * hd, 1), 0)
    bd_mask = (sub < hd) == left

    @pl.when(pl.program_id(1) == 0)
    def _():
        zero = jnp.zeros((hd, hd), F32)
        for s in seqs:
            ext_ref[s, 7:8, :] = shift_in_ref[s]
            for p in range(npair):
                top = jnp.concatenate([st_in_ref[s, 2 * p], zero], axis=1)
                bot = jnp.concatenate([zero, st_in_ref[s, 2 * p + 1]], axis=1)
                bd_ref[s, p] = jnp.concatenate([top, bot], axis=0)

    x = p_ref[...].reshape(nq * r, RWKV_PROJ)
    for s in seqs:
        ext_ref[s, 8:8 + c, :] = x[rows[s]]
    prev = [ext_ref[s, 7:7 + c, :] for s in seqs]
    prev = prev[0] if len(prev) == 1 else jnp.concatenate(prev, axis=0)
    for s in seqs:
        ext_ref[s, 0:8, :] = ext_ref[s, c:c + 8, :]
        shift_out_ref[s] = ext_ref[s, 7:8, :]

    mixed = x + (prev - x) * mu_ref[...]
    rr = mixed[:, 0:1024]
    kc = mixed[:, 1024:2048]
    vc = mixed[:, 2048:3072]
    wd = mixed[:, 3072:3136]
    ad = mixed[:, 3136:3200]
    gd = mixed[:, 3200:3328]
    w_log = -jax.nn.softplus(-(w0_ref[...] + _dot(jnp.tanh(wd), ww2_ref[...]))) - 0.5
    lw = -jnp.exp(w_log)
    a_lr = jax.nn.sigmoid(a0_ref[...] + _dot(ad, wa2_ref[...]))
    g_rw = _dot(jax.nn.sigmoid(gd), wg2_ref[...])
    kk = kc * kk_ref[...]
    kmod = kc * (1.0 + (a_lr - 1.0) * ka_ref[...])
    rkk = rr * kmod * rk_ref[...]

    m_incl = masks_ref[0]
    m_strict = masks_ref[1]
    m_blk = masks_ref[2]
    eye = masks_ref[3]
    cl = _dot_f32(tri_ref[...], lw)

    def half_sum(a):
        sl = jnp.sum(jnp.where(left, a, 0.0), axis=-1, keepdims=True)
        sr = jnp.sum(jnp.where(left, 0.0, a), axis=-1, keepdims=True)
        return jnp.where(left, sl, sr)

    def bd(a):
        ab = a.astype(BF16)
        zero = jnp.zeros_like(ab)
        return jnp.concatenate([jnp.where(left, ab, zero), jnp.where(left, zero, ab)], axis=0)

    def lp_dot(a, b):
        return jnp.dot(a.astype(BF16), bd(b), preferred_element_type=F32)

    def cat_rows(pieces):
        return pieces[0] if nb == 1 else jnp.concatenate(pieces, axis=0)

    units = [(su, p) for su in range(nq) for p in range(npair)]
    pairs = range(len(units))
    ucols = [slice(p * 2 * hd, (p + 1) * 2 * hd) for _, p in units]
    urows = [slice(su * r, (su + 1) * r) for su, _ in units]
    lrows = [slice(sl * c, (sl + 1) * c) for sl in range(nb)]
    v_p = [vc[ur, cs] for ur, cs in zip(urows, ucols)]
    kt, rt, kh, bh, e_last = [], [], [], [], []
    for ur, cs in zip(urows, ucols):
        kk_p = kk[ur, cs]
        kk_p = kk_p * lax.rsqrt(jnp.maximum(half_sum(kk_p * kk_p), 1e-24))
        cl_p = cl[ur, cs]
        e_incl = jnp.exp(cl_p)
        e_inv = jnp.exp(-cl_p)
        rt.append(rr[ur, cs] * e_incl)
        kt.append(kk_p * jnp.exp(cl_p - lw[ur, cs]))
        kh.append(kmod[ur, cs] * e_inv)
        bh.append(kk_p * a_lr[ur, cs] * e_inv)
        e_last.append([e_incl[sl * c + c - 1:sl * c + c, :] for sl in range(nb)])
    lhs = [jnp.concatenate([kt[p], rt[p]], axis=0).astype(BF16) for p in pairs]
    gk = [_dot_nt(lhs[p], bd(kh[p])) for p in pairs]
    gb = [_dot_nt(lhs[p], bd(bh[p])) for p in pairs]
    both = [[_dot_nt(jnp.concatenate([kt[k][rw], rt[k][rw]], axis=0) if nb > 1 else lhs[k],
                     bd_ref[su * nb + sl, p]) for sl, rw in enumerate(lrows)]
            for k, (su, p) in enumerate(units)]
    a_kk = [g[:r] * m_strict for g in gk]
    a_rk = [g[r:] * m_incl for g in gk]
    a_kb = [g[:r] * m_strict for g in gb]
    a_rb = [g[r:] * m_incl for g in gb]
    akv = [lp_dot(a_kk[p], v_p[p]) for p in pairs]

    n1 = [a * m_blk for a in a_kb]
    n2 = [lp_dot(n, n) for n in n1]
    p1 = [lp_dot(eye - n1[p], eye + n2[p]) for p in pairs]
    n4 = [lp_dot(n, n) for n in n2]
    p2 = [lp_dot(p1[p], eye + n4[p]) for p in pairs]
    n8 = [lp_dot(n, n) for n in n4]
    dinv = [lp_dot(p2[p], eye + n8[p]) for p in pairs]
    xm = [lp_dot(dinv[p], a_kb[p] - n1[p]) for p in pairs]
    x2 = [lp_dot(x_, x_) for x_ in xm]
    t1 = [lp_dot(eye - xm[p], eye + x2[p]) for p in pairs]
    tinv = [lp_dot(t1[p], dinv[p]) for p in pairs]

    ks = [cat_rows([b[:c] for b in both[p]]) for p in pairs]
    rs = [cat_rows([b[c:] for b in both[p]]) for p in pairs]
    u = [lp_dot(tinv[p], ks[p] + akv[p]) for p in pairs]
    y = [rs[p] + lp_dot(a_rk[p], v_p[p]) - lp_dot(a_rb[p], u[p]) for p in pairs]
    for k, (su, p) in enumerate(units):
        for sl, rw in enumerate(lrows):
            s = su * nb + sl
            upd = _dot_tn(jnp.concatenate([v_p[k][rw], -u[k][rw]], axis=0),
                          jnp.concatenate([kh[k][rw], bh[k][rw]], axis=0))
            bd_ref[s, p] = (bd_ref[s, p] + jnp.where(bd_mask, upd, 0.0)) * e_last[k][sl]
    for k, (su, p) in enumerate(units):
        ur, cs = urows[k], ucols[k]
        bonus = half_sum(rkk[ur, cs]) * v_p[k]
        yc = y[k] - half_sum(y[k]) * (1.0 / hd)
        yn = yc * lax.rsqrt(half_sum(yc * yc) * (1.0 / hd) + RWKV_GN_EPS)
        out = (yn * lng_ref[:, cs] + lnb_ref[:, cs] + bonus) * g_rw[ur, cs]
        o_ref[su * nb:(su + 1) * nb, :, cs] = out.reshape(nb, c, 2 * hd)

    @pl.when(pl.program_id(1) == pl.num_programs(1) - 1)
    def _():
        for s in seqs:
            for p in range(npair):
                blk = bd_ref[s, p]
                st_ref[s, 2 * p] = blk[:hd, :hd]
                st_ref[s, 2 * p + 1] = blk[hd:, hd:]


def _rwkv(p_rwkv, shift_all, st_all, l, bsz, t, prm, shift_acc, st_acc):
    nb, c = _seq_blocking(bsz, t, RWKV_CHUNK, SEQ_BLOCK_ROWS)
    r = nb * c
    nchunk = t // c
    assert 2 * r == LANES, "the lane-paired layout holds two (r, r) matrices side by side"
    nq = next(n for n in ((4, 2, 1) if nb == 1 else (2, 1)) if bsz % (n * nb) == 0)
    nbt = nq * nb
    _, incl, strict = _seq_masks(nb, c)
    tri = jnp.asarray(_seq_masks(nbt, c)[1].astype(np.float32))
    i = np.arange(r)
    blk = (i[:, None] // 16) == (i[None, :] // 16)
    masks = np.stack([incl, strict, blk, np.eye(r, dtype=bool)]).astype(np.float32)
    masks = jnp.asarray(np.tile(masks, (1, 1, 2)))
    st_shape = (nbt, RWKV_HEADS, RWKV_HD, RWKV_HD)
    sh_shape = (nbt, 1, RWKV_PROJ)
    if st_all is None:
        st_in = jnp.zeros((bsz,) + st_shape[1:], F32)
        shift_in = jnp.zeros((bsz,) + sh_shape[1:], F32)
        st_spec = pl.BlockSpec(st_shape, lambda i, j: (i, 0, 0, 0))
        sh_spec = pl.BlockSpec(sh_shape, lambda i, j: (i, 0, 0))
    else:
        st_in, shift_in = st_all, shift_all
        st_spec = pl.BlockSpec((None,) + st_shape, lambda i, j: (l, i, 0, 0, 0))
        sh_spec = pl.BlockSpec((None,) + sh_shape, lambda i, j: (l, i, 0, 0))

    def vec(n):
        return pl.BlockSpec((None, 1, n), lambda i, j: (l, 0, 0))

    def mat(k):
        return pl.BlockSpec((None, k, 1024), lambda i, j: (l, 0, 0))

    out, shift_new, st_new = pl.pallas_call(
        functools.partial(_rwkv_kernel, nq=nq, nb=nb, c=c),
        grid=(bsz // nbt, nchunk),
        in_specs=[
            pl.BlockSpec((nbt, c, RWKV_PROJ), lambda i, j: (i, j, 0)),
            vec(RWKV_PROJ), vec(1024), mat(64), vec(1024), mat(64), mat(128),
            vec(1024), vec(1024), vec(1024), vec(1024), vec(1024),
            pl.BlockSpec((4, r, 2 * r), lambda i, j: (0, 0, 0)),
            pl.BlockSpec((nq * r, nq * r), lambda i, j: (0, 0)),
            sh_spec, st_spec,
            pl.BlockSpec(memory_space=pl.ANY), pl.BlockSpec(memory_space=pl.ANY),
        ],
        out_specs=[
            pl.BlockSpec((nbt, c, 1024), lambda i, j: (i, j, 0)),
            pl.BlockSpec((None,) + sh_shape, lambda i, j: (l, i, 0, 0)),
            pl.BlockSpec((None,) + st_shape, lambda i, j: (l, i, 0, 0, 0)),
        ],
        out_shape=[
            jax.ShapeDtypeStruct((bsz, t, 1024), F32),
            jax.ShapeDtypeStruct(shift_acc.shape, F32),
            jax.ShapeDtypeStruct(st_acc.shape, F32),
        ],
        input_output_aliases={16: 1, 17: 2},
        scratch_shapes=[pltpu.VMEM((nbt, 8 + c, RWKV_PROJ), F32),
                        pltpu.VMEM((nbt, RWKV_HEADS // 2, 2 * RWKV_HD, 2 * RWKV_HD), F32)],
        compiler_params=_params(("parallel", "arbitrary")),
        name="rwkv7",
    )(p_rwkv.reshape(bsz, t, RWKV_PROJ), prm["rwkv_mu"], prm["rwkv_w0"], prm["rwkv_w_w2"], prm["rwkv_a0"],
      prm["rwkv_w_a2"], prm["rwkv_w_g2"], prm["rwkv_k_k"], prm["rwkv_k_a"], prm["rwkv_r_k"], prm["rwkv_lnx_g"],
      prm["rwkv_lnx_b"], masks, tri, shift_in, st_in, shift_acc, st_acc)
    return out.reshape(bsz * t, 1024), shift_new, st_new


def _merge_kernel(a_ref, b_ref, c_ref, g_ref, x_ref, wb_ref, wo_ref, qn_ref, wq_ref, o_ref, q_ref):
    merged = None
    for i, br in enumerate((a_ref, b_ref, c_ref)):
        gate = jax.nn.sigmoid(g_ref[:, i * 1024:(i + 1) * 1024])
        term = gate * jnp.dot(br[...].astype(BF16), wb_ref[i], preferred_element_type=F32)
        merged = term if merged is None else merged + term
    x = x_ref[...] + jnp.dot(merged.astype(BF16), wo_ref[...], preferred_element_type=F32)
    o_ref[...] = x
    q_ref[...] = jnp.dot(_rms(x, qn_ref[...]).astype(BF16), wq_ref[...], preferred_element_type=F32)


def _merge(out_a, out_b, out_c, gates, x, l, wb, wo, qn, wq):
    m, d = x.shape
    tm = min(m, 512)
    row = lambda i: (i, 0)
    once = pl.Buffered(1)
    return pl.pallas_call(
        _merge_kernel,
        grid=(m // tm,),
        in_specs=[
            pl.BlockSpec((tm, d), row), pl.BlockSpec((tm, d), row), pl.BlockSpec((tm, d), row),
            pl.BlockSpec((tm, GATE_PROJ), row), pl.BlockSpec((tm, d), row),
            pl.BlockSpec((None, 3, d, d), lambda i: (l, 0, 0, 0), pipeline_mode=once),
            pl.BlockSpec((None, d, d), lambda i: (l, 0, 0), pipeline_mode=once),
            pl.BlockSpec((None, 1, d), lambda i: (l, 0, 0)),
            pl.BlockSpec((None, d, d), lambda i: (l, 0, 0), pipeline_mode=once),
        ],
        out_specs=[pl.BlockSpec((tm, d), row), pl.BlockSpec((tm, d), row)],
        out_shape=[jax.ShapeDtypeStruct((m, d), F32), jax.ShapeDtypeStruct((m, d), F32)],
        compiler_params=_params(("parallel",)),
        name="merge",
    )(out_a, out_b, out_c, gates, x, wb, wo, qn, wq)


def _attn_kernel(q_ref, k_ref, v_ref, o_ref, *, nb, c):
    rows = [slice(s * c, (s + 1) * c) for s in range(nb)]
    sc = [_dot_nt(q_ref[rows[s], :], k_ref[s]) * (X_HEAD_DIM ** -0.5) for s in range(nb)]
    e = [jnp.exp(x - jnp.max(x, axis=-1, keepdims=True)) for x in sc]
    pr = [x * (1.0 / jnp.sum(x, axis=-1, keepdims=True)) for x in e]
    for s in range(nb):
        o_ref[rows[s], :] = _dot(pr[s], v_ref[s])


def _attn_cache_kernel(q_ref, k_hbm, v_hbm, o_ref, kbuf, vbuf, sem, *, l, nb, c):
    nh = pl.num_programs(1)
    step = pl.program_id(0) * nh + pl.program_id(1)
    nsteps = pl.num_programs(0) * nh
    slot = step % 2

    def copies(at_step, sl):
        blk, h = at_step // nh, at_step % nh
        src = (l, pl.ds(blk * nb, nb), slice(None), h, slice(None))
        return (pltpu.make_async_copy(k_hbm.at[src], kbuf.at[sl], sem.at[0, sl]),
                pltpu.make_async_copy(v_hbm.at[src], vbuf.at[sl], sem.at[1, sl]))

    @pl.when(step == 0)
    def _():
        for cp in copies(step, slot):
            cp.start()

    @pl.when(step + 1 < nsteps)
    def _():
        for cp in copies(step + 1, 1 - slot):
            cp.start()

    for cp in copies(step, slot):
        cp.wait()
    _attn_kernel(q_ref, kbuf.at[slot], vbuf.at[slot], o_ref, nb=nb, c=c)


def _attn_cache(q, ck, cv, l, bsz, t):
    nb, c = max(1, min(bsz, SEQ_BLOCK_ROWS // t)), t
    r = nb * c
    hd = X_HEAD_DIM
    return pl.pallas_call(
        functools.partial(_attn_cache_kernel, l=l, nb=nb, c=c),
        grid=(bsz // nb, X_HEADS),
        in_specs=[pl.BlockSpec((r, hd), lambda i, h: (i, h)),
                  pl.BlockSpec(memory_space=pl.ANY), pl.BlockSpec(memory_space=pl.ANY)],
        out_specs=pl.BlockSpec((r, hd), lambda i, h: (i, h)),
        out_shape=jax.ShapeDtypeStruct((bsz * t, 1024), F32),
        scratch_shapes=[pltpu.VMEM((2, nb, N_MEM, hd), F32), pltpu.VMEM((2, nb, N_MEM, hd), F32),
                        pltpu.SemaphoreType.DMA((2, 2))],
        compiler_params=_params(("arbitrary", "arbitrary")),
        name="xattn_cache",
    )(q, ck, cv)


def _attn(q, mk, mv, l, bsz, t):
    if t > ATTN_ROWS:
        nb, c = 1, ATTN_ROWS
    else:
        nb, c = max(1, min(bsz, SEQ_BLOCK_ROWS // t)), t
    r = nb * c
    ntile = t // c
    hd = X_HEAD_DIM
    if mk.ndim == 3:
        kv_spec = pl.BlockSpec((nb, N_MEM, hd), lambda i, j, h: (i, 0, h))
    else:
        kv_spec = pl.BlockSpec((None, nb, N_MEM, hd), lambda i, j, h: (l, i, 0, h))
    return pl.pallas_call(
        functools.partial(_attn_kernel, nb=nb, c=c),
        grid=(bsz // nb, ntile, X_HEADS),
        in_specs=[pl.BlockSpec((r, hd), lambda i, j, h: (i * ntile + j, h)), kv_spec, kv_spec],
        out_specs=pl.BlockSpec((r, hd), lambda i, j, h: (i * ntile + j, h)),
        out_shape=jax.ShapeDtypeStruct((bsz * t, 1024), F32),
        compiler_params=_params(("parallel", "arbitrary", "arbitrary")),
        name="xattn_core",
    )(q, mk, mv)


def _trunk(x, prm, states, mem_k, mem_v, bsz, t, offset):
    acc_ret = jnp.zeros((DEPTH, bsz, RET_HEADS, RET_DK, RET_DK), F32)
    acc_ssm = jnp.zeros((DEPTH, bsz, SSM_HEADS, SSM_HEADDIM, SSM_STATE), F32)
    acc_conv = jnp.zeros((DEPTH, bsz, CONV_W - 1, CONV_DIM), F32)
    acc_rwkv = jnp.zeros((DEPTH, bsz, RWKV_HEADS, RWKV_HD, RWKV_HD), F32)
    acc_shift = jnp.zeros((DEPTH, bsz, 1, RWKV_PROJ), F32)
    for l in range(DEPTH):
        x = _ffn(x, prm["ffa_norm"], l, prm["ffa_w1"], prm["ffa_w3"], prm["ffa_w2"])
        p_ret, p_gate = _norm_matmul(x, prm["mix_norm"], l, [prm["w_ret"], prm["w_gate"]])
        p_ssm, p_rwkv = _norm_matmul(x, prm["mix_norm"], l, [prm["w_ssm"], prm["w_rwkv"]])
        if states is None:
            s_ret = s_ssm = s_conv = s_rwkv = s_shift = None
        else:
            s_ret, s_ssm, s_conv, s_rwkv, s_shift = states
        out_a, acc_ret = _retention(p_ret, s_ret, l, bsz, t, offset, acc_ret)
        out_b, acc_conv, acc_ssm = _ssd(p_ssm, s_conv, s_ssm, l, bsz, t, prm, acc_conv, acc_ssm)
        out_c, acc_shift, acc_rwkv = _rwkv(p_rwkv, s_shift, s_rwkv, l, bsz, t, prm, acc_shift, acc_rwkv)
        x, q = _merge(out_a, out_b, out_c, p_gate, x, l, prm["w_branch"], prm["w_out"],
                      prm["xattn_norm"], prm["xattn_wq"])
        if isinstance(mem_k, (list, tuple)):
            att = _attn(q, mem_k[l], mem_v[l], l, bsz, t)
        else:
            att = _attn_cache(q, mem_k, mem_v, l, bsz, t)
        x = _ffn(x, prm["ffb_norm"], l, prm["ffb_w1"], prm["ffb_w3"], prm["ffb_w2"], att=att, wo=prm["xattn_wo"],
                 final_g=prm["final_norm"] if l == DEPTH - 1 else None)
    return x, (acc_ret, acc_ssm, acc_conv, acc_rwkv, acc_shift)


def _prep_params(raw):
    p = {}
    for name in ("ffa_w1", "ffa_w3", "ffa_w2", "ffb_w1", "ffb_w3", "ffb_w2", "w_branch", "w_out",
                 "xattn_wq", "xattn_wk", "xattn_wv", "xattn_wo", "rwkv_w_w2", "rwkv_w_a2", "rwkv_w_g2"):
        p[name] = raw[name].astype(BF16)
    w_in = raw["w_in"]
    c0, c1, c2 = RET_PROJ, RET_PROJ + SSM_PROJ, RET_PROJ + SSM_PROJ + RWKV_PROJ
    p["w_ret"] = w_in[:, :, :c0].astype(BF16)
    p["w_ssm"] = jnp.pad(w_in[:, :, c0:c1], ((0, 0), (0, 0), (0, SSM_PROJ_PAD - SSM_PROJ))).astype(BF16)
    p["w_rwkv"] = w_in[:, :, c1:c2].astype(BF16)
    p["w_gate"] = w_in[:, :, c2:].astype(BF16)
    for name in ("ffa_norm", "mix_norm", "ssm_conv_b", "ssm_norm", "rwkv_mu", "rwkv_w0", "rwkv_a0",
                 "rwkv_k_k", "rwkv_k_a", "rwkv_lnx_g", "rwkv_lnx_b", "xattn_norm", "mem_norm", "ffb_norm"):
        p[name] = raw[name][:, None, :]
    p["rwkv_r_k"] = raw["rwkv_r_k"].reshape(DEPTH, 1, 1024)
    pad = ((0, 0), (0, LANES - SSM_HEADS))
    p["ssm_dt_bias"] = jnp.pad(raw["ssm_dt_bias"], pad)[:, None, :]
    p["ssm_a_log"] = jnp.pad(raw["ssm_a_log"], pad)[:, None, :]
    p["ssm_d"] = jnp.repeat(raw["ssm_d"], SSM_HEADDIM, axis=-1)[:, None, :]
    p["ssm_conv_w"] = raw["ssm_conv_w"]
    p["final_norm"] = raw["final_norm"][None, :]
    return p


def _run(x_prompt, x_sample, mem_prompt, states, cache_mem_k, cache_mem_v, raw):
    prm = _prep_params(raw)
    b, t, d = x_prompt.shape
    db, dt_, _ = x_sample.shape
    n_mem = mem_prompt.shape[1]
    mem2 = mem_prompt.reshape(b * n_mem, d)
    mkv = [_norm_matmul(mem2, prm["mem_norm"], l, [prm["xattn_wk"], prm["xattn_wv"]]) for l in range(DEPTH)]
    mk = [kv[0] for kv in mkv]
    mv = [kv[1] for kv in mkv]
    mk3 = [a.reshape(b, n_mem, d) for a in mk]
    mv3 = [a.reshape(b, n_mem, d) for a in mv]
    p_mem_k = jnp.stack(mk).reshape(DEPTH, b, n_mem, X_HEADS, X_HEAD_DIM)
    p_mem_v = jnp.stack(mv).reshape(DEPTH, b, n_mem, X_HEADS, X_HEAD_DIM)
    y_p, st_p = _trunk(x_prompt.reshape(b * t, d), prm, None, mk3, mv3, b, t, 0)

    y_s, st_s = _trunk(x_sample.reshape(db * dt_, d), prm, states, cache_mem_k, cache_mem_v, db, dt_, PAST_LEN)
    return (y_p.reshape(b, t, d), y_s.reshape(db, dt_, d)) + st_p + (p_mem_k, p_mem_v) + st_s


def kernel(x_prompt, x_sample, mem_prompt, state_ret, state_ssm, state_conv, state_rwkv, state_shift, cache_mem_k, cache_mem_v, ffa_norm, ffa_w1, ffa_w3, ffa_w2, mix_norm, w_in, ssm_conv_w, ssm_conv_b, ssm_dt_bias, ssm_a_log, ssm_d, ssm_norm, rwkv_mu, rwkv_w0, rwkv_w_w2, rwkv_a0, rwkv_w_a2, rwkv_w_g2, rwkv_k_k, rwkv_k_a, rwkv_r_k, rwkv_lnx_g, rwkv_lnx_b, w_branch, w_out, xattn_norm, mem_norm, xattn_wq, xattn_wk, xattn_wv, xattn_wo, ffb_norm, ffb_w1, ffb_w3, ffb_w2, final_norm):
    raw = dict(ffa_norm=ffa_norm, ffa_w1=ffa_w1, ffa_w3=ffa_w3, ffa_w2=ffa_w2, mix_norm=mix_norm, w_in=w_in,
               ssm_conv_w=ssm_conv_w, ssm_conv_b=ssm_conv_b, ssm_dt_bias=ssm_dt_bias, ssm_a_log=ssm_a_log,
               ssm_d=ssm_d, ssm_norm=ssm_norm, rwkv_mu=rwkv_mu, rwkv_w0=rwkv_w0, rwkv_w_w2=rwkv_w_w2,
               rwkv_a0=rwkv_a0, rwkv_w_a2=rwkv_w_a2, rwkv_w_g2=rwkv_w_g2, rwkv_k_k=rwkv_k_k,
               rwkv_k_a=rwkv_k_a, rwkv_r_k=rwkv_r_k, rwkv_lnx_g=rwkv_lnx_g, rwkv_lnx_b=rwkv_lnx_b,
               w_branch=w_branch, w_out=w_out, xattn_norm=xattn_norm, mem_norm=mem_norm,
               xattn_wq=xattn_wq, xattn_wk=xattn_wk, xattn_wv=xattn_wv, xattn_wo=xattn_wo,
               ffb_norm=ffb_norm, ffb_w1=ffb_w1, ffb_w3=ffb_w3, ffb_w2=ffb_w2, final_norm=final_norm)
    states = (state_ret, state_ssm, state_conv, state_rwkv, state_shift)
    return _run(x_prompt, x_sample, mem_prompt, states, cache_mem_k, cache_mem_v, raw)
```

```python
import functools

import numpy as np
import jax
import jax.numpy as jnp
from jax import lax
from jax.experimental import pallas as pl
from jax.experimental.pallas import tpu as pltpu

F32 = jnp.float32
BF16 = jnp.bfloat16

DEPTH = 4
PAST_LEN = 16384
EPS = 1e-6
RET_HEADS = 8
RET_DK = 128
RET_GN_EPS = 1e-5
ROPE_BASE = 10000.0
SSM_HEADS = 16
SSM_HEADDIM = 64
SSM_GROUPS = 2
SSM_STATE = 128
CONV_W = 4
CONV_DIM = 1536
RWKV_HEADS = 16
RWKV_HD = 64
RWKV_PROJ = 3328
RWKV_GN_EPS = 64e-5
N_MEM = 256
X_HEADS = 4
X_HEAD_DIM = 256
RET_PROJ = 4096
SSM_PROJ = 2576
SSM_PROJ_PAD = 2688
GATE_PROJ = 3072
LANES = 128
VMEM_LIMIT = 56 * 1024 * 1024

RET_CHUNK = 128
SSD_CHUNK = 128
RWKV_CHUNK = 64
ATTN_ROWS = 2048
SEQ_BLOCK_ROWS = 64
CACHE_BLOCK_ROWS = 128
RWKV_DIAG_BLOCK = 16


def _params(sem):
    return pltpu.CompilerParams(dimension_semantics=sem, vmem_limit_bytes=VMEM_LIMIT)


def _rms(x, g):
    return x * lax.rsqrt(jnp.mean(x * x, axis=-1, keepdims=True) + EPS) * g


def _head_norm(x, eps):
    mu = jnp.mean(x, axis=-1, keepdims=True)
    xc = x - mu
    return xc * lax.rsqrt(jnp.mean(xc * xc, axis=-1, keepdims=True) + eps)


def _silu(x):
    return x * jax.nn.sigmoid(x)


def _dot(a, b):
    return jnp.dot(a.astype(BF16), b.astype(BF16), preferred_element_type=F32)


def _dot_nt(a, b):
    return lax.dot_general(a.astype(BF16), b.astype(BF16), (((1,), (1,)), ((), ())),
                           preferred_element_type=F32)


def _dot_tn(a, b):
    return lax.dot_general(a.astype(BF16), b.astype(BF16), (((0,), (0,)), ((), ())),
                           preferred_element_type=F32)


def _dot_f32(a, b):
    return jnp.dot(a, b, precision=lax.Precision.HIGHEST, preferred_element_type=F32)


def _seq_blocking(bsz, t, chunk_max, block_rows):
    if t > chunk_max:
        assert t % chunk_max == 0
        return 1, chunk_max
    nb = max(1, min(bsz, block_rows // t))
    assert bsz % nb == 0
    return nb, t


def _seq_masks(nb, c):
    r = nb * c
    i = np.arange(r)
    same = (i[:, None] // c) == (i[None, :] // c)
    incl = same & (i[:, None] >= i[None, :])
    strict = same & (i[:, None] > i[None, :])
    return same, incl, strict


def _nm_kernel(x_ref, g_ref, *refs):
    n = len(refs) // 2
    h = _rms(x_ref[...], g_ref[...]).astype(BF16)
    for w_ref, o_ref in zip(refs[:n], refs[n:]):
        o_ref[...] = jnp.dot(h, w_ref[...], preferred_element_type=F32)


def _norm_matmul(x, g, l, ws):
    m, d = x.shape
    tm = min(m, 512)
    row = lambda i: (i, 0)
    once = pl.Buffered(1)
    outs = pl.pallas_call(
        _nm_kernel,
        grid=(m // tm,),
        in_specs=[pl.BlockSpec((tm, d), row), pl.BlockSpec((None, 1, d), lambda i: (l, 0, 0))]
        + [pl.BlockSpec((None, d, w.shape[-1]), lambda i: (l, 0, 0), pipeline_mode=once) for w in ws],
        out_specs=[pl.BlockSpec((tm, w.shape[-1]), row) for w in ws],
        out_shape=[jax.ShapeDtypeStruct((m, w.shape[-1]), F32) for w in ws],
        compiler_params=_params(("parallel",)),
        name="norm_matmul",
    )(x, g, *ws)
    return outs


def _ffn_kernel(*refs, pre, final):
    x_ref, g_ref, w1_ref, w3_ref, w2_ref = refs[:5]
    extra = list(refs[5:-1])
    o_ref = refs[-1]
    x = x_ref[...]
    if pre:
        att_ref, wo_ref = extra[:2]
        extra = extra[2:]
        x = x + jnp.dot(att_ref[...].astype(BF16), wo_ref[...], preferred_element_type=F32)
    h = _rms(x, g_ref[...]).astype(BF16)
    a = jnp.dot(h, w1_ref[...], preferred_element_type=F32)
    b = jnp.dot(h, w3_ref[...], preferred_element_type=F32)
    u = (_silu(a) * b).astype(BF16)
    y = x + 0.5 * jnp.dot(u, w2_ref[...], preferred_element_type=F32)
    o_ref[...] = _rms(y, extra[0][...]) if final else y


def _ffn(x, g, l, w1, w3, w2, att=None, wo=None, final_g=None):
    m, d = x.shape
    f = w1.shape[-1]
    tm = min(m, 512)
    once = pl.Buffered(1)
    row = lambda i: (i, 0)
    in_specs = [
        pl.BlockSpec((tm, d), row),
        pl.BlockSpec((None, 1, d), lambda i: (l, 0, 0)),
        pl.BlockSpec((None, d, f), lambda i: (l, 0, 0), pipeline_mode=once),
        pl.BlockSpec((None, d, f), lambda i: (l, 0, 0), pipeline_mode=once),
        pl.BlockSpec((None, f, d), lambda i: (l, 0, 0), pipeline_mode=once),
    ]
    args = [x, g, w1, w3, w2]
    if att is not None:
        in_specs += [pl.BlockSpec((tm, d), row), pl.BlockSpec((None, d, d), lambda i: (l, 0, 0), pipeline_mode=once)]
        args += [att, wo]
    if final_g is not None:
        in_specs.append(pl.BlockSpec((1, d), lambda i: (0, 0)))
        args.append(final_g)
    return pl.pallas_call(
        functools.partial(_ffn_kernel, pre=att is not None, final=final_g is not None),
        grid=(m // tm,),
        in_specs=in_specs,
        out_specs=pl.BlockSpec((tm, d), row),
        out_shape=jax.ShapeDtypeStruct((m, d), F32),
        compiler_params=_params(("parallel",)),
        name="ffn",
    )(*args)


def _ret_kernel(p_ref, cos_ref, sin_ref, dmat_ref, qdec_ref, kdec_ref, st_in_ref, acc_ref,
                o_ref, st_ref, *, nq, nb, c, sdec):
    del acc_ref

    @pl.when(pl.program_id(1) == 0)
    def _():
        st_ref[...] = st_in_ref[...]

    cosf = cos_ref[...]
    sinf = sin_ref[...]
    hd = RET_DK
    r = nb * c
    units = [(su, h) for su in range(nq) for h in range(RET_HEADS)]
    lrows = [slice(sl * c, (sl + 1) * c) for sl in range(nb)]

    def slab(su, col0):
        return p_ref[su * nb:(su + 1) * nb, :, col0:col0 + hd].reshape(r, hd)

    qr, kr, v = [], [], []
    for su, h in units:
        q = slab(su, h * hd)
        k = slab(su, 1024 + h * hd)
        qr.append(((q * cosf + pltpu.roll(q, hd // 2, 1) * sinf) * (RET_DK ** -0.5)).astype(BF16))
        kr.append(k * cosf + pltpu.roll(k, hd // 2, 1) * sinf)
        v.append(slab(su, 2048 + h * hd).astype(BF16))
    scores = [_dot_nt(qr[k], kr[k]) * dmat_ref[h] for k, (_, h) in enumerate(units)]
    inter = [[_dot(qr[k][rw], st_ref[su * nb + sl, h]) for sl, rw in enumerate(lrows)]
             for k, (su, h) in enumerate(units)]
    o = [_dot(scores[k], v[k]) for k in range(len(units))]
    for k, (su, h) in enumerate(units):
        kd = kr[k] * kdec_ref[h]
        qdec = qdec_ref[h]
        for sl, rw in enumerate(lrows):
            s = su * nb + sl
            o_s = o[k][rw] + inter[k][sl] * qdec[rw]
            g = p_ref[s, :, 3072 + h * hd:3072 + (h + 1) * hd]
            o_ref[s, :, h * hd:(h + 1) * hd] = _silu(g) * _head_norm(o_s, RET_GN_EPS)
            st_ref[s, h] = st_ref[s, h] * sdec[h] + _dot_tn(kd[rw], v[k][rw])


def _retention(p_ret, st_all, l, bsz, t, offset, acc):
    nb, c = _seq_blocking(bsz, t, RET_CHUNK, SEQ_BLOCK_ROWS)
    r = nb * c
    nchunk = t // c
    heads = np.arange(RET_HEADS, dtype=np.float64)
    log_g = np.log1p(-np.exp2(-5.0 - heads))
    idx = np.arange(r) % c
    _, incl, _ = _seq_masks(nb, c)
    diff = (idx[:, None] - idx[None, :]).astype(np.float64)
    dmat = np.where(incl[None], np.exp(np.maximum(diff, 0.0)[None] * log_g[:, None, None]), 0.0)
    qdec = np.exp((idx + 1.0)[None, :] * log_g[:, None])
    kdec = np.exp((c - 1.0 - idx)[None, :] * log_g[:, None])
    qdec = np.broadcast_to(qdec[:, :, None], (RET_HEADS, r, RET_DK))
    kdec = np.broadcast_to(kdec[:, :, None], (RET_HEADS, r, RET_DK))
    sdec = tuple(float(x) for x in np.exp(c * log_g))

    half = RET_DK // 2
    freqs = ROPE_BASE ** (-jnp.arange(half, dtype=F32) / half)
    pos = jnp.float32(offset) + jnp.arange(t, dtype=F32)
    ang = pos[:, None] * freqs[None, :]
    cos = jnp.cos(ang)
    sin = jnp.sin(ang)
    cosf = jnp.tile(jnp.concatenate([cos, cos], axis=-1), (nb, 1))
    sinf = jnp.tile(jnp.concatenate([-sin, sin], axis=-1), (nb, 1))

    nq = 2 if (nb == 1 and bsz % 2 == 0) else 1
    nbt = nq * nb
    st_shape = (nbt, RET_HEADS, RET_DK, RET_DK)
    if st_all is None:
        st_in = jnp.zeros((bsz,) + st_shape[1:], F32)
        st_spec = pl.BlockSpec(st_shape, lambda i, j: (i, 0, 0, 0))
    else:
        st_in = st_all
        st_spec = pl.BlockSpec((None,) + st_shape, lambda i, j: (l, i, 0, 0, 0))
    out, st_new = pl.pallas_call(
        functools.partial(_ret_kernel, nq=nq, nb=nb, c=c, sdec=sdec),
        grid=(bsz // nbt, nchunk),
        in_specs=[
            pl.BlockSpec((nbt, c, RET_PROJ), lambda i, j: (i, j, 0)),
            pl.BlockSpec((r, RET_DK), lambda i, j: (j, 0)),
            pl.BlockSpec((r, RET_DK), lambda i, j: (j, 0)),
            pl.BlockSpec((RET_HEADS, r, r), lambda i, j: (0, 0, 0)),
            pl.BlockSpec((RET_HEADS, r, RET_DK), lambda i, j: (0, 0, 0)),
            pl.BlockSpec((RET_HEADS, r, RET_DK), lambda i, j: (0, 0, 0)),
            st_spec,
            pl.BlockSpec(memory_space=pl.ANY),
        ],
        out_specs=[
            pl.BlockSpec((nbt, c, 1024), lambda i, j: (i, j, 0)),
            pl.BlockSpec((None,) + st_shape, lambda i, j: (l, i, 0, 0, 0)),
        ],
        out_shape=[
            jax.ShapeDtypeStruct((bsz, t, 1024), F32),
            jax.ShapeDtypeStruct(acc.shape, F32),
        ],
        input_output_aliases={7: 1},
        compiler_params=_params(("parallel", "arbitrary")),
        name="retention",
    )(p_ret.reshape(bsz, t, RET_PROJ), cosf, sinf, jnp.asarray(dmat, F32), jnp.asarray(qdec, F32),
      jnp.asarray(kdec, F32), st_in, acc)
    return out.reshape(bsz * t, 1024), st_new


def _ssd_kernel(p_ref, cw_ref, cb_ref, dtb_ref, alog_ref, dfull_ref, norm_ref, tri_ref, mask2_ref,
                e64_ref, er_ref, conv_in_ref, st_in_ref, conv_acc_ref, st_acc_ref,
                o_ref, conv_out_ref, st_ref, ext_ref, stt_ref, *, nb, c):
    del conv_acc_ref, st_acc_ref
    r = nb * c
    hp = SSM_HEADDIM
    npair = SSM_HEADS // 2
    pairs = range(npair)
    seqs = range(nb)
    rows = [slice(s * c, (s + 1) * c) for s in seqs]
    cols = [slice(q * 2 * hp, (q + 1) * 2 * hp) for q in pairs]
    left = lax.broadcasted_iota(jnp.int32, (1, 2 * hp), 1) < hp

    @pl.when(pl.program_id(1) == 0)
    def _():
        for s in seqs:
            ext_ref[s, 5:8, :] = conv_in_ref[s]
            for q in pairs:
                both = jnp.concatenate([st_in_ref[s, 2 * q], st_in_ref[s, 2 * q + 1]], axis=0)
                stt_ref[s, q] = both.T

    for s in range(nb):
        ext_ref[s, 8:8 + c, :] = p_ref[s * c:(s + 1) * c, 1024:1024 + CONV_DIM]
    pieces = []
    for s in range(nb):
        acc = cb_ref[...] + ext_ref[s, 5:5 + c, :] * cw_ref[0:1, :]
        for j in range(1, CONV_W):
            acc = acc + ext_ref[s, 5 + j:5 + j + c, :] * cw_ref[j:j + 1, :]
        pieces.append(acc)
    for s in range(nb):
        conv_out_ref[s] = ext_ref[s, c + 5:c + 8, :]
        ext_ref[s, 0:8, :] = ext_ref[s, c:c + 8, :]
    xbc = _silu(pieces[0] if nb == 1 else jnp.concatenate(pieces, axis=0))
    xs = xbc[:, :1024]
    bm = xbc[:, 1024:1024 + SSM_GROUPS * SSM_STATE]
    cm = xbc[:, 1024 + SSM_GROUPS * SSM_STATE:]

    dt = jax.nn.softplus(p_ref[:, 1024 + CONV_DIM:] + dtb_ref[...])
    a = -jnp.exp(alog_ref[...])
    tri = tri_ref[...]
    cum = _dot_f32(tri, dt * a)
    cum_t = cum.T

    def expand(v, e_ref):
        hi = v.astype(BF16)
        r1 = v - hi.astype(F32)
        mid = r1.astype(BF16)
        lo = (r1 - mid.astype(F32)).astype(BF16)
        return jnp.dot(jnp.concatenate([hi, mid, lo], axis=1), e_ref[...], preferred_element_type=F32)

    def bd(v):
        vb = v.astype(BF16)
        zero = jnp.zeros_like(vb)
        return jnp.concatenate([jnp.where(left, vb, zero), jnp.where(left, zero, vb)], axis=0)

    def cat_rows(pieces):
        return pieces[0] if nb == 1 else jnp.concatenate(pieces, axis=0)

    cum_full = expand(cum, e64_ref)
    cum_wide = cum_full if r == hp else expand(cum, er_ref)
    xdt = xs * expand(dt, e64_ref)
    last = [cum_full[s * c + c - 1:s * c + c, :] for s in seqs]
    xw = cat_rows([xdt[rw] * jnp.exp(last[s] - cum_full[rw]) for s, rw in enumerate(rows)])
    eci = jnp.exp(cum_full)
    mask2 = mask2_ref[...] > 0.5

    hg = SSM_HEADS // SSM_GROUPS
    bm_g = [bm[:, g * SSM_STATE:(g + 1) * SSM_STATE] for g in range(SSM_GROUPS)]
    cm_g = [cm[:, g * SSM_STATE:(g + 1) * SSM_STATE] for g in range(SSM_GROUPS)]
    cb2 = []
    for g in range(SSM_GROUPS):
        cb = _dot_nt(cm_g[g], bm_g[g])
        cb2.append(jnp.concatenate([cb, cb], axis=1))
    grp = [2 * q // hg for q in pairs]
    inter = [cat_rows([_dot(cm_g[grp[q]][rw], stt_ref[s, q]) for s, rw in enumerate(rows)]) for q in pairs]
    sc = []
    for q in pairs:
        row2 = jnp.concatenate([cum_t[2 * q:2 * q + 1, :], cum_t[2 * q + 1:2 * q + 2, :]], axis=1)
        seg = cum_wide[:, q * 2 * r:(q + 1) * 2 * r] - row2
        sc.append(cb2[grp[q]] * jnp.where(mask2, jnp.exp(seg), 0.0))
    y_p = [jnp.dot(sc[q].astype(BF16), bd(xdt[:, cols[q]]), preferred_element_type=F32)
           + inter[q] * eci[:, cols[q]] for q in pairs]
    for q in pairs:
        for s, rw in enumerate(rows):
            upd = _dot_tn(bm_g[grp[q]][rw], xw[rw, cols[q]])
            stt_ref[s, q] = stt_ref[s, q] * jnp.exp(last[s][:, cols[q]]) + upd
    y = jnp.concatenate(y_p, axis=1) + dfull_ref[...] * xs
    z = p_ref[:, :1024]
    o_ref[...] = _rms(y * _silu(z), norm_ref[...])

    @pl.when(pl.program_id(1) == pl.num_programs(1) - 1)
    def _():
        for s in seqs:
            for q in pairs:
                both = stt_ref[s, q].T
                st_ref[s, 2 * q] = both[:hp]
                st_ref[s, 2 * q + 1] = both[hp:]


def _ssd(p_ssm, conv_all, st_all, l, bsz, t, prm, conv_acc, st_acc):
    nb, c = _seq_blocking(bsz, t, SSD_CHUNK, SEQ_BLOCK_ROWS)
    r = nb * c
    nchunk = t // c
    _, incl, _ = _seq_masks(nb, c)
    tri = jnp.asarray(incl.astype(np.float32))
    mask2 = jnp.asarray(np.tile(incl.astype(np.float32), (1, 2)))

    def expander(width):
        e = np.zeros((LANES, SSM_HEADS * width), np.float32)
        for h in range(SSM_HEADS):
            e[h, h * width:(h + 1) * width] = 1.0
        return jnp.asarray(np.concatenate([e, e, e], axis=0), BF16)

    e64 = expander(SSM_HEADDIM)
    e_r = expander(r)
    st_shape = (nb, SSM_HEADS, SSM_HEADDIM, SSM_STATE)
    cv_shape = (nb, CONV_W - 1, CONV_DIM)
    if st_all is None:
        st_in = jnp.zeros((bsz,) + st_shape[1:], F32)
        conv_in = jnp.zeros((bsz,) + cv_shape[1:], F32)
        st_spec = pl.BlockSpec(st_shape, lambda i, j: (i, 0, 0, 0))
        cv_spec = pl.BlockSpec(cv_shape, lambda i, j: (i, 0, 0))
    else:
        st_in, conv_in = st_all, conv_all
        st_spec = pl.BlockSpec((None,) + st_shape, lambda i, j: (l, i, 0, 0, 0))
        cv_spec = pl.BlockSpec((None,) + cv_shape, lambda i, j: (l, i, 0, 0))
    m = bsz * t

    def vec(n):
        return pl.BlockSpec((None, 1, n), lambda i, j: (l, 0, 0))

    return pl.pallas_call(
        functools.partial(_ssd_kernel, nb=nb, c=c),
        grid=(bsz // nb, nchunk),
        in_specs=[
            pl.BlockSpec((r, SSM_PROJ_PAD), lambda i, j: (i * nchunk + j, 0)),
            pl.BlockSpec((None, CONV_W, CONV_DIM), lambda i, j: (l, 0, 0)),
            vec(CONV_DIM), vec(LANES), vec(LANES), vec(1024), vec(1024),
            pl.BlockSpec((r, r), lambda i, j: (0, 0)),
            pl.BlockSpec((r, 2 * r), lambda i, j: (0, 0)),
            pl.BlockSpec(e64.shape, lambda i, j: (0, 0)),
            pl.BlockSpec(e_r.shape, lambda i, j: (0, 0)),
            cv_spec, st_spec,
            pl.BlockSpec(memory_space=pl.ANY), pl.BlockSpec(memory_space=pl.ANY),
        ],
        out_specs=[
            pl.BlockSpec((r, 1024), lambda i, j: (i * nchunk + j, 0)),
            pl.BlockSpec((None,) + cv_shape, lambda i, j: (l, i, 0, 0)),
            pl.BlockSpec((None,) + st_shape, lambda i, j: (l, i, 0, 0, 0)),
        ],
        out_shape=[
            jax.ShapeDtypeStruct((m, 1024), F32),
            jax.ShapeDtypeStruct(conv_acc.shape, F32),
            jax.ShapeDtypeStruct(st_acc.shape, F32),
        ],
        input_output_aliases={13: 1, 14: 2},
        scratch_shapes=[pltpu.VMEM((nb, 8 + c, CONV_DIM), F32),
                        pltpu.VMEM((nb, SSM_HEADS // 2, SSM_STATE, 2 * SSM_HEADDIM), F32)],
        compiler_params=_params(("parallel", "arbitrary")),
        name="ssd",
    )(p_ssm, prm["ssm_conv_w"], prm["ssm_conv_b"], prm["ssm_dt_bias"], prm["ssm_a_log"],
      prm["ssm_d"], prm["ssm_norm"], tri, mask2, e64, e_r, conv_in, st_in, conv_acc, st_acc)


def _rwkv_kernel(p_ref, mu_ref, w0_ref, ww2_ref, a0_ref, wa2_ref, wg2_ref, kk_ref, ka_ref,
                 rk_ref, lng_ref, lnb_ref, masks_ref, tri_ref, shift_in_ref, st_in_ref, shift_acc_ref,
                 st_acc_ref, o_ref, shift_out_ref, st_ref, ext_ref, bd_ref, *, nq, nb, c):
    del shift_acc_ref, st_acc_ref
    r = nb * c
    hd = RWKV_HD
    npair = RWKV_HEADS // 2
    seqs = range(nq * nb)
    rows = [slice(s * c, (s + 1) * c) for s in seqs]
    lane = lax.broadcasted_iota(jnp.int32, (1, 2 * hd), 1)
    left = lane < hd
    sub = lax.broadcasted_iota(jnp.int32, (2 * hd, 1), 0)
    bd_mask = (sub < hd) == left

    @pl.when(pl.program_id(1) == 0)
    def _():
        zero = jnp.zeros((hd, hd), F32)
        for s in seqs:
            ext_ref[s, 7:8, :] = shift_in_ref[s]
            for p in range(npair):
                top = jnp.concatenate([st_in_ref[s, 2 * p], zero], axis=1)
                bot = jnp.concatenate([zero, st_in_ref[s, 2 * p + 1]], axis=1)
                bd_ref[s, p] = jnp.concatenate([top, bot], axis=0)

    x = p_ref[...].reshape(nq * r, RWKV_PROJ)
    for s in seqs:
        ext_ref[s, 8:8 + c, :] = x[rows[s]]
    prev = [ext_ref[s, 7:7 + c, :] for s in seqs]
    prev = prev[0] if len(prev) == 1 else jnp.concatenate(prev, axis=0)
    for s in seqs:
        ext_ref[s, 0:8, :] = ext_ref[s, c:c + 8, :]
        shift_out_ref[s] = ext_ref[s, 7:8, :]

    mixed = x + (prev - x) * mu_ref[...]
    rr = mixed[:, 0:1024]
    kc = mixed[:, 1024:2048]
    vc = mixed[:, 2048:3072]
    wd = mixed[:, 3072:3136]
    ad = mixed[:, 3136:3200]
    gd = mixed[:, 3200:3328]
    w_log = -jax.nn.softplus(-(w0_ref[...] + _dot(jnp.tanh(wd), ww2_ref[...]))) - 0.5
    lw = -jnp.exp(w_log)
    a_lr = jax.nn.sigmoid(a0_ref[...] + _dot(ad, wa2_ref[...]))
    g_rw = _dot(jax.nn.sigmoid(gd), wg2_ref[...])
    kk = kc * kk_ref[...]
    kmod = kc * (1.0 + (a_lr - 1.0) * ka_ref[...])
    rkk = rr * kmod * rk_ref[...]

    m_incl = masks_ref[0]
    m_strict = masks_ref[1]
    m_blk = masks_ref[2]
    eye = masks_ref[3]
    cl = _dot_f32(tri_ref[...], lw)

    def half_sum(a):
        sl = jnp.sum(jnp.where(left, a, 0.0), axis=-1, keepdims=True)
        sr = jnp.sum(jnp.where(left, 0.0, a), axis=-1, keepdims=True)
        return jnp.where(left, sl, sr)

    def bd(a):
        ab = a.astype(BF16)
        zero = jnp.zeros_like(ab)
        return jnp.concatenate([jnp.where(left, ab, zero), jnp.where(left, zero, ab)], axis=0)

    def lp_dot(a, b):
        return jnp.dot(a.astype(BF16), bd(b), preferred_element_type=F32)

    def cat_rows(pieces):
        return pieces[0] if nb == 1 else jnp.concatenate(pieces, axis=0)

    units = [(su, p) for su in range(nq) for p in range(npair)]
    pairs = range(len(units))
    ucols = [slice(p * 2 * hd, (p + 1) * 2 * hd) for _, p in units]
    urows = [slice(su * r, (su + 1) * r) for su, _ in units]
    lrows = [slice(sl * c, (sl + 1) * c) for sl in range(nb)]
    v_p = [vc[ur, cs] for ur, cs in zip(urows, ucols)]
    kt, rt, kh, bh, e_last = [], [], [], [], []
    for ur, cs in zip(urows, ucols):
        kk_p = kk[ur, cs]
        kk_p = kk_p * lax.rsqrt(jnp.maximum(half_sum(kk_p * kk_p), 1e-24))
        cl_p = cl[ur, cs]
        e_incl = jnp.exp(cl_p)
        e_inv = jnp.exp(-cl_p)
        rt.append(rr[ur, cs] * e_incl)
        kt.append(kk_p * jnp.exp(cl_p - lw[ur, cs]))
        kh.append(kmod[ur, cs] * e_inv)
        bh.append(kk_p * a_lr[ur, cs] * e_inv)
        e_last.append([e_incl[sl * c + c - 1:sl * c + c, :] for sl in range(nb)])
    lhs = [jnp.concatenate([kt[p], rt[p]], axis=0).astype(BF16) for p in pairs]
    gk = [_dot_nt(lhs[p], bd(kh[p])) for p in pairs]
    gb = [_dot_nt(lhs[p], bd(bh[p])) for p in pairs]
    both = [[_dot_nt(jnp.concatenate([kt[k][rw], rt[k][rw]], axis=0) if nb > 1 else lhs[k],
                     bd_ref[su * nb + sl, p]) for sl, rw in enumerate(lrows)]
            for k, (su, p) in enumerate(units)]
    a_kk = [g[:r] * m_strict for g in gk]
    a_rk = [g[r:] * m_incl for g in gk]
    a_kb = [g[:r] * m_strict for g in gb]
    a_rb = [g[r:] * m_incl for g in gb]
    akv = [lp_dot(a_kk[p], v_p[p]) for p in pairs]

    n1 = [a * m_blk for a in a_kb]
    n2 = [lp_dot(n, n) for n in n1]
    p1 = [lp_dot(eye - n1[p], eye + n2[p]) for p in pairs]
    n4 = [lp_dot(n, n) for n in n2]
    p2 = [lp_dot(p1[p], eye + n4[p]) for p in pairs]
    n8 = [lp_dot(n, n) for n in n4]
    dinv = [lp_dot(p2[p], eye + n8[p]) for p in pairs]
    xm = [lp_dot(dinv[p], a_kb[p] - n1[p]) for p in pairs]
    x2 = [lp_dot(x_, x_) for x_ in xm]
    t1 = [lp_dot(eye - xm[p], eye + x2[p]) for p in pairs]
    tinv = [lp_dot(t1[p], dinv[p]) for p in pairs]

    ks = [cat_rows([b[:c] for b in both[p]]) for p in pairs]
    rs = [cat_rows([b[c:] for b in both[p]]) for p in pairs]
    u = [lp_dot(tinv[p], ks[p] + akv[p]) for p in pairs]
    y = [rs[p] + lp_dot(a_rk[p], v_p[p]) - lp_dot(a_rb[p], u[p]) for p in pairs]
    for k, (su, p) in enumerate(units):
        for sl, rw in enumerate(lrows):
            s = su * nb + sl
            upd = _dot_tn(jnp.concatenate([v_p[k][rw], -u[k][rw]], axis=0),
                          jnp.concatenate([kh[k][rw], bh[k][rw]], axis=0))
            bd_ref[s, p] = (bd_ref[s, p] + jnp.where(bd_mask, upd, 0.0)) * e_last[k][sl]
    for k, (su, p) in enumerate(units):
        ur, cs = urows[k], ucols[k]
        bonus = half_sum(rkk[ur, cs]) * v_p[k]
        yc = y[k] - half_sum(y[k]) * (1.0 / hd)
        yn = yc * lax.rsqrt(half_sum(yc * yc) * (1.0 / hd) + RWKV_GN_EPS)
        out = (yn * lng_ref[:, cs] + lnb_ref[:, cs] + bonus) * g_rw[ur, cs]
        o_ref[su * nb:(su + 1) * nb, :, cs] = out.reshape(nb, c, 2 * hd)

    @pl.when(pl.program_id(1) == pl.num_programs(1) - 1)
    def _():
        for s in seqs:
            for p in range(npair):
                blk = bd_ref[s, p]
                st_ref[s, 2 * p] = blk[:hd, :hd]
                st_ref[s, 2 * p + 1] = blk[hd:, hd:]


def _rwkv(p_rwkv, shift_all, st_all, l, bsz, t, prm, shift_acc, st_acc):
    nb, c = _seq_blocking(bsz, t, RWKV_CHUNK, SEQ_BLOCK_ROWS)
    r = nb * c
    nchunk = t // c
    assert 2 * r == LANES, "the lane-paired layout holds two (r, r) matrices side by side"
    nq = next(n for n in ((4, 2, 1) if nb == 1 else (2, 1)) if bsz % (n * nb) == 0)
    nbt = nq * nb
    _, incl, strict = _seq_masks(nb, c)
    tri = jnp.asarray(_seq_masks(nbt, c)[1].astype(np.float32))
    i = np.arange(r)
    blk = (i[:, None] // RWKV_DIAG_BLOCK) == (i[None, :] // RWKV_DIAG_BLOCK)
    masks = np.stack([incl, strict, blk, np.eye(r, dtype=bool)]).astype(np.float32)
    masks = jnp.asarray(np.tile(masks, (1, 1, 2)))
    st_shape = (nbt, RWKV_HEADS, RWKV_HD, RWKV_HD)
    sh_shape = (nbt, 1, RWKV_PROJ)
    if st_all is None:
        st_in = jnp.zeros((bsz,) + st_shape[1:], F32)
        shift_in = jnp.zeros((bsz,) + sh_shape[1:], F32)
        st_spec = pl.BlockSpec(st_shape, lambda i, j: (i, 0, 0, 0))
        sh_spec = pl.BlockSpec(sh_shape, lambda i, j: (i, 0, 0))
    else:
        st_in, shift_in = st_all, shift_all
        st_spec = pl.BlockSpec((None,) + st_shape, lambda i, j: (l, i, 0, 0, 0))
        sh_spec = pl.BlockSpec((None,) + sh_shape, lambda i, j: (l, i, 0, 0))

    def vec(n):
        return pl.BlockSpec((None, 1, n), lambda i, j: (l, 0, 0))

    def mat(k):
        return pl.BlockSpec((None, k, 1024), lambda i, j: (l, 0, 0))

    out, shift_new, st_new = pl.pallas_call(
        functools.partial(_rwkv_kernel, nq=nq, nb=nb, c=c),
        grid=(bsz // nbt, nchunk),
        in_specs=[
            pl.BlockSpec((nbt, c, RWKV_PROJ), lambda i, j: (i, j, 0)),
            vec(RWKV_PROJ), vec(1024), mat(64), vec(1024), mat(64), mat(128),
            vec(1024), vec(1024), vec(1024), vec(1024), vec(1024),
            pl.BlockSpec((4, r, 2 * r), lambda i, j: (0, 0, 0)),
            pl.BlockSpec((nq * r, nq * r), lambda i, j: (0, 0)),
            sh_spec, st_spec,
            pl.BlockSpec(memory_space=pl.ANY), pl.BlockSpec(memory_space=pl.ANY),
        ],
        out_specs=[
            pl.BlockSpec((nbt, c, 1024), lambda i, j: (i, j, 0)),
            pl.BlockSpec((None,) + sh_shape, lambda i, j: (l, i, 0, 0)),
            pl.BlockSpec((None,) + st_shape, lambda i, j: (l, i, 0, 0, 0)),
        ],
        out_shape=[
            jax.ShapeDtypeStruct((bsz, t, 1024), F32),
            jax.ShapeDtypeStruct(shift_acc.shape, F32),
            jax.ShapeDtypeStruct(st_acc.shape, F32),
        ],
        input_output_aliases={16: 1, 17: 2},
        scratch_shapes=[pltpu.VMEM((nbt, 8 + c, RWKV_PROJ), F32),
                        pltpu.VMEM((nbt, RWKV_HEADS // 2, 2 * RWKV_HD, 2 * RWKV_HD), F32)],
        compiler_params=_params(("parallel", "arbitrary")),
        name="rwkv7",
    )(p_rwkv.reshape(bsz, t, RWKV_PROJ), prm["rwkv_mu"], prm["rwkv_w0"], prm["rwkv_w_w2"], prm["rwkv_a0"],
      prm["rwkv_w_a2"], prm["rwkv_w_g2"], prm["rwkv_k_k"], prm["rwkv_k_a"], prm["rwkv_r_k"], prm["rwkv_lnx_g"],
      prm["rwkv_lnx_b"], masks, tri, shift_in, st_in, shift_acc, st_acc)
    return out.reshape(bsz * t, 1024), shift_new, st_new


def _merge_kernel(a_ref, b_ref, c_ref, g_ref, x_ref, wb_ref, wo_ref, qn_ref, wq_ref, o_ref, q_ref):
    merged = None
    for i, br in enumerate((a_ref, b_ref, c_ref)):
        gate = jax.nn.sigmoid(g_ref[:, i * 1024:(i + 1) * 1024])
        term = gate * jnp.dot(br[...].astype(BF16), wb_ref[i], preferred_element_type=F32)
        merged = term if merged is None else merged + term
    x = x_ref[...] + jnp.dot(merged.astype(BF16), wo_ref[...], preferred_element_type=F32)
    o_ref[...] = x
    q_ref[...] = jnp.dot(_rms(x, qn_ref[...]).astype(BF16), wq_ref[...], preferred_element_type=F32)


def _merge(out_a, out_b, out_c, gates, x, l, wb, wo, qn, wq):
    m, d = x.shape
    tm = min(m, 512)
    row = lambda i: (i, 0)
    once = pl.Buffered(1)
    return pl.pallas_call(
        _merge_kernel,
        grid=(m // tm,),
        in_specs=[
            pl.BlockSpec((tm, d), row), pl.BlockSpec((tm, d), row), pl.BlockSpec((tm, d), row),
            pl.BlockSpec((tm, GATE_PROJ), row), pl.BlockSpec((tm, d), row),
            pl.BlockSpec((None, 3, d, d), lambda i: (l, 0, 0, 0), pipeline_mode=once),
            pl.BlockSpec((None, d, d), lambda i: (l, 0, 0), pipeline_mode=once),
            pl.BlockSpec((None, 1, d), lambda i: (l, 0, 0)),
            pl.BlockSpec((None, d, d), lambda i: (l, 0, 0), pipeline_mode=once),
        ],
        out_specs=[pl.BlockSpec((tm, d), row), pl.BlockSpec((tm, d), row)],
        out_shape=[jax.ShapeDtypeStruct((m, d), F32), jax.ShapeDtypeStruct((m, d), F32)],
        compiler_params=_params(("parallel",)),
        name="merge",
    )(out_a, out_b, out_c, gates, x, wb, wo, qn, wq)


def _attn_kernel(q_ref, k_ref, v_ref, o_ref, *, nb, c):
    rows = [slice(s * c, (s + 1) * c) for s in range(nb)]
    sc = [_dot_nt(q_ref[rows[s], :], k_ref[s]) * (X_HEAD_DIM ** -0.5) for s in range(nb)]
    e = [jnp.exp(x - jnp.max(x, axis=-1, keepdims=True)) for x in sc]
    pr = [x * (1.0 / jnp.sum(x, axis=-1, keepdims=True)) for x in e]
    for s in range(nb):
        o_ref[rows[s], :] = _dot(pr[s], v_ref[s])


def _attn_cache_kernel(q_ref, k_hbm, v_hbm, o_ref, kbuf, vbuf, sem, *, l, nb, c):
    nh = pl.num_programs(1)
    step = pl.program_id(0) * nh + pl.program_id(1)
    nsteps = pl.num_programs(0) * nh
    slot = step % 2

    def copies(at_step, sl):
        blk, h = at_step // nh, at_step % nh
        src = (l, pl.ds(blk * nb, nb), slice(None), h, slice(None))
        return (pltpu.make_async_copy(k_hbm.at[src], kbuf.at[sl], sem.at[0, sl]),
                pltpu.make_async_copy(v_hbm.at[src], vbuf.at[sl], sem.at[1, sl]))

    @pl.when(step == 0)
    def _():
        for cp in copies(step, slot):
            cp.start()

    @pl.when(step + 1 < nsteps)
    def _():
        for cp in copies(step + 1, 1 - slot):
            cp.start()

    for cp in copies(step, slot):
        cp.wait()
    _attn_kernel(q_ref, kbuf.at[slot], vbuf.at[slot], o_ref, nb=nb, c=c)


def _attn_cache(q, ck, cv, l, bsz, t):
    nb, c = max(1, min(bsz, CACHE_BLOCK_ROWS // t)), t
    r = nb * c
    hd = X_HEAD_DIM
    return pl.pallas_call(
        functools.partial(_attn_cache_kernel, l=l, nb=nb, c=c),
        grid=(bsz // nb, X_HEADS),
        in_specs=[pl.BlockSpec((r, hd), lambda i, h: (i, h)),
                  pl.BlockSpec(memory_space=pl.ANY), pl.BlockSpec(memory_space=pl.ANY)],
        out_specs=pl.BlockSpec((r, hd), lambda i, h: (i, h)),
        out_shape=jax.ShapeDtypeStruct((bsz * t, 1024), F32),
        scratch_shapes=[pltpu.VMEM((2, nb, N_MEM, hd), F32), pltpu.VMEM((2, nb, N_MEM, hd), F32),
                        pltpu.SemaphoreType.DMA((2, 2))],
        compiler_params=_params(("arbitrary", "arbitrary")),
        name="xattn_cache",
    )(q, ck, cv)


def _attn(q, mk, mv, bsz, t):
    if t > ATTN_ROWS:
        nb, c = 1, ATTN_ROWS
    else:
        nb, c = max(1, min(bsz, SEQ_BLOCK_ROWS // t)), t
    r = nb * c
    ntile = t // c
    hd = X_HEAD_DIM
    kv_spec = pl.BlockSpec((nb, N_MEM, hd), lambda i, j, h: (i, 0, h))
    return pl.pallas_call(
        functools.partial(_attn_kernel, nb=nb, c=c),
        grid=(bsz // nb, ntile, X_HEADS),
        in_specs=[pl.BlockSpec((r, hd), lambda i, j, h: (i * ntile + j, h)), kv_spec, kv_spec],
        out_specs=pl.BlockSpec((r, hd), lambda i, j, h: (i * ntile + j, h)),
        out_shape=jax.ShapeDtypeStruct((bsz * t, 1024), F32),
        compiler_params=_params(("parallel", "arbitrary", "arbitrary")),
        name="xattn_core",
    )(q, mk, mv)


def _trunk(x, prm, states, mem_k, mem_v, bsz, t, offset):
    acc_ret = jnp.zeros((DEPTH, bsz, RET_HEADS, RET_DK, RET_DK), F32)
    acc_ssm = jnp.zeros((DEPTH, bsz, SSM_HEADS, SSM_HEADDIM, SSM_STATE), F32)
    acc_conv = jnp.zeros((DEPTH, bsz, CONV_W - 1, CONV_DIM), F32)
    acc_rwkv = jnp.zeros((DEPTH, bsz, RWKV_HEADS, RWKV_HD, RWKV_HD), F32)
    acc_shift = jnp.zeros((DEPTH, bsz, 1, RWKV_PROJ), F32)
    for l in range(DEPTH):
        x = _ffn(x, prm["ffa_norm"], l, prm["ffa_w1"], prm["ffa_w3"], prm["ffa_w2"])
        p_ret, p_gate = _norm_matmul(x, prm["mix_norm"], l, [prm["w_ret"], prm["w_gate"]])
        p_ssm, p_rwkv = _norm_matmul(x, prm["mix_norm"], l, [prm["w_ssm"], prm["w_rwkv"]])
        if states is None:
            s_ret = s_ssm = s_conv = s_rwkv = s_shift = None
        else:
            s_ret, s_ssm, s_conv, s_rwkv, s_shift = states
        out_a, acc_ret = _retention(p_ret, s_ret, l, bsz, t, offset, acc_ret)
        out_b, acc_conv, acc_ssm = _ssd(p_ssm, s_conv, s_ssm, l, bsz, t, prm, acc_conv, acc_ssm)
        out_c, acc_shift, acc_rwkv = _rwkv(p_rwkv, s_shift, s_rwkv, l, bsz, t, prm, acc_shift, acc_rwkv)
        x, q = _merge(out_a, out_b, out_c, p_gate, x, l, prm["w_branch"], prm["w_out"],
                      prm["xattn_norm"], prm["xattn_wq"])
        if isinstance(mem_k, (list, tuple)):
            att = _attn(q, mem_k[l], mem_v[l], bsz, t)
        else:
            att = _attn_cache(q, mem_k, mem_v, l, bsz, t)
        x = _ffn(x, prm["ffb_norm"], l, prm["ffb_w1"], prm["ffb_w3"], prm["ffb_w2"], att=att, wo=prm["xattn_wo"],
                 final_g=prm["final_norm"] if l == DEPTH - 1 else None)
    return x, (acc_ret, acc_ssm, acc_conv, acc_rwkv, acc_shift)


def _prep_params(raw):
    p = {}
    for name in ("ffa_w1", "ffa_w3", "ffa_w2", "ffb_w1", "ffb_w3", "ffb_w2", "w_branch", "w_out",
                 "xattn_wq", "xattn_wk", "xattn_wv", "xattn_wo", "rwkv_w_w2", "rwkv_w_a2", "rwkv_w_g2"):
        p[name] = raw[name].astype(BF16)
    w_in = raw["w_in"]
    c0, c1, c2 = RET_PROJ, RET_PROJ + SSM_PROJ, RET_PROJ + SSM_PROJ + RWKV_PROJ
    p["w_ret"] = w_in[:, :, :c0].astype(BF16)
    p["w_ssm"] = jnp.pad(w_in[:, :, c0:c1], ((0, 0), (0, 0), (0, SSM_PROJ_PAD - SSM_PROJ))).astype(BF16)
    p["w_rwkv"] = w_in[:, :, c1:c2].astype(BF16)
    p["w_gate"] = w_in[:, :, c2:].astype(BF16)
    for name in ("ffa_norm", "mix_norm", "ssm_conv_b", "ssm_norm", "rwkv_mu", "rwkv_w0", "rwkv_a0",
                 "rwkv_k_k", "rwkv_k_a", "rwkv_lnx_g", "rwkv_lnx_b", "xattn_norm", "mem_norm", "ffb_norm"):
        p[name] = raw[name][:, None, :]
    p["rwkv_r_k"] = raw["rwkv_r_k"].reshape(DEPTH, 1, 1024)
    pad = ((0, 0), (0, LANES - SSM_HEADS))
    p["ssm_dt_bias"] = jnp.pad(raw["ssm_dt_bias"], pad)[:, None, :]
    p["ssm_a_log"] = jnp.pad(raw["ssm_a_log"], pad)[:, None, :]
    p["ssm_d"] = jnp.repeat(raw["ssm_d"], SSM_HEADDIM, axis=-1)[:, None, :]
    p["ssm_conv_w"] = raw["ssm_conv_w"]
    p["final_norm"] = raw["final_norm"][None, :]
    return p


def _run(x_prompt, x_sample, mem_prompt, states, cache_mem_k, cache_mem_v, raw):
    prm = _prep_params(raw)
    b, t, d = x_prompt.shape
    db, dt_, _ = x_sample.shape
    n_mem = mem_prompt.shape[1]
    mem2 = mem_prompt.reshape(b * n_mem, d)
    mkv = [_norm_matmul(mem2, prm["mem_norm"], l, [prm["xattn_wk"], prm["xattn_wv"]]) for l in range(DEPTH)]
    mk = [kv[0] for kv in mkv]
    mv = [kv[1] for kv in mkv]
    mk3 = [a.reshape(b, n_mem, d) for a in mk]
    mv3 = [a.reshape(b, n_mem, d) for a in mv]
    p_mem_k = jnp.stack(mk).reshape(DEPTH, b, n_mem, X_HEADS, X_HEAD_DIM)
    p_mem_v = jnp.stack(mv).reshape(DEPTH, b, n_mem, X_HEADS, X_HEAD_DIM)
    y_p, st_p = _trunk(x_prompt.reshape(b * t, d), prm, None, mk3, mv3, b, t, 0)

    y_s, st_s = _trunk(x_sample.reshape(db * dt_, d), prm, states, cache_mem_k, cache_mem_v, db, dt_, PAST_LEN)
    return (y_p.reshape(b, t, d), y_s.reshape(db, dt_, d)) + st_p + (p_mem_k, p_mem_v) + st_s


def kernel(x_prompt, x_sample, mem_prompt, state_ret, state_ssm, state_conv, state_rwkv, state_shift, cache_mem_k, cache_mem_v, ffa_norm, ffa_w1, ffa_w3, ffa_w2, mix_norm, w_in, ssm_conv_w, ssm_conv_b, ssm_dt_bias, ssm_a_log, ssm_d, ssm_norm, rwkv_mu, rwkv_w0, rwkv_w_w2, rwkv_a0, rwkv_w_a2, rwkv_w_g2, rwkv_k_k, rwkv_k_a, rwkv_r_k, rwkv_lnx_g, rwkv_lnx_b, w_branch, w_out, xattn_norm, mem_norm, xattn_wq, xattn_wk, xattn_wv, xattn_wo, ffb_norm, ffb_w1, ffb_w3, ffb_w2, final_norm):
    raw = dict(ffa_norm=ffa_norm, ffa_w1=ffa_w1, ffa_w3=ffa_w3, ffa_w2=ffa_w2, mix_norm=mix_norm, w_in=w_in,
               ssm_conv_w=ssm_conv_w, ssm_conv_b=ssm_conv_b, ssm_dt_bias=ssm_dt_bias, ssm_a_log=ssm_a_log,
               ssm_d=ssm_d, ssm_norm=ssm_norm, rwkv_mu=rwkv_mu, rwkv_w0=rwkv_w0, rwkv_w_w2=rwkv_w_w2,
               rwkv_a0=rwkv_a0, rwkv_w_a2=rwkv_w_a2, rwkv_w_g2=rwkv_w_g2, rwkv_k_k=rwkv_k_k,
               rwkv_k_a=rwkv_k_a, rwkv_r_k=rwkv_r_k, rwkv_lnx_g=rwkv_lnx_g, rwkv_lnx_b=rwkv_lnx_b,
               w_branch=w_branch, w_out=w_out, xattn_norm=xattn_norm, mem_norm=mem_norm,
               xattn_wq=xattn_wq, xattn_wk=xattn_wk, xattn_wv=xattn_wv, xattn_wo=xattn_wo,
               ffb_norm=ffb_norm, ffb_w1=ffb_w1, ffb_w3=ffb_w3, ffb_w2=ffb_w2, final_norm=final_norm)
    states = (state_ret, state_ssm, state_conv, state_rwkv, state_shift)
    return _run(x_prompt, x_sample, mem_prompt, states, cache_mem_k, cache_mem_v, raw)
```

```python
import functools

import numpy as np
import jax
import jax.numpy as jnp
from jax import lax
from jax.experimental import pallas as pl
from jax.experimental.pallas import tpu as pltpu

F32 = jnp.float32
BF16 = jnp.bfloat16

DEPTH = 4
PAST_LEN = 16384
EPS = 1e-6
RET_HEADS = 8
RET_DK = 128
RET_GN_EPS = 1e-5
ROPE_BASE = 10000.0
SSM_HEADS = 16
SSM_HEADDIM = 64
SSM_GROUPS = 2
SSM_STATE = 128
CONV_W = 4
CONV_DIM = 1536
RWKV_HEADS = 16
RWKV_HD = 64
RWKV_PROJ = 3328
RWKV_GN_EPS = 64e-5
N_MEM = 256
X_HEADS = 4
X_HEAD_DIM = 256
RET_PROJ = 4096
SSM_PROJ = 2576
SSM_PROJ_PAD = 2688
GATE_PROJ = 3072
LANES = 128
VMEM_LIMIT = 56 * 1024 * 1024

RET_CHUNK = 128
SSD_CHUNK = 128
RWKV_CHUNK = 64
ATTN_ROWS = 2048
SEQ_BLOCK_ROWS = 64
CACHE_BLOCK_ROWS = 128
RWKV_DIAG_BLOCK = 16


def _params(sem):
    return pltpu.CompilerParams(dimension_semantics=sem, vmem_limit_bytes=VMEM_LIMIT)


def _rms(x, g):
    return x * lax.rsqrt(jnp.mean(x * x, axis=-1, keepdims=True) + EPS) * g


def _head_norm(x, eps):
    mu = jnp.mean(x, axis=-1, keepdims=True)
    xc = x - mu
    return xc * lax.rsqrt(jnp.mean(xc * xc, axis=-1, keepdims=True) + eps)


def _silu(x):
    return x * jax.nn.sigmoid(x)


def _dot(a, b):
    return jnp.dot(a.astype(BF16), b.astype(BF16), preferred_element_type=F32)


def _dot_nt(a, b):
    return lax.dot_general(a.astype(BF16), b.astype(BF16), (((1,), (1,)), ((), ())),
                           preferred_element_type=F32)


def _dot_tn(a, b):
    return lax.dot_general(a.astype(BF16), b.astype(BF16), (((0,), (0,)), ((), ())),
                           preferred_element_type=F32)


def _dot_f32(a, b):
    return jnp.dot(a, b, precision=lax.Precision.HIGHEST, preferred_element_type=F32)


def _seq_blocking(bsz, t, chunk_max, block_rows):
    if t > chunk_max:
        assert t % chunk_max == 0
        return 1, chunk_max
    nb = max(1, min(bsz, block_rows // t))
    assert bsz % nb == 0
    return nb, t


def _seq_masks(nb, c):
    r = nb * c
    i = np.arange(r)
    same = (i[:, None] // c) == (i[None, :] // c)
    incl = same & (i[:, None] >= i[None, :])
    strict = same & (i[:, None] > i[None, :])
    return same, incl, strict


def _nm_kernel(x_ref, g_ref, *refs):
    n = len(refs) // 2
    h = _rms(x_ref[...], g_ref[...]).astype(BF16)
    for w_ref, o_ref in zip(refs[:n], refs[n:]):
        o_ref[...] = jnp.dot(h, w_ref[...], preferred_element_type=F32)


def _norm_matmul(x, g, l, ws):
    m, d = x.shape
    tm = min(m, 512)
    row = lambda i: (i, 0)
    once = pl.Buffered(1)
    outs = pl.pallas_call(
        _nm_kernel,
        grid=(m // tm,),
        in_specs=[pl.BlockSpec((tm, d), row), pl.BlockSpec((None, 1, d), lambda i: (l, 0, 0))]
        + [pl.BlockSpec((None, d, w.shape[-1]), lambda i: (l, 0, 0), pipeline_mode=once) for w in ws],
        out_specs=[pl.BlockSpec((tm, w.shape[-1]), row) for w in ws],
        out_shape=[jax.ShapeDtypeStruct((m, w.shape[-1]), F32) for w in ws],
        compiler_params=_params(("parallel",)),
        name="norm_matmul",
    )(x, g, *ws)
    return outs


def _ffn_kernel(*refs, pre, final):
    x_ref, g_ref, w1_ref, w3_ref, w2_ref = refs[:5]
    extra = list(refs[5:-1])
    o_ref = refs[-1]
    x = x_ref[...]
    if pre:
        att_ref, wo_ref = extra[:2]
        extra = extra[2:]
        x = x + jnp.dot(att_ref[...].astype(BF16), wo_ref[...], preferred_element_type=F32)
    h = _rms(x, g_ref[...]).astype(BF16)
    a = jnp.dot(h, w1_ref[...], preferred_element_type=F32)
    b = jnp.dot(h, w3_ref[...], preferred_element_type=F32)
    u = (_silu(a) * b).astype(BF16)
    y = x + 0.5 * jnp.dot(u, w2_ref[...], preferred_element_type=F32)
    o_ref[...] = _rms(y, extra[0][...]) if final else y


def _ffn(x, g, l, w1, w3, w2, att=None, wo=None, final_g=None):
    m, d = x.shape
    f = w1.shape[-1]
    tm = min(m, 512)
    once = pl.Buffered(1)
    row = lambda i: (i, 0)
    in_specs = [
        pl.BlockSpec((tm, d), row),
        pl.BlockSpec((None, 1, d), lambda i: (l, 0, 0)),
        pl.BlockSpec((None, d, f), lambda i: (l, 0, 0), pipeline_mode=once),
        pl.BlockSpec((None, d, f), lambda i: (l, 0, 0), pipeline_mode=once),
        pl.BlockSpec((None, f, d), lambda i: (l, 0, 0), pipeline_mode=once),
    ]
    args = [x, g, w1, w3, w2]
    if att is not None:
        in_specs += [pl.BlockSpec((tm, d), row), pl.BlockSpec((None, d, d), lambda i: (l, 0, 0), pipeline_mode=once)]
        args += [att, wo]
    if final_g is not None:
        in_specs.append(pl.BlockSpec((1, d), lambda i: (0, 0)))
        args.append(final_g)
    return pl.pallas_call(
        functools.partial(_ffn_kernel, pre=att is not None, final=final_g is not None),
        grid=(m // tm,),
        in_specs=in_specs,
        out_specs=pl.BlockSpec((tm, d), row),
        out_shape=jax.ShapeDtypeStruct((m, d), F32),
        compiler_params=_params(("parallel",)),
        name="ffn",
    )(*args)


def _ret_kernel(p_ref, cos_ref, sin_ref, dmat_ref, qdec_ref, kdec_ref, st_in_ref, acc_ref,
                o_ref, st_ref, *, nq, nb, c, sdec):
    del acc_ref

    @pl.when(pl.program_id(1) == 0)
    def _():
        st_ref[...] = st_in_ref[...]

    cosf = cos_ref[...]
    sinf = sin_ref[...]
    hd = RET_DK
    r = nb * c
    units = [(su, h) for su in range(nq) for h in range(RET_HEADS)]
    lrows = [slice(sl * c, (sl + 1) * c) for sl in range(nb)]

    def slab(su, col0):
        return p_ref[su * nb:(su + 1) * nb, :, col0:col0 + hd].reshape(r, hd)

    qr, kr, v = [], [], []
    for su, h in units:
        q = slab(su, h * hd)
        k = slab(su, 1024 + h * hd)
        qr.append(((q * cosf + pltpu.roll(q, hd // 2, 1) * sinf) * (RET_DK ** -0.5)).astype(BF16))
        kr.append(k * cosf + pltpu.roll(k, hd // 2, 1) * sinf)
        v.append(slab(su, 2048 + h * hd).astype(BF16))
    scores = [_dot_nt(qr[k], kr[k]) * dmat_ref[h] for k, (_, h) in enumerate(units)]
    inter = [[_dot(qr[k][rw], st_ref[su * nb + sl, h]) for sl, rw in enumerate(lrows)]
             for k, (su, h) in enumerate(units)]
    o = [_dot(scores[k], v[k]) for k in range(len(units))]
    for k, (su, h) in enumerate(units):
        kd = kr[k] * kdec_ref[h]
        qdec = qdec_ref[h]
        for sl, rw in enumerate(lrows):
            s = su * nb + sl
            o_s = o[k][rw] + inter[k][sl] * qdec[rw]
            g = p_ref[s, :, 3072 + h * hd:3072 + (h + 1) * hd]
            o_ref[s, :, h * hd:(h + 1) * hd] = _silu(g) * _head_norm(o_s, RET_GN_EPS)
            st_ref[s, h] = st_ref[s, h] * sdec[h] + _dot_tn(kd[rw], v[k][rw])


def _retention(p_ret, st_all, l, bsz, t, offset, acc):
    nb, c = _seq_blocking(bsz, t, RET_CHUNK, SEQ_BLOCK_ROWS)
    r = nb * c
    nchunk = t // c
    heads = np.arange(RET_HEADS, dtype=np.float64)
    log_g = np.log1p(-np.exp2(-5.0 - heads))
    idx = np.arange(r) % c
    _, incl, _ = _seq_masks(nb, c)
    diff = (idx[:, None] - idx[None, :]).astype(np.float64)
    dmat = np.where(incl[None], np.exp(np.maximum(diff, 0.0)[None] * log_g[:, None, None]), 0.0)
    qdec = np.exp((idx + 1.0)[None, :] * log_g[:, None])
    kdec = np.exp((c - 1.0 - idx)[None, :] * log_g[:, None])
    qdec = np.broadcast_to(qdec[:, :, None], (RET_HEADS, r, RET_DK))
    kdec = np.broadcast_to(kdec[:, :, None], (RET_HEADS, r, RET_DK))
    sdec = tuple(float(x) for x in np.exp(c * log_g))

    half = RET_DK // 2
    freqs = ROPE_BASE ** (-jnp.arange(half, dtype=F32) / half)
    pos = jnp.float32(offset) + jnp.arange(t, dtype=F32)
    ang = pos[:, None] * freqs[None, :]
    cos = jnp.cos(ang)
    sin = jnp.sin(ang)
    cosf = jnp.tile(jnp.concatenate([cos, cos], axis=-1), (nb, 1))
    sinf = jnp.tile(jnp.concatenate([-sin, sin], axis=-1), (nb, 1))

    nq = next(n for n in ((4, 2, 1) if nb == 1 else (1,)) if bsz % (n * nb) == 0)
    nbt = nq * nb
    st_shape = (nbt, RET_HEADS, RET_DK, RET_DK)
    if st_all is None:
        st_in = jnp.zeros((bsz,) + st_shape[1:], F32)
        st_spec = pl.BlockSpec(st_shape, lambda i, j: (i, 0, 0, 0))
    else:
        st_in = st_all
        st_spec = pl.BlockSpec((None,) + st_shape, lambda i, j: (l, i, 0, 0, 0))
    out, st_new = pl.pallas_call(
        functools.partial(_ret_kernel, nq=nq, nb=nb, c=c, sdec=sdec),
        grid=(bsz // nbt, nchunk),
        in_specs=[
            pl.BlockSpec((nbt, c, RET_PROJ), lambda i, j: (i, j, 0)),
            pl.BlockSpec((r, RET_DK), lambda i, j: (j, 0)),
            pl.BlockSpec((r, RET_DK), lambda i, j: (j, 0)),
            pl.BlockSpec((RET_HEADS, r, r), lambda i, j: (0, 0, 0)),
            pl.BlockSpec((RET_HEADS, r, RET_DK), lambda i, j: (0, 0, 0)),
            pl.BlockSpec((RET_HEADS, r, RET_DK), lambda i, j: (0, 0, 0)),
            st_spec,
            pl.BlockSpec(memory_space=pl.ANY),
        ],
        out_specs=[
            pl.BlockSpec((nbt, c, 1024), lambda i, j: (i, j, 0)),
            pl.BlockSpec((None,) + st_shape, lambda i, j: (l, i, 0, 0, 0)),
        ],
        out_shape=[
            jax.ShapeDtypeStruct((bsz, t, 1024), F32),
            jax.ShapeDtypeStruct(acc.shape, F32),
        ],
        input_output_aliases={7: 1},
        compiler_params=_params(("parallel", "arbitrary")),
        name="retention",
    )(p_ret.reshape(bsz, t, RET_PROJ), cosf, sinf, jnp.asarray(dmat, F32), jnp.asarray(qdec, F32),
      jnp.asarray(kdec, F32), st_in, acc)
    return out.reshape(bsz * t, 1024), st_new


def _ssd_kernel(p_ref, cw_ref, cb_ref, dtb_ref, alog_ref, dfull_ref, norm_ref, tri_ref, mask2_ref,
                e64_ref, er_ref, conv_in_ref, st_in_ref, conv_acc_ref, st_acc_ref,
                o_ref, conv_out_ref, st_ref, ext_ref, stt_ref, *, nb, c):
    del conv_acc_ref, st_acc_ref
    r = nb * c
    hp = SSM_HEADDIM
    npair = SSM_HEADS // 2
    pairs = range(npair)
    seqs = range(nb)
    rows = [slice(s * c, (s + 1) * c) for s in seqs]
    cols = [slice(q * 2 * hp, (q + 1) * 2 * hp) for q in pairs]
    left = lax.broadcasted_iota(jnp.int32, (1, 2 * hp), 1) < hp

    @pl.when(pl.program_id(1) == 0)
    def _():
        for s in seqs:
            ext_ref[s, 5:8, :] = conv_in_ref[s]
            for q in pairs:
                both = jnp.concatenate([st_in_ref[s, 2 * q], st_in_ref[s, 2 * q + 1]], axis=0)
                stt_ref[s, q] = both.T

    for s in range(nb):
        ext_ref[s, 8:8 + c, :] = p_ref[s * c:(s + 1) * c, 1024:1024 + CONV_DIM]
    pieces = []
    for s in range(nb):
        acc = cb_ref[...] + ext_ref[s, 5:5 + c, :] * cw_ref[0:1, :]
        for j in range(1, CONV_W):
            acc = acc + ext_ref[s, 5 + j:5 + j + c, :] * cw_ref[j:j + 1, :]
        pieces.append(acc)
    for s in range(nb):
        conv_out_ref[s] = ext_ref[s, c + 5:c + 8, :]
        ext_ref[s, 0:8, :] = ext_ref[s, c:c + 8, :]
    xbc = _silu(pieces[0] if nb == 1 else jnp.concatenate(pieces, axis=0))
    xs = xbc[:, :1024]
    bm = xbc[:, 1024:1024 + SSM_GROUPS * SSM_STATE]
    cm = xbc[:, 1024 + SSM_GROUPS * SSM_STATE:]

    dt = jax.nn.softplus(p_ref[:, 1024 + CONV_DIM:] + dtb_ref[...])
    a = -jnp.exp(alog_ref[...])
    tri = tri_ref[...]
    cum = _dot_f32(tri, dt * a)
    cum_t = cum.T

    def expand(v, e_ref):
        hi = v.astype(BF16)
        r1 = v - hi.astype(F32)
        mid = r1.astype(BF16)
        lo = (r1 - mid.astype(F32)).astype(BF16)
        return jnp.dot(jnp.concatenate([hi, mid, lo], axis=1), e_ref[...], preferred_element_type=F32)

    def bd(v):
        vb = v.astype(BF16)
        zero = jnp.zeros_like(vb)
        return jnp.concatenate([jnp.where(left, vb, zero), jnp.where(left, zero, vb)], axis=0)

    def cat_rows(pieces):
        return pieces[0] if nb == 1 else jnp.concatenate(pieces, axis=0)

    cum_full = expand(cum, e64_ref)
    cum_wide = cum_full if r == hp else expand(cum, er_ref)
    xdt = xs * expand(dt, e64_ref)
    last = [cum_full[s * c + c - 1:s * c + c, :] for s in seqs]
    xw = cat_rows([xdt[rw] * jnp.exp(last[s] - cum_full[rw]) for s, rw in enumerate(rows)])
    eci = jnp.exp(cum_full)
    mask2 = mask2_ref[...] > 0.5

    hg = SSM_HEADS // SSM_GROUPS
    bm_g = [bm[:, g * SSM_STATE:(g + 1) * SSM_STATE] for g in range(SSM_GROUPS)]
    cm_g = [cm[:, g * SSM_STATE:(g + 1) * SSM_STATE] for g in range(SSM_GROUPS)]
    cb2 = []
    for g in range(SSM_GROUPS):
        cb = _dot_nt(cm_g[g], bm_g[g])
        cb2.append(jnp.concatenate([cb, cb], axis=1))
    grp = [2 * q // hg for q in pairs]
    inter = [cat_rows([_dot(cm_g[grp[q]][rw], stt_ref[s, q]) for s, rw in enumerate(rows)]) for q in pairs]
    sc = []
    for q in pairs:
        row2 = jnp.concatenate([cum_t[2 * q:2 * q + 1, :], cum_t[2 * q + 1:2 * q + 2, :]], axis=1)
        seg = cum_wide[:, q * 2 * r:(q + 1) * 2 * r] - row2
        sc.append(cb2[grp[q]] * jnp.where(mask2, jnp.exp(seg), 0.0))
    y_p = [jnp.dot(sc[q].astype(BF16), bd(xdt[:, cols[q]]), preferred_element_type=F32)
           + inter[q] * eci[:, cols[q]] for q in pairs]
    for q in pairs:
        for s, rw in enumerate(rows):
            upd = _dot_tn(bm_g[grp[q]][rw], xw[rw, cols[q]])
            stt_ref[s, q] = stt_ref[s, q] * jnp.exp(last[s][:, cols[q]]) + upd
    y = jnp.concatenate(y_p, axis=1) + dfull_ref[...] * xs
    z = p_ref[:, :1024]
    o_ref[...] = _rms(y * _silu(z), norm_ref[...])

    @pl.when(pl.program_id(1) == pl.num_programs(1) - 1)
    def _():
        for s in seqs:
            for q in pairs:
                both = stt_ref[s, q].T
                st_ref[s, 2 * q] = both[:hp]
                st_ref[s, 2 * q + 1] = both[hp:]


def _ssd(p_ssm, conv_all, st_all, l, bsz, t, prm, conv_acc, st_acc):
    nb, c = _seq_blocking(bsz, t, SSD_CHUNK, SEQ_BLOCK_ROWS)
    r = nb * c
    nchunk = t // c
    _, incl, _ = _seq_masks(nb, c)
    tri = jnp.asarray(incl.astype(np.float32))
    mask2 = jnp.asarray(np.tile(incl.astype(np.float32), (1, 2)))

    def expander(width):
        e = np.zeros((LANES, SSM_HEADS * width), np.float32)
        for h in range(SSM_HEADS):
            e[h, h * width:(h + 1) * width] = 1.0
        return jnp.asarray(np.concatenate([e, e, e], axis=0), BF16)

    e64 = expander(SSM_HEADDIM)
    e_r = expander(r)
    st_shape = (nb, SSM_HEADS, SSM_HEADDIM, SSM_STATE)
    cv_shape = (nb, CONV_W - 1, CONV_DIM)
    if st_all is None:
        st_in = jnp.zeros((bsz,) + st_shape[1:], F32)
        conv_in = jnp.zeros((bsz,) + cv_shape[1:], F32)
        st_spec = pl.BlockSpec(st_shape, lambda i, j: (i, 0, 0, 0))
        cv_spec = pl.BlockSpec(cv_shape, lambda i, j: (i, 0, 0))
    else:
        st_in, conv_in = st_all, conv_all
        st_spec = pl.BlockSpec((None,) + st_shape, lambda i, j: (l, i, 0, 0, 0))
        cv_spec = pl.BlockSpec((None,) + cv_shape, lambda i, j: (l, i, 0, 0))
    m = bsz * t

    def vec(n):
        return pl.BlockSpec((None, 1, n), lambda i, j: (l, 0, 0))

    return pl.pallas_call(
        functools.partial(_ssd_kernel, nb=nb, c=c),
        grid=(bsz // nb, nchunk),
        in_specs=[
            pl.BlockSpec((r, SSM_PROJ_PAD), lambda i, j: (i * nchunk + j, 0)),
            pl.BlockSpec((None, CONV_W, CONV_DIM), lambda i, j: (l, 0, 0)),
            vec(CONV_DIM), vec(LANES), vec(LANES), vec(1024), vec(1024),
            pl.BlockSpec((r, r), lambda i, j: (0, 0)),
            pl.BlockSpec((r, 2 * r), lambda i, j: (0, 0)),
            pl.BlockSpec(e64.shape, lambda i, j: (0, 0)),
            pl.BlockSpec(e_r.shape, lambda i, j: (0, 0)),
            cv_spec, st_spec,
            pl.BlockSpec(memory_space=pl.ANY), pl.BlockSpec(memory_space=pl.ANY),
        ],
        out_specs=[
            pl.BlockSpec((r, 1024), lambda i, j: (i * nchunk + j, 0)),
            pl.BlockSpec((None,) + cv_shape, lambda i, j: (l, i, 0, 0)),
            pl.BlockSpec((None,) + st_shape, lambda i, j: (l, i, 0, 0, 0)),
        ],
        out_shape=[
            jax.ShapeDtypeStruct((m, 1024), F32),
            jax.ShapeDtypeStruct(conv_acc.shape, F32),
            jax.ShapeDtypeStruct(st_acc.shape, F32),
        ],
        input_output_aliases={13: 1, 14: 2},
        scratch_shapes=[pltpu.VMEM((nb, 8 + c, CONV_DIM), F32),
                        pltpu.VMEM((nb, SSM_HEADS // 2, SSM_STATE, 2 * SSM_HEADDIM), F32)],
        compiler_params=_params(("parallel", "arbitrary")),
        name="ssd",
    )(p_ssm, prm["ssm_conv_w"], prm["ssm_conv_b"], prm["ssm_dt_bias"], prm["ssm_a_log"],
      prm["ssm_d"], prm["ssm_norm"], tri, mask2, e64, e_r, conv_in, st_in, conv_acc, st_acc)


def _rwkv_kernel(p_ref, mu_ref, w0_ref, ww2_ref, a0_ref, wa2_ref, wg2_ref, kk_ref, ka_ref,
                 rk_ref, lng_ref, lnb_ref, masks_ref, tri_ref, shift_in_ref, st_in_ref, shift_acc_ref,
                 st_acc_ref, o_ref, shift_out_ref, st_ref, ext_ref, bd_ref, *, nq, nb, c):
    del shift_acc_ref, st_acc_ref
    r = nb * c
    hd = RWKV_HD
    npair = RWKV_HEADS // 2
    seqs = range(nq * nb)
    rows = [slice(s * c, (s + 1) * c) for s in seqs]
    lane = lax.broadcasted_iota(jnp.int32, (1, 2 * hd), 1)
    left = lane < hd
    sub = lax.broadcasted_iota(jnp.int32, (2 * hd, 1), 0)
    bd_mask = (sub < hd) == left

    @pl.when(pl.program_id(1) == 0)
    def _():
        zero = jnp.zeros((hd, hd), F32)
        for s in seqs:
            ext_ref[s, 7:8, :] = shift_in_ref[s]
            for p in range(npair):
                top = jnp.concatenate([st_in_ref[s, 2 * p], zero], axis=1)
                bot = jnp.concatenate([zero, st_in_ref[s, 2 * p + 1]], axis=1)
                bd_ref[s, p] = jnp.concatenate([top, bot], axis=0)

    x = p_ref[...].reshape(nq * r, RWKV_PROJ)
    for s in seqs:
        ext_ref[s, 8:8 + c, :] = x[rows[s]]
    prev = [ext_ref[s, 7:7 + c, :] for s in seqs]
    prev = prev[0] if len(prev) == 1 else jnp.concatenate(prev, axis=0)
    for s in seqs:
        ext_ref[s, 0:8, :] = ext_ref[s, c:c + 8, :]
        shift_out_ref[s] = ext_ref[s, 7:8, :]

    mixed = x + (prev - x) * mu_ref[...]
    rr = mixed[:, 0:1024]
    kc = mixed[:, 1024:2048]
    vc = mixed[:, 2048:3072]
    wd = mixed[:, 3072:3136]
    ad = mixed[:, 3136:3200]
    gd = mixed[:, 3200:3328]
    w_log = -jax.nn.softplus(-(w0_ref[...] + _dot(jnp.tanh(wd), ww2_ref[...]))) - 0.5
    lw = -jnp.exp(w_log)
    a_lr = jax.nn.sigmoid(a0_ref[...] + _dot(ad, wa2_ref[...]))
    g_rw = _dot(jax.nn.sigmoid(gd), wg2_ref[...])
    kk = kc * kk_ref[...]
    kmod = kc * (1.0 + (a_lr - 1.0) * ka_ref[...])
    rkk = rr * kmod * rk_ref[...]

    m_incl = masks_ref[0]
    m_strict = masks_ref[1]
    m_blk = masks_ref[2]
    eye = masks_ref[3]
    cl = _dot_f32(tri_ref[...], lw)

    def half_sum(a):
        sl = jnp.sum(jnp.where(left, a, 0.0), axis=-1, keepdims=True)
        sr = jnp.sum(jnp.where(left, 0.0, a), axis=-1, keepdims=True)
        return jnp.where(left, sl, sr)

    def bd(a):
        ab = a.astype(BF16)
        zero = jnp.zeros_like(ab)
        return jnp.concatenate([jnp.where(left, ab, zero), jnp.where(left, zero, ab)], axis=0)

    def lp_dot(a, b):
        return jnp.dot(a.astype(BF16), bd(b), preferred_element_type=F32)

    def cat_rows(pieces):
        return pieces[0] if nb == 1 else jnp.concatenate(pieces, axis=0)

    units = [(su, p) for su in range(nq) for p in range(npair)]
    pairs = range(len(units))
    ucols = [slice(p * 2 * hd, (p + 1) * 2 * hd) for _, p in units]
    urows = [slice(su * r, (su + 1) * r) for su, _ in units]
    lrows = [slice(sl * c, (sl + 1) * c) for sl in range(nb)]
    v_p = [vc[ur, cs] for ur, cs in zip(urows, ucols)]
    kt, rt, kh, bh, e_last = [], [], [], [], []
    for ur, cs in zip(urows, ucols):
        kk_p = kk[ur, cs]
        kk_p = kk_p * lax.rsqrt(jnp.maximum(half_sum(kk_p * kk_p), 1e-24))
        cl_p = cl[ur, cs]
        e_incl = jnp.exp(cl_p)
        e_inv = jnp.exp(-cl_p)
        rt.append(rr[ur, cs] * e_incl)
        kt.append(kk_p * jnp.exp(cl_p - lw[ur, cs]))
        kh.append(kmod[ur, cs] * e_inv)
        bh.append(kk_p * a_lr[ur, cs] * e_inv)
        e_last.append([e_incl[sl * c + c - 1:sl * c + c, :] for sl in range(nb)])
    lhs = [jnp.concatenate([kt[p], rt[p]], axis=0).astype(BF16) for p in pairs]
    gk = [_dot_nt(lhs[p], bd(kh[p])) for p in pairs]
    gb = [_dot_nt(lhs[p], bd(bh[p])) for p in pairs]
    both = [[_dot_nt(jnp.concatenate([kt[k][rw], rt[k][rw]], axis=0) if nb > 1 else lhs[k],
                     bd_ref[su * nb + sl, p]) for sl, rw in enumerate(lrows)]
            for k, (su, p) in enumerate(units)]
    a_kk = [g[:r] * m_strict for g in gk]
    a_rk = [g[r:] * m_incl for g in gk]
    a_kb = [g[:r] * m_strict for g in gb]
    a_rb = [g[r:] * m_incl for g in gb]
    akv = [lp_dot(a_kk[p], v_p[p]) for p in pairs]

    n1 = [a * m_blk for a in a_kb]
    n2 = [lp_dot(n, n) for n in n1]
    p1 = [lp_dot(eye - n1[p], eye + n2[p]) for p in pairs]
    n4 = [lp_dot(n, n) for n in n2]
    p2 = [lp_dot(p1[p], eye + n4[p]) for p in pairs]
    n8 = [lp_dot(n, n) for n in n4]
    dinv = [lp_dot(p2[p], eye + n8[p]) for p in pairs]
    xm = [lp_dot(dinv[p], a_kb[p] - n1[p]) for p in pairs]
    x2 = [lp_dot(x_, x_) for x_ in xm]
    t1 = [lp_dot(eye - xm[p], eye + x2[p]) for p in pairs]
    tinv = [lp_dot(t1[p], dinv[p]) for p in pairs]

    ks = [cat_rows([b[:c] for b in both[p]]) for p in pairs]
    rs = [cat_rows([b[c:] for b in both[p]]) for p in pairs]
    u = [lp_dot(tinv[p], ks[p] + akv[p]) for p in pairs]
    y = [rs[p] + lp_dot(a_rk[p], v_p[p]) - lp_dot(a_rb[p], u[p]) for p in pairs]
    for k, (su, p) in enumerate(units):
        for sl, rw in enumerate(lrows):
            s = su * nb + sl
            upd = _dot_tn(jnp.concatenate([v_p[k][rw], -u[k][rw]], axis=0),
                          jnp.concatenate([kh[k][rw], bh[k][rw]], axis=0))
            bd_ref[s, p] = (bd_ref[s, p] + jnp.where(bd_mask, upd, 0.0)) * e_last[k][sl]
    for k, (su, p) in enumerate(units):
        ur, cs = urows[k], ucols[k]
        bonus = half_sum(rkk[ur, cs]) * v_p[k]
        yc = y[k] - half_sum(y[k]) * (1.0 / hd)
        yn = yc * lax.rsqrt(half_sum(yc * yc) * (1.0 / hd) + RWKV_GN_EPS)
        out = (yn * lng_ref[:, cs] + lnb_ref[:, cs] + bonus) * g_rw[ur, cs]
        o_ref[su * nb:(su + 1) * nb, :, cs] = out.reshape(nb, c, 2 * hd)

    @pl.when(pl.program_id(1) == pl.num_programs(1) - 1)
    def _():
        for s in seqs:
            for p in range(npair):
                blk = bd_ref[s, p]
                st_ref[s, 2 * p] = blk[:hd, :hd]
                st_ref[s, 2 * p + 1] = blk[hd:, hd:]


def _rwkv(p_rwkv, shift_all, st_all, l, bsz, t, prm, shift_acc, st_acc):
    nb, c = _seq_blocking(bsz, t, RWKV_CHUNK, SEQ_BLOCK_ROWS)
    r = nb * c
    nchunk = t // c
    assert 2 * r == LANES, "the lane-paired layout holds two (r, r) matrices side by side"
    nq = next(n for n in ((4, 2, 1) if nb == 1 else (2, 1)) if bsz % (n * nb) == 0)
    nbt = nq * nb
    _, incl, strict = _seq_masks(nb, c)
    tri = jnp.asarray(_seq_masks(nbt, c)[1].astype(np.float32))
    i = np.arange(r)
    blk = (i[:, None] // RWKV_DIAG_BLOCK) == (i[None, :] // RWKV_DIAG_BLOCK)
    masks = np.stack([incl, strict, blk, np.eye(r, dtype=bool)]).astype(np.float32)
    masks = jnp.asarray(np.tile(masks, (1, 1, 2)))
    st_shape = (nbt, RWKV_HEADS, RWKV_HD, RWKV_HD)
    sh_shape = (nbt, 1, RWKV_PROJ)
    if st_all is None:
        st_in = jnp.zeros((bsz,) + st_shape[1:], F32)
        shift_in = jnp.zeros((bsz,) + sh_shape[1:], F32)
        st_spec = pl.BlockSpec(st_shape, lambda i, j: (i, 0, 0, 0))
        sh_spec = pl.BlockSpec(sh_shape, lambda i, j: (i, 0, 0))
    else:
        st_in, shift_in = st_all, shift_all
        st_spec = pl.BlockSpec((None,) + st_shape, lambda i, j: (l, i, 0, 0, 0))
        sh_spec = pl.BlockSpec((None,) + sh_shape, lambda i, j: (l, i, 0, 0))

    def vec(n):
        return pl.BlockSpec((None, 1, n), lambda i, j: (l, 0, 0))

    def mat(k):
        return pl.BlockSpec((None, k, 1024), lambda i, j: (l, 0, 0))

    out, shift_new, st_new = pl.pallas_call(
        functools.partial(_rwkv_kernel, nq=nq, nb=nb, c=c),
        grid=(bsz // nbt, nchunk),
        in_specs=[
            pl.BlockSpec((nbt, c, RWKV_PROJ), lambda i, j: (i, j, 0)),
            vec(RWKV_PROJ), vec(1024), mat(64), vec(1024), mat(64), mat(128),
            vec(1024), vec(1024), vec(1024), vec(1024), vec(1024),
            pl.BlockSpec((4, r, 2 * r), lambda i, j: (0, 0, 0)),
            pl.BlockSpec((nq * r, nq * r), lambda i, j: (0, 0)),
            sh_spec, st_spec,
            pl.BlockSpec(memory_space=pl.ANY), pl.BlockSpec(memory_space=pl.ANY),
        ],
        out_specs=[
            pl.BlockSpec((nbt, c, 1024), lambda i, j: (i, j, 0)),
            pl.BlockSpec((None,) + sh_shape, lambda i, j: (l, i, 0, 0)),
            pl.BlockSpec((None,) + st_shape, lambda i, j: (l, i, 0, 0, 0)),
        ],
        out_shape=[
            jax.ShapeDtypeStruct((bsz, t, 1024), F32),
            jax.ShapeDtypeStruct(shift_acc.shape, F32),
            jax.ShapeDtypeStruct(st_acc.shape, F32),
        ],
        input_output_aliases={16: 1, 17: 2},
        scratch_shapes=[pltpu.VMEM((nbt, 8 + c, RWKV_PROJ), F32),
                        pltpu.VMEM((nbt, RWKV_HEADS // 2, 2 * RWKV_HD, 2 * RWKV_HD), F32)],
        compiler_params=_params(("parallel", "arbitrary")),
        name="rwkv7",
    )(p_rwkv.reshape(bsz, t, RWKV_PROJ), prm["rwkv_mu"], prm["rwkv_w0"], prm["rwkv_w_w2"], prm["rwkv_a0"],
      prm["rwkv_w_a2"], prm["rwkv_w_g2"], prm["rwkv_k_k"], prm["rwkv_k_a"], prm["rwkv_r_k"], prm["rwkv_lnx_g"],
      prm["rwkv_lnx_b"], masks, tri, shift_in, st_in, shift_acc, st_acc)
    return out.reshape(bsz * t, 1024), shift_new, st_new


def _merge_kernel(a_ref, b_ref, c_ref, g_ref, x_ref, wb_ref, wo_ref, qn_ref, wq_ref, o_ref, q_ref):
    merged = None
    for i, br in enumerate((a_ref, b_ref, c_ref)):
        gate = jax.nn.sigmoid(g_ref[:, i * 1024:(i + 1) * 1024])
        term = gate * jnp.dot(br[...].astype(BF16), wb_ref[i], preferred_element_type=F32)
        merged = term if merged is None else merged + term
    x = x_ref[...] + jnp.dot(merged.astype(BF16), wo_ref[...], preferred_element_type=F32)
    o_ref[...] = x
    q_ref[...] = jnp.dot(_rms(x, qn_ref[...]).astype(BF16), wq_ref[...], preferred_element_type=F32)


def _merge(out_a, out_b, out_c, gates, x, l, wb, wo, qn, wq):
    m, d = x.shape
    tm = min(m, 512)
    row = lambda i: (i, 0)
    once = pl.Buffered(1)
    return pl.pallas_call(
        _merge_kernel,
        grid=(m // tm,),
        in_specs=[
            pl.BlockSpec((tm, d), row), pl.BlockSpec((tm, d), row), pl.BlockSpec((tm, d), row),
            pl.BlockSpec((tm, GATE_PROJ), row), pl.BlockSpec((tm, d), row),
            pl.BlockSpec((None, 3, d, d), lambda i: (l, 0, 0, 0), pipeline_mode=once),
            pl.BlockSpec((None, d, d), lambda i: (l, 0, 0), pipeline_mode=once),
            pl.BlockSpec((None, 1, d), lambda i: (l, 0, 0)),
            pl.BlockSpec((None, d, d), lambda i: (l, 0, 0), pipeline_mode=once),
        ],
        out_specs=[pl.BlockSpec((tm, d), row), pl.BlockSpec((tm, d), row)],
        out_shape=[jax.ShapeDtypeStruct((m, d), F32), jax.ShapeDtypeStruct((m, d), F32)],
        compiler_params=_params(("parallel",)),
        name="merge",
    )(out_a, out_b, out_c, gates, x, wb, wo, qn, wq)


def _attn_kernel(q_ref, k_ref, v_ref, o_ref, *, nb, c):
    rows = [slice(s * c, (s + 1) * c) for s in range(nb)]
    sc = [_dot_nt(q_ref[rows[s], :], k_ref[s]) * (X_HEAD_DIM ** -0.5) for s in range(nb)]
    e = [jnp.exp(x - jnp.max(x, axis=-1, keepdims=True)) for x in sc]
    pr = [x * (1.0 / jnp.sum(x, axis=-1, keepdims=True)) for x in e]
    for s in range(nb):
        o_ref[rows[s], :] = _dot(pr[s], v_ref[s])


def _attn_cache_kernel(q_ref, k_hbm, v_hbm, o_ref, kbuf, vbuf, sem, *, l, nb, c):
    nh = pl.num_programs(1)
    step = pl.program_id(0) * nh + pl.program_id(1)
    nsteps = pl.num_programs(0) * nh
    slot = step % 2

    def copies(at_step, sl):
        blk, h = at_step // nh, at_step % nh
        src = (l, pl.ds(blk * nb, nb), slice(None), h, slice(None))
        return (pltpu.make_async_copy(k_hbm.at[src], kbuf.at[sl], sem.at[0, sl]),
                pltpu.make_async_copy(v_hbm.at[src], vbuf.at[sl], sem.at[1, sl]))

    @pl.when(step == 0)
    def _():
        for cp in copies(step, slot):
            cp.start()

    @pl.when(step + 1 < nsteps)
    def _():
        for cp in copies(step + 1, 1 - slot):
            cp.start()

    for cp in copies(step, slot):
        cp.wait()
    _attn_kernel(q_ref, kbuf.at[slot], vbuf.at[slot], o_ref, nb=nb, c=c)


def _attn_cache(q, ck, cv, l, bsz, t):
    nb, c = max(1, min(bsz, CACHE_BLOCK_ROWS // t)), t
    r = nb * c
    hd = X_HEAD_DIM
    return pl.pallas_call(
        functools.partial(_attn_cache_kernel, l=l, nb=nb, c=c),
        grid=(bsz // nb, X_HEADS),
        in_specs=[pl.BlockSpec((r, hd), lambda i, h: (i, h)),
                  pl.BlockSpec(memory_space=pl.ANY), pl.BlockSpec(memory_space=pl.ANY)],
        out_specs=pl.BlockSpec((r, hd), lambda i, h: (i, h)),
        out_shape=jax.ShapeDtypeStruct((bsz * t, 1024), F32),
        scratch_shapes=[pltpu.VMEM((2, nb, N_MEM, hd), F32), pltpu.VMEM((2, nb, N_MEM, hd), F32),
                        pltpu.SemaphoreType.DMA((2, 2))],
        compiler_params=_params(("arbitrary", "arbitrary")),
        name="xattn_cache",
    )(q, ck, cv)


def _attn(q, mk, mv, bsz, t):
    if t > ATTN_ROWS:
        nb, c = 1, ATTN_ROWS
    else:
        nb, c = max(1, min(bsz, SEQ_BLOCK_ROWS // t)), t
    r = nb * c
    ntile = t // c
    hd = X_HEAD_DIM
    kv_spec = pl.BlockSpec((nb, N_MEM, hd), lambda i, j, h: (i, 0, h))
    return pl.pallas_call(
        functools.partial(_attn_kernel, nb=nb, c=c),
        grid=(bsz // nb, ntile, X_HEADS),
        in_specs=[pl.BlockSpec((r, hd), lambda i, j, h: (i * ntile + j, h)), kv_spec, kv_spec],
        out_specs=pl.BlockSpec((r, hd), lambda i, j, h: (i * ntile + j, h)),
        out_shape=jax.ShapeDtypeStruct((bsz * t, 1024), F32),
        compiler_params=_params(("parallel", "arbitrary", "arbitrary")),
        name="xattn_core",
    )(q, mk, mv)


def _trunk(x, prm, states, mem_k, mem_v, bsz, t, offset):
    acc_ret = jnp.zeros((DEPTH, bsz, RET_HEADS, RET_DK, RET_DK), F32)
    acc_ssm = jnp.zeros((DEPTH, bsz, SSM_HEADS, SSM_HEADDIM, SSM_STATE), F32)
    acc_conv = jnp.zeros((DEPTH, bsz, CONV_W - 1, CONV_DIM), F32)
    acc_rwkv = jnp.zeros((DEPTH, bsz, RWKV_HEADS, RWKV_HD, RWKV_HD), F32)
    acc_shift = jnp.zeros((DEPTH, bsz, 1, RWKV_PROJ), F32)
    for l in range(DEPTH):
        x = _ffn(x, prm["ffa_norm"], l, prm["ffa_w1"], prm["ffa_w3"], prm["ffa_w2"])
        p_ret, p_gate = _norm_matmul(x, prm["mix_norm"], l, [prm["w_ret"], prm["w_gate"]])
        p_ssm, p_rwkv = _norm_matmul(x, prm["mix_norm"], l, [prm["w_ssm"], prm["w_rwkv"]])
        if states is None:
            s_ret = s_ssm = s_conv = s_rwkv = s_shift = None
        else:
            s_ret, s_ssm, s_conv, s_rwkv, s_shift = states
        out_a, acc_ret = _retention(p_ret, s_ret, l, bsz, t, offset, acc_ret)
        out_b, acc_conv, acc_ssm = _ssd(p_ssm, s_conv, s_ssm, l, bsz, t, prm, acc_conv, acc_ssm)
        out_c, acc_shift, acc_rwkv = _rwkv(p_rwkv, s_shift, s_rwkv, l, bsz, t, prm, acc_shift, acc_rwkv)
        x, q = _merge(out_a, out_b, out_c, p_gate, x, l, prm["w_branch"], prm["w_out"],
                      prm["xattn_norm"], prm["xattn_wq"])
        if isinstance(mem_k, (list, tuple)):
            att = _attn(q, mem_k[l], mem_v[l], bsz, t)
        else:
            att = _attn_cache(q, mem_k, mem_v, l, bsz, t)
        x = _ffn(x, prm["ffb_norm"], l, prm["ffb_w1"], prm["ffb_w3"], prm["ffb_w2"], att=att, wo=prm["xattn_wo"],
                 final_g=prm["final_norm"] if l == DEPTH - 1 else None)
    return x, (acc_ret, acc_ssm, acc_conv, acc_rwkv, acc_shift)


def _prep_params(raw):
    p = {}
    for name in ("ffa_w1", "ffa_w3", "ffa_w2", "ffb_w1", "ffb_w3", "ffb_w2", "w_branch", "w_out",
                 "xattn_wq", "xattn_wk", "xattn_wv", "xattn_wo", "rwkv_w_w2", "rwkv_w_a2", "rwkv_w_g2"):
        p[name] = raw[name].astype(BF16)
    w_in = raw["w_in"]
    c0, c1, c2 = RET_PROJ, RET_PROJ + SSM_PROJ, RET_PROJ + SSM_PROJ + RWKV_PROJ
    p["w_ret"] = w_in[:, :, :c0].astype(BF16)
    p["w_ssm"] = jnp.pad(w_in[:, :, c0:c1], ((0, 0), (0, 0), (0, SSM_PROJ_PAD - SSM_PROJ))).astype(BF16)
    p["w_rwkv"] = w_in[:, :, c1:c2].astype(BF16)
    p["w_gate"] = w_in[:, :, c2:].astype(BF16)
    for name in ("ffa_norm", "mix_norm", "ssm_conv_b", "ssm_norm", "rwkv_mu", "rwkv_w0", "rwkv_a0",
                 "rwkv_k_k", "rwkv_k_a", "rwkv_lnx_g", "rwkv_lnx_b", "xattn_norm", "mem_norm", "ffb_norm"):
        p[name] = raw[name][:, None, :]
    p["rwkv_r_k"] = raw["rwkv_r_k"].reshape(DEPTH, 1, 1024)
    pad = ((0, 0), (0, LANES - SSM_HEADS))
    p["ssm_dt_bias"] = jnp.pad(raw["ssm_dt_bias"], pad)[:, None, :]
    p["ssm_a_log"] = jnp.pad(raw["ssm_a_log"], pad)[:, None, :]
    p["ssm_d"] = jnp.repeat(raw["ssm_d"], SSM_HEADDIM, axis=-1)[:, None, :]
    p["ssm_conv_w"] = raw["ssm_conv_w"]
    p["final_norm"] = raw["final_norm"][None, :]
    return p


def _run(x_prompt, x_sample, mem_prompt, states, cache_mem_k, cache_mem_v, raw):
    prm = _prep_params(raw)
    b, t, d = x_prompt.shape
    db, dt_, _ = x_sample.shape
    n_mem = mem_prompt.shape[1]
    mem2 = mem_prompt.reshape(b * n_mem, d)
    mkv = [_norm_matmul(mem2, prm["mem_norm"], l, [prm["xattn_wk"], prm["xattn_wv"]]) for l in range(DEPTH)]
    mk = [kv[0] for kv in mkv]
    mv = [kv[1] for kv in mkv]
    mk3 = [a.reshape(b, n_mem, d) for a in mk]
    mv3 = [a.reshape(b, n_mem, d) for a in mv]
    p_mem_k = jnp.stack(mk).reshape(DEPTH, b, n_mem, X_HEADS, X_HEAD_DIM)
    p_mem_v = jnp.stack(mv).reshape(DEPTH, b, n_mem, X_HEADS, X_HEAD_DIM)
    y_p, st_p = _trunk(x_prompt.reshape(b * t, d), prm, None, mk3, mv3, b, t, 0)

    y_s, st_s = _trunk(x_sample.reshape(db * dt_, d), prm, states, cache_mem_k, cache_mem_v, db, dt_, PAST_LEN)
    return (y_p.reshape(b, t, d), y_s.reshape(db, dt_, d)) + st_p + (p_mem_k, p_mem_v) + st_s


def kernel(x_prompt, x_sample, mem_prompt, state_ret, state_ssm, state_conv, state_rwkv, state_shift, cache_mem_k, cache_mem_v, ffa_norm, ffa_w1, ffa_w3, ffa_w2, mix_norm, w_in, ssm_conv_w, ssm_conv_b, ssm_dt_bias, ssm_a_log, ssm_d, ssm_norm, rwkv_mu, rwkv_w0, rwkv_w_w2, rwkv_a0, rwkv_w_a2, rwkv_w_g2, rwkv_k_k, rwkv_k_a, rwkv_r_k, rwkv_lnx_g, rwkv_lnx_b, w_branch, w_out, xattn_norm, mem_norm, xattn_wq, xattn_wk, xattn_wv, xattn_wo, ffb_norm, ffb_w1, ffb_w3, ffb_w2, final_norm):
    raw = dict(ffa_norm=ffa_norm, ffa_w1=ffa_w1, ffa_w3=ffa_w3, ffa_w2=ffa_w2, mix_norm=mix_norm, w_in=w_in,
               ssm_conv_w=ssm_conv_w, ssm_conv_b=ssm_conv_b, ssm_dt_bias=ssm_dt_bias, ssm_a_log=ssm_a_log,
               ssm_d=ssm_d, ssm_norm=ssm_norm, rwkv_mu=rwkv_mu, rwkv_w0=rwkv_w0, rwkv_w_w2=rwkv_w_w2,
               rwkv_a0=rwkv_a0, rwkv_w_a2=rwkv_w_a2, rwkv_w_g2=rwkv_w_g2, rwkv_k_k=rwkv_k_k,
               rwkv_k_a=rwkv_k_a, rwkv_r_k=rwkv_r_k, rwkv_lnx_g=rwkv_lnx_g, rwkv_lnx_b=rwkv_lnx_b,
               w_branch=w_branch, w_out=w_out, xattn_norm=xattn_norm, mem_norm=mem_norm,
               xattn_wq=xattn_wq, xattn_wk=xattn_wk, xattn_wv=xattn_wv, xattn_wo=xattn_wo,
               ffb_norm=ffb_norm, ffb_w1=ffb_w1, ffb_w3=ffb_w3, ffb_w2=ffb_w2, final_norm=final_norm)
    states = (state_ret, state_ssm, state_conv, state_rwkv, state_shift)
    return _run(x_prompt, x_sample, mem_prompt, states, cache_mem_k, cache_mem_v, raw)
```
